```python
import math
import jax, jax.numpy as jnp
from jax import lax
import numpy as np

D_MODEL = 1024
BATCH = 32
SEQ = 2048
DEPTH = 2

N_META = 16
DN_ALPHA = (2 * DEPTH) ** 0.25
DN_BETA = (8 * DEPTH) ** -0.25

A_HEADS = 4
A_HEAD_DIM = D_MODEL // 8
A_OUT = A_HEADS * A_HEAD_DIM
A_KV_RANK = D_MODEL // 4
IDX_HEADS = 8
IDX_DIM = D_MODEL // 16
IDX_TOPK_CAP = 256
Q_BLOCK = 128

RG_WIDTH = D_MODEL // 2
RG_BLOCKS = 8
RG_C = 8.0
CONV_WIDTH = 4

GDN_K_HEADS = 8
GDN_V_HEADS = 16
GDN_HEAD_DIM = D_MODEL // 8
GDN_QK_WIDTH = GDN_K_HEADS * GDN_HEAD_DIM
GDN_V_WIDTH = GDN_V_HEADS * GDN_HEAD_DIM
GDN_CONV_CH = 2 * GDN_QK_WIDTH + GDN_V_WIDTH
CHUNK = 64

MOE_GROUPS = 4
MOE_PER_GROUP = 8
N_EXPERTS = MOE_GROUPS * MOE_PER_GROUP
MOE_TOPK = 2
EXPERT_FF = D_MODEL // 2
MOE_BLOCK = 256

EVEN_PARTS = (A_OUT, A_KV_RANK, IDX_HEADS * IDX_DIM, IDX_DIM, IDX_HEADS, RG_WIDTH, RG_WIDTH)
EVEN_IN_WIDTH = sum(EVEN_PARTS)
EVEN_MIX_WIDTH = A_OUT + RG_WIDTH
ODD_PARTS = (GDN_CONV_CH, GDN_V_WIDTH, GDN_V_HEADS, GDN_V_HEADS)
ODD_IN_WIDTH = sum(ODD_PARTS)

kernel_name = 'hybrid_dsa_rglru_gdn_hiermoe_deepnorm'

F32 = jnp.float32


def _offsets(parts):
    out, s = [], 0
    for p in parts[:-1]:
        s += p
        out.append(s)
    return out


def layer_norm(x, g, b, eps=1e-5):
    xf = x.astype(F32)
    mu = jnp.mean(xf, -1, keepdims=True)
    var = jnp.mean(jnp.square(xf - mu), -1, keepdims=True)
    return ((xf - mu) * lax.rsqrt(var + eps) * g.astype(F32) + b.astype(F32)).astype(x.dtype)


def rms_norm(x, g, eps=1e-6):
    xf = x.astype(F32)
    y = xf * lax.rsqrt(jnp.mean(jnp.square(xf), -1, keepdims=True) + eps)
    return (y * g.astype(F32)).astype(x.dtype)


def l2_normalize(x, eps=1e-6):
    xf = x.astype(F32)
    return xf * lax.rsqrt(jnp.sum(jnp.square(xf), -1, keepdims=True) + eps)


def causal_depthwise_conv(x, w):
    c = x.shape[-1]
    return lax.conv_general_dilated(
        x, w[:, None, :].astype(x.dtype), window_strides=(1,),
        padding=[(w.shape[0] - 1, 0)], dimension_numbers=('NWC', 'WIO', 'NWC'),
        feature_group_count=c)


def dsa_attention(q, k, v, q_idx, k_idx, w_idx):
    B, T, H, dh = q.shape
    n_sel = min(IDX_TOPK_CAP, T // 4)
    nb = -(-T // Q_BLOCK)
    tp = nb * Q_BLOCK

    def to_blocks(a):
        a = jnp.pad(a, [(0, 0), (0, tp - T)] + [(0, 0)] * (a.ndim - 2))
        return a.reshape(B, nb, Q_BLOCK, *a.shape[2:]).swapaxes(0, 1)

    q_pos = jnp.arange(tp, dtype=jnp.int32).reshape(nb, Q_BLOCK)
    key_pos = jnp.arange(T, dtype=jnp.int32)
    k_idx_f = k_idx.astype(F32)

    def block(args):
        qb, qib, wb, pb = args
        s = jnp.einsum('bqhd,bsd->bqhs', qib.astype(F32), k_idx_f) * IDX_DIM ** -0.5
        score = jnp.einsum('bqh,bqhs->bqs', wb.astype(F32), jax.nn.relu(s))
        visible = key_pos[None, :] <= pb[:, None]
        score = jnp.where(visible[None], score, -jnp.inf)
        _, sel = lax.top_k(score, n_sel)
        k_sel = jax.vmap(lambda kk, ii: kk[ii])(k, sel)
        v_sel = jax.vmap(lambda vv, ii: vv[ii])(v, sel)
        logits = jnp.einsum('bqhd,bqkd->bqhk', qb, k_sel).astype(F32) * dh ** -0.5
        valid = sel <= pb[None, :, None]
        logits = jnp.where(valid[:, :, None, :], logits, -jnp.inf)
        p = jax.nn.softmax(logits, axis=-1).astype(v.dtype)
        return jnp.einsum('bqhk,bqkd->bqhd', p, v_sel)

    out = lax.map(block, (to_blocks(q), to_blocks(q_idx), to_blocks(w_idx), q_pos))
    return out.swapaxes(0, 1).reshape(B, tp, H, dh)[:, :T]


def rg_lru(x, w_a, b_a, w_x, b_x, lam):
    B, T, R = x.shape
    xb = x.reshape(B, T, RG_BLOCKS, R // RG_BLOCKS)
    r = jax.nn.sigmoid(jnp.einsum('btni,nij->btnj', xb, w_a).reshape(B, T, R) + b_a)
    i = jax.nn.sigmoid(jnp.einsum('btni,nij->btnj', xb, w_x).reshape(B, T, R) + b_x)
    log_a = -RG_C * r.astype(F32) * jax.nn.softplus(-lam.astype(F32))
    a = jnp.exp(log_a)
    u = jnp.sqrt(-jnp.expm1(2.0 * log_a)) * (i * x).astype(F32)

    def combine(c1, c2):
        a1, b1 = c1
        a2, b2 = c2
        return a1 * a2, a2 * b1 + b2

    _, h = lax.associative_scan(combine, (a, u), axis=1)
    return h.astype(x.dtype)


def chunked_gated_delta_rule(q, k, v, g, beta):
    B, T, H, dk = q.shape
    dv = v.shape[-1]
    n = T // CHUNK

    def blocks(a):
        a = a.astype(F32).reshape(B, n, CHUNK, H, *a.shape[3:])
        return jnp.moveaxis(a, 3, 1)

    q = blocks(q) * dk ** -0.5
    k, v, g, beta = blocks(k), blocks(v), blocks(g), blocks(beta)
    gc = jnp.cumsum(g, axis=-1)
    causal = jnp.tril(jnp.ones((CHUNK, CHUNK), bool))
    eye = jnp.eye(CHUNK, dtype=F32)
    decay = jnp.exp(jnp.where(causal, gc[..., :, None] - gc[..., None, :], -jnp.inf))
    kb = k * beta[..., None]
    m = jnp.tril(jnp.einsum('bhnid,bhnjd->bhnij', kb, k) * decay, -1)
    rhs = jnp.concatenate([v * beta[..., None], kb * jnp.exp(gc)[..., None]], axis=-1)
    sol = lax.linalg.triangular_solve(m + eye, rhs, left_side=True, lower=True)
    u, w = sol[..., :dv], sol[..., dv:]
    attn = jnp.einsum('bhnid,bhnjd->bhnij', q, k) * decay

    def step(S, xs):
        q_c, k_c, u_c, w_c, g_c, a_c = xs
        v_new = u_c - jnp.einsum('bhck,bhkv->bhcv', w_c, S)
        o = (jnp.einsum('bhck,bhkv->bhcv', q_c * jnp.exp(g_c)[..., None], S)
             + jnp.einsum('bhij,bhjv->bhiv', a_c, v_new))
        g_last = g_c[..., -1:]
        S = (S * jnp.exp(g_last)[..., None]
             + jnp.einsum('bhck,bhcv->bhkv', k_c * jnp.exp(g_last - g_c)[..., None], v_new))
        return S, o

    xs = tuple(jnp.moveaxis(a, 2, 0) for a in (q, k, u, w, gc, attn))
    S0 = jnp.zeros((B, H, dk, dv), F32)
    _, o = lax.scan(step, S0, xs)
    o = jnp.moveaxis(o, 0, 2).reshape(B, H, T, dv)
    return jnp.swapaxes(o, 1, 2)


def even_mixer(x, w_in, kv_norm, w_uk, w_uv, conv_w, conv_b, rg_wa, rg_ba, rg_wx, rg_bx, rg_lambda, w_out):
    B, T, _ = x.shape
    q, c_kv, q_idx, k_idx, w_idx, gate_b, x_b = jnp.split(x @ w_in, _offsets(EVEN_PARTS), axis=-1)
    latent = rms_norm(c_kv, kv_norm)
    k = latent @ w_uk
    v = latent @ w_uv
    attn = dsa_attention(q.reshape(B, T, A_HEADS, A_HEAD_DIM), k, v,
                         q_idx.reshape(B, T, IDX_HEADS, IDX_DIM), k_idx, w_idx * IDX_HEADS ** -0.5)
    xr = causal_depthwise_conv(x_b, conv_w) + conv_b
    rec = rg_lru(xr, rg_wa, rg_ba, rg_wx, rg_bx, rg_lambda) * jax.nn.gelu(gate_b)
    mixed = jnp.concatenate([attn.reshape(B, T, A_OUT), rec], axis=-1)
    return mixed @ w_out


def odd_mixer(x, w_in, conv_w, a_log, dt_bias, o_norm, w_out):
    B, T, _ = x.shape
    qkv, z, a, b = jnp.split(x @ w_in, _offsets(ODD_PARTS), axis=-1)
    qkv = jax.nn.silu(causal_depthwise_conv(qkv, conv_w))
    q, k, v = jnp.split(qkv, [GDN_QK_WIDTH, 2 * GDN_QK_WIDTH], axis=-1)
    rep = GDN_V_HEADS // GDN_K_HEADS
    q = jnp.repeat(l2_normalize(q.reshape(B, T, GDN_K_HEADS, GDN_HEAD_DIM)), rep, axis=2)
    k = jnp.repeat(l2_normalize(k.reshape(B, T, GDN_K_HEADS, GDN_HEAD_DIM)), rep, axis=2)
    v = v.reshape(B, T, GDN_V_HEADS, GDN_HEAD_DIM)
    g = -jnp.exp(a_log.astype(F32)) * jax.nn.softplus(a.astype(F32) + dt_bias.astype(F32))
    beta = jax.nn.sigmoid(b.astype(F32))
    front = (-N_META) % CHUNK
    back = (-(T + front)) % CHUNK

    def padt(t):
        return jnp.pad(t, [(0, 0), (front, back)] + [(0, 0)] * (t.ndim - 2))

    o = chunked_gated_delta_rule(padt(q), padt(k), padt(v), padt(g), padt(beta))[:, front:front + T]
    o = rms_norm(o.astype(x.dtype), o_norm) * jax.nn.silu(z.reshape(B, T, GDN_V_HEADS, GDN_HEAD_DIM))
    return o.reshape(B, T, GDN_V_WIDTH) @ w_out


def routed_expert_ffn(xf, expert_idx, gates, w_gate, w_up, w_down):
    N, D = xf.shape
    A = N * MOE_TOPK
    flat_e = expert_idx.reshape(A).astype(jnp.int32)
    order = jnp.argsort(flat_e)
    sorted_e = flat_e[order]
    counts = jnp.bincount(flat_e, length=N_EXPERTS)
    padded = (counts + MOE_BLOCK - 1) // MOE_BLOCK * MOE_BLOCK
    start = jnp.cumsum(counts) - counts
    pend = jnp.cumsum(padded)
    pstart = pend - padded
    slot = pstart[sorted_e] + jnp.arange(A, dtype=jnp.int32) - start[sorted_e]
    n_blocks = -(-A // MOE_BLOCK) + N_EXPERTS
    P = n_blocks * MOE_BLOCK
    src = jnp.full((P,), N, jnp.int32).at[slot].set((order // MOE_TOPK).astype(jnp.int32))
    x_pad = jnp.concatenate([xf, jnp.zeros((1, D), xf.dtype)], axis=0)
    xs = x_pad[src].reshape(n_blocks, MOE_BLOCK, D)
    block_start = jnp.arange(n_blocks, dtype=jnp.int32) * MOE_BLOCK
    block_expert = jnp.minimum(jnp.searchsorted(pend, block_start, side='right'), N_EXPERTS - 1)

    def expert_block(args):
        xb, e = args
        hid = jax.nn.silu(xb @ w_gate[e]) * (xb @ w_up[e])
        return hid @ w_down[e]

    ys = lax.map(expert_block, (xs, block_expert)).reshape(P, D)
    y_assign = jnp.zeros((A, D), ys.dtype).at[order].set(ys[slot])
    return jnp.einsum('nk,nkd->nd', gates.astype(ys.dtype), y_assign.reshape(N, MOE_TOPK, D))


def hierarchical_moe(x, group_w, group_b, expert_w, expert_b, w_gate, w_up, w_down):
    B, T, D = x.shape
    xf = x.reshape(B * T, D)
    n = xf.shape[0]
    grp_logits = (xf @ group_w + group_b).astype(F32)
    grp = jnp.argmax(grp_logits, axis=-1)
    grp_gate = jnp.take_along_axis(jax.nn.softmax(grp_logits, axis=-1), grp[:, None], axis=1)
    exp_logits = (xf @ expert_w + expert_b).astype(F32).reshape(n, MOE_GROUPS, MOE_PER_GROUP)
    exp_logits = jnp.take_along_axis(exp_logits, grp[:, None, None], axis=1)[:, 0]
    top_val, top_local = lax.top_k(exp_logits, MOE_TOPK)
    gates = jax.nn.softmax(top_val, axis=-1) * grp_gate
    expert_idx = grp[:, None].astype(jnp.int32) * MOE_PER_GROUP + top_local
    y = routed_expert_ffn(xf, expert_idx, gates, w_gate, w_up, w_down)
    return y.reshape(B, T, D)


def setup_inputs(seed: int = 0) -> dict:
    key = jax.random.key(seed)
    ks = iter(jax.random.split(key, 40))
    NE = (DEPTH + 1) // 2
    NO = DEPTH // 2

    def nrm(shape, scale):
        return jax.random.normal(next(ks), shape, F32) * scale

    def unif(shape, lo, hi):
        return jax.random.uniform(next(ks), shape, F32, lo, hi)

    rg_a = unif((NE, RG_WIDTH), 0.9, 0.999) ** (1.0 / RG_C)
    dt = jnp.exp(unif((NO, GDN_V_HEADS), math.log(1e-3), math.log(1e-1)))
    bs = RG_WIDTH // RG_BLOCKS
    return {
        'x': nrm((BATCH, SEQ, D_MODEL), 1.0),
        'meta_tokens': nrm((N_META, D_MODEL), 1.0),
        'even_w_in': nrm((NE, D_MODEL, EVEN_IN_WIDTH), D_MODEL ** -0.5),
        'even_kv_norm': 1.0 + nrm((NE, A_KV_RANK), 0.02),
        'even_w_uk': nrm((NE, A_KV_RANK, A_HEAD_DIM), A_KV_RANK ** -0.5),
        'even_w_uv': nrm((NE, A_KV_RANK, A_HEAD_DIM), A_KV_RANK ** -0.5),
        'even_conv_w': nrm((NE, CONV_WIDTH, RG_WIDTH), CONV_WIDTH ** -0.5),
        'even_conv_b': nrm((NE, RG_WIDTH), 0.02),
        'even_rg_wa': nrm((NE, RG_BLOCKS, bs, bs), bs ** -0.5),
        'even_rg_ba': nrm((NE, RG_WIDTH), 0.02),
        'even_rg_wx': nrm((NE, RG_BLOCKS, bs, bs), bs ** -0.5),
        'even_rg_bx': nrm((NE, RG_WIDTH), 0.02),
        'even_rg_lambda': jnp.log(rg_a) - jnp.log1p(-rg_a),
        'even_w_out': nrm((NE, EVEN_MIX_WIDTH, D_MODEL), EVEN_MIX_WIDTH ** -0.5 * DN_BETA),
        'odd_w_in': nrm((NO, D_MODEL, ODD_IN_WIDTH), D_MODEL ** -0.5),
        'odd_conv_w': nrm((NO, CONV_WIDTH, GDN_CONV_CH), CONV_WIDTH ** -0.5),
        'odd_a_log': jnp.log(unif((NO, GDN_V_HEADS), 1.0, 16.0)),
        'odd_dt_bias': dt + jnp.log(-jnp.expm1(-dt)),
        'odd_o_norm': 1.0 + nrm((NO, GDN_HEAD_DIM), 0.02),
        'odd_w_out': nrm((NO, GDN_V_WIDTH, D_MODEL), GDN_V_WIDTH ** -0.5 * DN_BETA),
        'ln_g': 1.0 + nrm((DEPTH, 2, D_MODEL), 0.02),
        'ln_b': nrm((DEPTH, 2, D_MODEL), 0.02),
        'moe_group_w': nrm((DEPTH, D_MODEL, MOE_GROUPS), D_MODEL ** -0.5),
        'moe_group_b': nrm((DEPTH, MOE_GROUPS), 0.01),
        'moe_expert_w': nrm((DEPTH, D_MODEL, N_EXPERTS), D_MODEL ** -0.5),
        'moe_expert_b': nrm((DEPTH, N_EXPERTS), 0.01),
        'moe_w_gate': nrm((DEPTH, N_EXPERTS, D_MODEL, EXPERT_FF), D_MODEL ** -0.5),
        'moe_w_up': nrm((DEPTH, N_EXPERTS, D_MODEL, EXPERT_FF), D_MODEL ** -0.5),
        'moe_w_down': nrm((DEPTH, N_EXPERTS, EXPERT_FF, D_MODEL), EXPERT_FF ** -0.5 * DN_BETA),
    }


def reference(x, meta_tokens, even_w_in, even_kv_norm, even_w_uk, even_w_uv, even_conv_w, even_conv_b,
              even_rg_wa, even_rg_ba, even_rg_wx, even_rg_bx, even_rg_lambda, even_w_out,
              odd_w_in, odd_conv_w, odd_a_log, odd_dt_bias, odd_o_norm, odd_w_out,
              ln_g, ln_b, moe_group_w, moe_group_b, moe_expert_w, moe_expert_b,
              moe_w_gate, moe_w_up, moe_w_down):
    B = x.shape[0]
    meta = jnp.broadcast_to(meta_tokens.astype(x.dtype)[None], (B, N_META, D_MODEL))
    h = jnp.concatenate([meta, x], axis=1)
    for layer in range(DEPTH):
        i = layer // 2
        if layer % 2 == 0:
            mix = even_mixer(h, even_w_in[i], even_kv_norm[i], even_w_uk[i], even_w_uv[i],
                             even_conv_w[i], even_conv_b[i], even_rg_wa[i], even_rg_ba[i],
                             even_rg_wx[i], even_rg_bx[i], even_rg_lambda[i], even_w_out[i])
        else:
            mix = odd_mixer(h, odd_w_in[i], odd_conv_w[i], odd_a_log[i], odd_dt_bias[i],
                            odd_o_norm[i], odd_w_out[i])
        h = layer_norm(DN_ALPHA * h + mix, ln_g[layer, 0], ln_b[layer, 0])
        ffn = hierarchical_moe(h, moe_group_w[layer], moe_group_b[layer], moe_expert_w[layer],
                               moe_expert_b[layer], moe_w_gate[layer], moe_w_up[layer], moe_w_down[layer])
        h = layer_norm(DN_ALPHA * h + ffn, ln_g[layer, 1], ln_b[layer, 1])
    return h[:, N_META:]
```

```python
import functools
import math

import jax
import jax.numpy as jnp
from jax import lax
from jax.experimental import pallas as pl
from jax.experimental.pallas import tpu as pltpu

F32 = jnp.float32
BF16 = jnp.bfloat16
I32 = jnp.int32

D_MODEL = 1024
DEPTH = 2
N_META = 16
DN_ALPHA = (2 * DEPTH) ** 0.25

A_HEADS = 4
A_HEAD_DIM = 128
A_OUT = A_HEADS * A_HEAD_DIM
A_KV_RANK = 256
IDX_HEADS = 8
IDX_DIM = 64
IDX_TOPK_CAP = 256

RG_WIDTH = 512
RG_BLOCKS = 8
RG_C = 8.0
CONV_WIDTH = 4

GDN_K_HEADS = 8
GDN_V_HEADS = 16
GDN_HEAD_DIM = 128
GDN_QK_WIDTH = GDN_K_HEADS * GDN_HEAD_DIM
GDN_V_WIDTH = GDN_V_HEADS * GDN_HEAD_DIM
CHUNK = 64

MOE_GROUPS = 4
MOE_PER_GROUP = 8
N_EXPERTS = MOE_GROUPS * MOE_PER_GROUP
MOE_TOPK = 2
EXPERT_FF = 512
MOE_BLOCK = 256

LANES = 128
SUBLANES = 8
TIME_TILE = 128
FRONT = (-N_META) % CHUNK
NEG_BIG = -1e30
INT_MIN = -2 ** 31
VMEM_LIMIT = 56 * 1024 * 1024


def _cparams(*sem):
    return pltpu.CompilerParams(dimension_semantics=sem, vmem_limit_bytes=VMEM_LIMIT)


def _const_spec(shape):
    nd = len(shape)
    return pl.BlockSpec(shape, lambda *_: (0,) * nd)


def _row_tile(tp):
    best = SUBLANES
    for t in range(SUBLANES, 641, SUBLANES):
        if tp % t == 0:
            best = t
    return best


def _valid_rows(p, t_real):
    return (p >= FRONT) & (p < FRONT + t_real)


def _layer_norm_rows(y, g, b):
    mu = jnp.mean(y, axis=-1, keepdims=True)
    yc = y - mu
    var = jnp.mean(yc * yc, axis=-1, keepdims=True)
    return yc * lax.rsqrt(var + 1e-5) * g + b


def _softplus(z):
    return jnp.maximum(z, 0.0) + jnp.log1p(jnp.exp(-jnp.abs(z)))


def _even_in_kernel(h_ref, wq_ref, wc_ref, wqi_ref, wki_ref, wwi_ref, wg_ref, wx_ref, kvn_ref, wukv_ref,
                    q_ref, qi_ref, ki_ref, wi_ref, gate_ref, xb_ref, kv_ref):
    a = h_ref[0].astype(BF16)
    dot = functools.partial(jnp.dot, preferred_element_type=F32)
    q_ref[0] = dot(a, wq_ref[...]).astype(BF16)
    qi_ref[0] = dot(a, wqi_ref[...]).astype(BF16)
    ki_ref[0] = dot(a, wki_ref[...]).astype(BF16)
    wi_ref[0] = dot(a, wwi_ref[...]) * (IDX_HEADS ** -0.5) * (IDX_DIM ** -0.5)
    gate_ref[0] = dot(a, wg_ref[...])
    xb_ref[0] = dot(a, wx_ref[...])
    c = dot(a, wc_ref[...])
    latent = c * lax.rsqrt(jnp.mean(c * c, axis=-1, keepdims=True) + 1e-6) * kvn_ref[...]
    kv_ref[0] = dot(latent.astype(BF16), wukv_ref[...]).astype(BF16)


def _even_in_proj(h, w_in, kv_norm, w_uk, w_uv):
    B, TP, D = h.shape
    tm = _row_tile(TP)
    o = [0]
    for p in (A_OUT, A_KV_RANK, IDX_HEADS * IDX_DIM, IDX_DIM, IDX_HEADS, RG_WIDTH, RG_WIDTH):
        o.append(o[-1] + p)
    wb = w_in.astype(BF16)
    wq, wc = wb[:, o[0]:o[1]], wb[:, o[1]:o[2]]
    wqi = wb[:, o[2]:o[3]].reshape(D, IDX_HEADS, IDX_DIM)
    wqi = jnp.pad(wqi, ((0, 0), (0, 0), (0, LANES - IDX_DIM))).reshape(D, IDX_HEADS * LANES)
    wki = jnp.pad(wb[:, o[3]:o[4]], ((0, 0), (0, LANES - IDX_DIM)))
    wwi = jnp.pad(wb[:, o[4]:o[5]], ((0, 0), (0, LANES - IDX_HEADS)))
    wg, wx = wb[:, o[5]:o[6]], wb[:, o[6]:o[7]]
    wukv = jnp.concatenate([w_uk, w_uv], axis=1).astype(BF16)
    kvn = kv_norm.reshape(1, A_KV_RANK).astype(F32)
    ws = (wq, wc, wqi, wki, wwi, wg, wx, kvn, wukv)
    outs = [(A_OUT, BF16), (IDX_HEADS * LANES, BF16), (LANES, BF16), (LANES, F32), (RG_WIDTH, F32),
            (RG_WIDTH, F32), (2 * A_HEAD_DIM, BF16)]
    return pl.pallas_call(
        _even_in_kernel,
        grid=(B, TP // tm),
        in_specs=[pl.BlockSpec((1, tm, D), lambda b, j: (b, j, 0))] + [_const_spec(w.shape) for w in ws],
        out_specs=[pl.BlockSpec((1, tm, n), lambda b, j: (b, j, 0)) for n, _ in outs],
        out_shape=[jax.ShapeDtypeStruct((B, TP, n), dt) for n, dt in outs],
        compiler_params=_cparams("parallel", "parallel"),
        name="even_in_proj",
    )(h, *ws)


def _dsa_kernel(q_ref, qi_ref, wi_ref, ki_ref, kv_ref, o_ref, key_ref, bias_ref, *, n_sel, t_real):
    i = pl.program_id(1)
    tq = q_ref.shape[1]
    tp = ki_ref.shape[1]
    qi = qi_ref[0]
    ki = ki_ref[0]
    wi = wi_ref[0]
    nt = (((1,), (1,)), ((), ()))
    score = jnp.zeros((tq, tp), F32)
    for h in range(IDX_HEADS):
        s = lax.dot_general(qi[:, h * LANES:(h + 1) * LANES], ki, nt, preferred_element_type=F32)
        score = score + wi[:, h:h + 1] * jnp.maximum(s, 0.0)

    kpos = lax.broadcasted_iota(I32, (tq, tp), 1)
    qpos = i * tq + lax.broadcasted_iota(I32, (tq, tp), 0)
    vis = (kpos <= qpos) & (kpos >= FRONT) & (kpos < FRONT + t_real)
    bits = lax.bitcast_convert_type(score, I32)
    key = jnp.where(bits < 0, bits ^ jnp.int32(0x7FFFFFFF), bits)
    key_ref[...] = jnp.where(vis, key, jnp.int32(INT_MIN))

    def count(mask):
        return jnp.sum(jnp.where(mask, 1, 0).astype(I32), axis=1, keepdims=True)

    def bit_step(it, cand):
        trial = cand + jnp.left_shift(jnp.int32(1), 31 - it)
        return jnp.where(count(key_ref[...] >= trial) >= n_sel, trial, cand)

    thr = lax.fori_loop(0, 32, bit_step, jnp.full((tq, 1), INT_MIN, I32))
    keys = key_ref[...]
    cnt_ge = count(keys >= thr)
    bias_ref[...] = jnp.where((keys >= thr) & vis, 0.0, NEG_BIG)
    tie_rows = (cnt_ge > n_sel) & (thr > jnp.int32(INT_MIN))

    @pl.when(jnp.max(jnp.where(tie_rows, 1, 0)) > 0)
    def _():
        keys = key_ref[...]
        eq = keys == thr
        need = n_sel - count(keys > thr)
        nbits = max(1, (tp - 1).bit_length())

        def idx_step(it, cand):
            trial = cand + jnp.left_shift(jnp.int32(1), nbits - 1 - it)
            return jnp.where(count(eq & (kpos < trial)) < need, trial, cand)

        last = lax.fori_loop(0, nbits, idx_step, jnp.zeros((tq, 1), I32))
        sel = (keys > thr) | (eq & (kpos <= last))
        bias_ref[...] = jnp.where(sel & vis, 0.0, NEG_BIG)

    bias = bias_ref[...]
    q = q_ref[0]
    kv = kv_ref[0]
    k = kv[:, :A_HEAD_DIM]
    v = kv[:, A_HEAD_DIM:]
    for h in range(A_HEADS):
        lg = lax.dot_general(q[:, h * A_HEAD_DIM:(h + 1) * A_HEAD_DIM], k, nt, preferred_element_type=F32)
        lg = lg * (A_HEAD_DIM ** -0.5) + bias
        m = jnp.max(lg, axis=1, keepdims=True)
        p = jnp.exp(lg - m)
        l = jnp.sum(p, axis=1, keepdims=True)
        o = jnp.dot(p.astype(BF16), v, preferred_element_type=F32) / l
        o_ref[0, :, h * A_HEAD_DIM:(h + 1) * A_HEAD_DIM] = o.astype(BF16)


def _dsa(q, qi, wi, ki, kv, t_real):
    B, TP, _ = q.shape
    tq = TIME_TILE
    n_sel = min(IDX_TOPK_CAP, t_real // 4)
    kern = functools.partial(_dsa_kernel, n_sel=n_sel, t_real=t_real)
    qspec = lambda n: pl.BlockSpec((1, tq, n), lambda b, i: (b, i, 0))
    kspec = lambda n: pl.BlockSpec((1, TP, n), lambda b, i: (b, 0, 0))
    return pl.pallas_call(
        kern,
        grid=(B, TP // tq),
        in_specs=[qspec(A_OUT), qspec(IDX_HEADS * LANES), qspec(LANES), kspec(LANES), kspec(2 * A_HEAD_DIM)],
        out_specs=qspec(A_OUT),
        out_shape=jax.ShapeDtypeStruct((B, TP, A_OUT), BF16),
        scratch_shapes=[pltpu.VMEM((tq, TP), I32), pltpu.VMEM((tq, TP), F32)],
        compiler_params=_cparams("parallel", "arbitrary"),
        name="dsa_attention",
    )(q, qi, wi, ki, kv)


def _shift_rows(x, s, fill):
    rows = lax.broadcasted_iota(I32, x.shape, 0)
    return jnp.where(rows >= s, pltpu.roll(x, s, 0), fill)


def _griffin_kernel(xb_ref, gate_ref, cw_ref, cb_ref, wg_ref, bg_ref, lam_ref, o_ref, tail_ref, car_ref, *, t_real):
    j = pl.program_id(1)
    tt = xb_ref.shape[1]

    @pl.when(j == 0)
    def _():
        tail_ref[...] = jnp.zeros_like(tail_ref)
        car_ref[...] = jnp.zeros_like(car_ref)

    x = xb_ref[0]
    win = jnp.concatenate([tail_ref[...], x], axis=0)
    tail_ref[...] = x[tt - SUBLANES:, :]
    xr = jnp.zeros_like(x) + cb_ref[...]
    for t in range(CONV_WIDTH):
        off = SUBLANES - (CONV_WIDTH - 1) + t
        xr = xr + cw_ref[t:t + 1, :] * win[off:off + tt, :]
    g = jnp.dot(xr.astype(BF16), wg_ref[...], preferred_element_type=F32) + bg_ref[...]
    r = jax.nn.sigmoid(g[:, :RG_WIDTH])
    ig = jax.nn.sigmoid(g[:, RG_WIDTH:])
    log_a = -RG_C * r * _softplus(-lam_ref[...])
    a = jnp.exp(log_a)
    th = jnp.tanh(log_a)
    u = jnp.sqrt(-2.0 * th / (1.0 - th)) * (ig * xr)
    p = j * tt + lax.broadcasted_iota(I32, x.shape, 0)
    u = jnp.where(_valid_rows(p, t_real), u, 0.0)
    s = 1
    while s < tt:
        u = u + a * _shift_rows(u, s, 0.0)
        a = a * _shift_rows(a, s, 1.0)
        s *= 2
    hcur = u + a * car_ref[0:1, :]
    car_ref[...] = jnp.broadcast_to(hcur[tt - 1:tt, :], car_ref.shape)
    o_ref[0] = hcur * jax.nn.gelu(gate_ref[0])


def _block_diag(w):
    nb, bs, _ = w.shape
    eye = jnp.eye(nb, dtype=w.dtype)
    return (w[:, :, None, :] * eye[:, None, :, None]).reshape(nb * bs, nb * bs)


def _griffin(xb, gate, conv_w, conv_b, rg_wa, rg_ba, rg_wx, rg_bx, rg_lambda, t_real):
    B, TP, R = xb.shape
    tt = TIME_TILE
    cw = jnp.pad(conv_w.astype(F32), ((0, SUBLANES - CONV_WIDTH), (0, 0)))
    cb = conv_b.reshape(1, R).astype(F32)
    wg = jnp.concatenate([_block_diag(rg_wa), _block_diag(rg_wx)], axis=1).astype(BF16)
    bg = jnp.concatenate([rg_ba, rg_bx]).reshape(1, 2 * R).astype(F32)
    lam = rg_lambda.reshape(1, R).astype(F32)
    tspec = pl.BlockSpec((1, tt, R), lambda b, j: (b, j, 0))
    ws = (cw, cb, wg, bg, lam)
    return pl.pallas_call(
        functools.partial(_griffin_kernel, t_real=t_real),
        grid=(B, TP // tt),
        in_specs=[tspec, tspec] + [_const_spec(w.shape) for w in ws],
        out_specs=tspec,
        out_shape=jax.ShapeDtypeStruct((B, TP, R), F32),
        scratch_shapes=[pltpu.VMEM((SUBLANES, R), F32), pltpu.VMEM((SUBLANES, R), F32)],
        compiler_params=_cparams("parallel", "arbitrary"),
        name="griffin_rglru",
    )(xb, gate, *ws)


def _out_ln_kernel(*refs, n_in, t_real):
    h_ref = refs[0]
    a_refs = refs[1:1 + n_in]
    w_refs = refs[1 + n_in:1 + 2 * n_in]
    g_ref, b_ref, o_ref = refs[1 + 2 * n_in:]
    j = pl.program_id(1)
    tm = h_ref.shape[1]
    y = DN_ALPHA * h_ref[0]
    for a_ref, w_ref in zip(a_refs, w_refs):
        y = y + jnp.dot(a_ref[0].astype(BF16), w_ref[...], preferred_element_type=F32)
    out = _layer_norm_rows(y, g_ref[...], b_ref[...])
    p = j * tm + lax.broadcasted_iota(I32, out.shape, 0)
    o_ref[0] = jnp.where(_valid_rows(p, t_real), out, 0.0)


def _out_ln(h, acts, ws, ln_g, ln_b, t_real):
    B, TP, D = h.shape
    tm = _row_tile(TP)
    ws = [w.astype(BF16) for w in ws]
    g = ln_g.reshape(1, D).astype(F32)
    b = ln_b.reshape(1, D).astype(F32)
    tspec = lambda n: pl.BlockSpec((1, tm, n), lambda bb, j: (bb, j, 0))
    return pl.pallas_call(
        functools.partial(_out_ln_kernel, n_in=len(acts), t_real=t_real),
        grid=(B, TP // tm),
        in_specs=[tspec(D)] + [tspec(a.shape[-1]) for a in acts] + [_const_spec(w.shape) for w in ws]
        + [_const_spec(g.shape), _const_spec(b.shape)],
        out_specs=tspec(D),
        out_shape=jax.ShapeDtypeStruct((B, TP, D), F32),
        compiler_params=_cparams("parallel", "parallel"),
        name="out_proj_ln",
    )(h, *acts, *ws, g, b)


E_LANE0, G_LANE0 = 0, N_EXPERTS
R_E1, R_E2, R_RANK1, R_RANK2, R_G1, R_G2 = range(6)


def _router_kernel(h_ref, w_ref, b_ref, rec_ref, rect_ref, cnt_ref, car_ref):
    t = pl.program_id(0)

    @pl.when(t == 0)
    def _():
        car_ref[...] = jnp.zeros_like(car_ref)

    tm = h_ref.shape[0]
    lg = jnp.dot(h_ref[...], w_ref[...], preferred_element_type=F32, precision=lax.Precision.HIGHEST) + b_ref[...]
    lane = lax.broadcasted_iota(I32, lg.shape, 1)
    neg_inf = -jnp.inf

    def first_argmax(x):
        m = jnp.max(x, axis=1, keepdims=True)
        return m, jnp.min(jnp.where(x == m, lane, jnp.int32(2 ** 30)), axis=1, keepdims=True)

    glog = jnp.where((lane >= G_LANE0) & (lane < G_LANE0 + MOE_GROUPS), lg, neg_inf)
    gmax, glane = first_argmax(glog)
    grp = glane - G_LANE0
    grp_gate = 1.0 / jnp.sum(jnp.exp(glog - gmax), axis=1, keepdims=True)
    in_grp = (lane >= grp * MOE_PER_GROUP) & (lane < (grp + 1) * MOE_PER_GROUP)
    elog = jnp.where(in_grp, lg, neg_inf)
    v1, e1 = first_argmax(elog)
    v2, e2 = first_argmax(jnp.where(lane == e1, neg_inf, elog))
    ex = jnp.exp(v2 - v1)
    g1 = grp_gate / (1.0 + ex)
    g2 = grp_gate * ex / (1.0 + ex)

    oh1 = jnp.where(lane == e1, 1.0, 0.0)
    oh2 = jnp.where(lane == e2, 1.0, 0.0)
    rows = lax.broadcasted_iota(I32, (tm, tm), 0)
    cols = lax.broadcasted_iota(I32, (tm, tm), 1)
    ltri = jnp.where(cols < rows, 1.0, 0.0).astype(BF16)
    p1 = jnp.dot(ltri, oh1.astype(BF16), preferred_element_type=F32)
    p2 = jnp.dot(ltri, oh2.astype(BF16), preferred_element_type=F32)
    car = car_ref[0:1, :]
    c1 = jnp.sum(oh1, axis=0, keepdims=True)
    c2 = jnp.sum(oh2, axis=0, keepdims=True)
    rank1 = jnp.sum(oh1 * (car + p1), axis=1, keepdims=True)
    rank2 = jnp.sum(oh2 * (car + c1 + p2), axis=1, keepdims=True)
    car = car + c1 + c2
    car_ref[...] = jnp.broadcast_to(car, car_ref.shape)
    cnt_ref[...] = jnp.broadcast_to(car, cnt_ref.shape)

    rec = jnp.zeros(lg.shape, F32)
    for ln, val in ((R_E1, e1.astype(F32)), (R_E2, e2.astype(F32)), (R_RANK1, rank1), (R_RANK2, rank2),
                    (R_G1, g1), (R_G2, g2)):
        rec = jnp.where(lane == ln, val, rec)
    rec_ref[...] = rec
    rect_ref[0] = rec.T[:SUBLANES, :]


def _moe_route(hf, group_w, group_b, expert_w, expert_b):
    N, D = hf.shape
    tm = MOE_BLOCK
    nt = N // tm
    w = jnp.zeros((D, LANES), F32)
    w = w.at[:, E_LANE0:E_LANE0 + N_EXPERTS].set(expert_w).at[:, G_LANE0:G_LANE0 + MOE_GROUPS].set(group_w)
    b = jnp.zeros((1, LANES), F32)
    b = b.at[0, E_LANE0:E_LANE0 + N_EXPERTS].set(expert_b).at[0, G_LANE0:G_LANE0 + MOE_GROUPS].set(group_b)
    return pl.pallas_call(
        _router_kernel,
        grid=(nt,),
        in_specs=[pl.BlockSpec((tm, D), lambda t: (t, 0)), _const_spec(w.shape), _const_spec(b.shape)],
        out_specs=[pl.BlockSpec((tm, LANES), lambda t: (t, 0)),
                   pl.BlockSpec((1, SUBLANES, tm), lambda t: (t, 0, 0)),
                   _const_spec((SUBLANES, LANES))],
        out_shape=[jax.ShapeDtypeStruct((N, LANES), F32),
                   jax.ShapeDtypeStruct((nt, SUBLANES, tm), F32),
                   jax.ShapeDtypeStruct((SUBLANES, LANES), F32)],
        scratch_shapes=[pltpu.VMEM((SUBLANES, LANES), F32)],
        compiler_params=_cparams("arbitrary"),
        name="moe_router",
    )(hf, w, b)


def _row_copy(src, si, dst, di, sem):
    return pltpu.make_async_copy(src.at[pl.ds(si, 1)], dst.at[pl.ds(di, 1)], sem)


def _load_route(idx_hbm, idx_smem, sem, t):
    cp = pltpu.make_async_copy(idx_hbm.at[t], idx_smem, sem)
    cp.start()
    cp.wait()


def _dispatch_kernel(pstart_ref, idx_hbm, h_hbm, xs_in, xs_hbm, idx_smem, isem, sem):
    del xs_in
    t = pl.program_id(0)
    tm = MOE_BLOCK
    _load_route(idx_hbm, idx_smem, isem, t)

    def issue(r, c):
        n = t * tm + r
        for k in range(MOE_TOPK):
            slot = pstart_ref[idx_smem[k * tm + r]] + idx_smem[(MOE_TOPK + k) * tm + r]
            _row_copy(h_hbm, n, xs_hbm, slot, sem).start()
        return c

    lax.fori_loop(0, tm, issue, 0, unroll=8)

    def drain(r, c):
        for k in range(MOE_TOPK):
            _row_copy(h_hbm, 0, xs_hbm, 0, sem).wait()
        return c

    lax.fori_loop(0, tm, drain, 0, unroll=8)


def _moe_dispatch(hf, idx, pstart, n_slots):
    N, D = hf.shape
    nt = N // MOE_BLOCK
    xs0 = jnp.zeros((n_slots, D), F32)
    return pl.pallas_call(
        _dispatch_kernel,
        grid_spec=pltpu.PrefetchScalarGridSpec(
            num_scalar_prefetch=1,
            grid=(nt,),
            in_specs=[pl.BlockSpec(memory_space=pl.ANY)] * 3,
            out_specs=pl.BlockSpec(memory_space=pl.ANY),
            scratch_shapes=[pltpu.SMEM((2 * MOE_TOPK * MOE_BLOCK,), I32), pltpu.SemaphoreType.DMA,
                            pltpu.SemaphoreType.DMA],
        ),
        out_shape=jax.ShapeDtypeStruct((n_slots, D), F32),
        input_output_aliases={3: 0},
        compiler_params=_cparams("arbitrary"),
        name="moe_dispatch",
    )(pstart, idx, hf, xs0)


def _ffn_kernel(be_ref, nb_ref, x_ref, wg_ref, wu_ref, wd_ref, o_ref):
    del be_ref

    @pl.when(pl.program_id(0) < nb_ref[0])
    def _():
        x = x_ref[...].astype(BF16)
        hg = jnp.dot(x, wg_ref[...], preferred_element_type=F32)
        hu = jnp.dot(x, wu_ref[...], preferred_element_type=F32)
        hid = (hg * jax.nn.sigmoid(hg)) * hu
        o_ref[...] = jnp.dot(hid.astype(BF16), wd_ref[...], preferred_element_type=F32)

    @pl.when(pl.program_id(0) >= nb_ref[0])
    def _():
        o_ref[...] = jnp.zeros_like(o_ref)


def _moe_ffn(xs, block_expert, nb_used, w_gate, w_up, w_down):
    P, D = xs.shape
    nb = P // MOE_BLOCK
    wspec = lambda s: pl.BlockSpec((None,) + s, lambda b, be, nbu: (be[b], 0, 0))
    return pl.pallas_call(
        _ffn_kernel,
        grid_spec=pltpu.PrefetchScalarGridSpec(
            num_scalar_prefetch=2,
            grid=(nb,),
            in_specs=[pl.BlockSpec((MOE_BLOCK, D), lambda b, be, nbu: (b, 0)),
                      wspec((D, EXPERT_FF)), wspec((D, EXPERT_FF)), wspec((EXPERT_FF, D))],
            out_specs=pl.BlockSpec((MOE_BLOCK, D), lambda b, be, nbu: (b, 0)),
        ),
        out_shape=jax.ShapeDtypeStruct((P, D), F32),
        compiler_params=_cparams("arbitrary"),
        name="moe_expert_ffn",
    )(block_expert, nb_used, xs, w_gate.astype(BF16), w_up.astype(BF16), w_down.astype(BF16))


def _combine_kernel(pstart_ref, idx_hbm, ys_hbm, h_ref, rec_ref, g_ref, b_ref, o_ref, idx_smem, ybuf, isem, sem,
                    *, tp, t_real):
    t = pl.program_id(0)
    tm = MOE_BLOCK
    _load_route(idx_hbm, idx_smem, isem, t)

    def issue(r, c):
        for k in range(MOE_TOPK):
            slot = pstart_ref[idx_smem[k * tm + r]] + idx_smem[(MOE_TOPK + k) * tm + r]
            _row_copy(ys_hbm, slot, ybuf.at[k], r, sem).start()
        return c

    lax.fori_loop(0, tm, issue, 0, unroll=8)

    def drain(r, c):
        for k in range(MOE_TOPK):
            _row_copy(ys_hbm, 0, ybuf.at[k], 0, sem).wait()
        return c

    lax.fori_loop(0, tm, drain, 0, unroll=8)

    rec = rec_ref[...]
    y = DN_ALPHA * h_ref[...] + rec[:, R_G1:R_G1 + 1] * ybuf[0] + rec[:, R_G2:R_G2 + 1] * ybuf[1]
    out = _layer_norm_rows(y, g_ref[...], b_ref[...])
    n = (t * tm + lax.broadcasted_iota(I32, out.shape, 0)).astype(F32)
    p = n - jnp.floor((n + 0.5) / tp) * tp
    o_ref[...] = jnp.where((p >= FRONT) & (p < FRONT + t_real), out, 0.0)


def _moe_combine(hf, ys, idx, rec, pstart, ln_g, ln_b, tp, t_real):
    N, D = hf.shape
    nt = N // MOE_BLOCK
    g = ln_g.reshape(1, D).astype(F32)
    b = ln_b.reshape(1, D).astype(F32)
    anyspec = pl.BlockSpec(memory_space=pl.ANY)
    return pl.pallas_call(
        functools.partial(_combine_kernel, tp=tp, t_real=t_real),
        grid_spec=pltpu.PrefetchScalarGridSpec(
            num_scalar_prefetch=1,
            grid=(nt,),
            in_specs=[anyspec, anyspec,
                      pl.BlockSpec((MOE_BLOCK, D), lambda t, ps: (t, 0)),
                      pl.BlockSpec((MOE_BLOCK, LANES), lambda t, ps: (t, 0)),
                      pl.BlockSpec((1, D), lambda t, ps: (0, 0)), pl.BlockSpec((1, D), lambda t, ps: (0, 0))],
            out_specs=pl.BlockSpec((MOE_BLOCK, D), lambda t, ps: (t, 0)),
            scratch_shapes=[pltpu.SMEM((2 * MOE_TOPK * MOE_BLOCK,), I32),
                            pltpu.VMEM((MOE_TOPK, MOE_BLOCK, D), F32),
                            pltpu.SemaphoreType.DMA, pltpu.SemaphoreType.DMA],
        ),
        out_shape=jax.ShapeDtypeStruct((N, D), F32),
        compiler_params=_cparams("arbitrary"),
        name="moe_combine_ln",
    )(pstart, idx, ys, hf, rec, g, b)


def _moe_layer(h, group_w, group_b, expert_w, expert_b, w_gate, w_up, w_down, ln_g, ln_b, t_real):
    B, TP, D = h.shape
    hf = h.reshape(B * TP, D)
    N = B * TP
    rec, rect, cnt = _moe_route(hf, group_w, group_b, expert_w, expert_b)
    counts = cnt[0, :N_EXPERTS].astype(I32)
    padded = (counts + MOE_BLOCK - 1) // MOE_BLOCK * MOE_BLOCK
    pend = jnp.cumsum(padded)
    pstart = (pend - padded).astype(I32)
    n_blocks = (N * MOE_TOPK) // MOE_BLOCK + N_EXPERTS
    block_start = jnp.arange(n_blocks, dtype=I32) * MOE_BLOCK
    block_expert = jnp.minimum(jnp.searchsorted(pend, block_start, side='right'), N_EXPERTS - 1).astype(I32)
    nb_used = (pend[-1:] // MOE_BLOCK).astype(I32)
    idx = rect[:, :2 * MOE_TOPK, :].astype(I32).reshape(N // MOE_BLOCK, 2 * MOE_TOPK * MOE_BLOCK)
    xs = _moe_dispatch(hf, idx, pstart, n_blocks * MOE_BLOCK)
    ys = _moe_ffn(xs, block_expert, nb_used, w_gate, w_up, w_down)
    out = _moe_combine(hf, ys, idx, rec, pstart, ln_g, ln_b, TP, t_real)
    return out.reshape(B, TP, D)


def _even_layer(h, w_in, kv_norm, w_uk, w_uv, conv_w, conv_b, rg_wa, rg_ba, rg_wx, rg_bx, rg_lambda, w_out,
                ln_g, ln_b, t_real):
    q, qi, ki, wi, gate, xb, kv = _even_in_proj(h, w_in, kv_norm, w_uk, w_uv)
    attn = _dsa(q, qi, wi, ki, kv, t_real)
    rec = _griffin(xb, gate, conv_w, conv_b, rg_wa, rg_ba, rg_wx, rg_bx, rg_lambda, t_real)
    return _out_ln(h, [attn, rec], [w_out[:A_OUT], w_out[A_OUT:]], ln_g, ln_b, t_real)


def _odd_in_kernel(h_ref, wq_ref, wk_ref, wv_ref, wz_ref, wa_ref, wb_ref, q_ref, k_ref, v_ref, z_ref, a_ref, b_ref):
    a = h_ref[0].astype(BF16)
    for w_ref, o_ref in ((wq_ref, q_ref), (wk_ref, k_ref), (wv_ref, v_ref), (wz_ref, z_ref), (wa_ref, a_ref),
                         (wb_ref, b_ref)):
        o_ref[0] = jnp.dot(a, w_ref[...], preferred_element_type=F32)


def _odd_in_proj(h, w_in):
    B, TP, D = h.shape
    tm = _row_tile(TP) // 2 if _row_tile(TP) % 16 == 0 else _row_tile(TP)
    wb = w_in.astype(BF16)
    o = [0, GDN_QK_WIDTH, 2 * GDN_QK_WIDTH, 2 * GDN_QK_WIDTH + GDN_V_WIDTH, 2 * GDN_QK_WIDTH + 2 * GDN_V_WIDTH]
    o += [o[-1] + GDN_V_HEADS, o[-1] + 2 * GDN_V_HEADS]
    ws = [wb[:, o[i]:o[i + 1]] for i in range(4)]
    ws += [jnp.pad(wb[:, o[i]:o[i + 1]], ((0, 0), (0, LANES - GDN_V_HEADS))) for i in (4, 5)]
    widths = [GDN_QK_WIDTH, GDN_QK_WIDTH, GDN_V_WIDTH, GDN_V_WIDTH, LANES, LANES]
    return pl.pallas_call(
        _odd_in_kernel,
        grid=(B, TP // tm),
        in_specs=[pl.BlockSpec((1, tm, D), lambda b, j: (b, j, 0))] + [_const_spec(w.shape) for w in ws],
        out_specs=[pl.BlockSpec((1, tm, n), lambda b, j: (b, j, 0)) for n in widths],
        out_shape=[jax.ShapeDtypeStruct((B, TP, n), F32) for n in widths],
        compiler_params=_cparams("parallel", "parallel"),
        name="odd_in_proj",
    )(h, *ws)


def _conv_silu(x, tail_ref, cw):
    tt = x.shape[0]
    win = jnp.concatenate([tail_ref[...], x], axis=0)
    tail_ref[...] = x[tt - SUBLANES:, :]
    y = jnp.zeros_like(x)
    for t in range(CONV_WIDTH):
        off = SUBLANES - (CONV_WIDTH - 1) + t
        y = y + cw[t:t + 1, :] * win[off:off + tt, :]
    return y * jax.nn.sigmoid(y)


def _gdn_prep_kernel(q_ref, k_ref, v_ref, a_ref, b_ref, cwq_ref, cwk_ref, cwv_ref, alog_ref, dtb_ref,
                     u_ref, w_ref, qg_ref, kg_ref, at_ref, e_ref,
                     tq_ref, tk_ref, tv_ref, qs_ref, ks_ref, vs_ref, *, t_real):
    j = pl.program_id(1)
    tt = q_ref.shape[1]
    hd = GDN_HEAD_DIM
    half = CHUNK

    @pl.when(j == 0)
    def _():
        tq_ref[...] = jnp.zeros_like(tq_ref)
        tk_ref[...] = jnp.zeros_like(tk_ref)
        tv_ref[...] = jnp.zeros_like(tv_ref)

    rows1 = j * tt + lax.broadcasted_iota(I32, (tt, 1), 0)
    valid = _valid_rows(rows1, t_real)
    q = jnp.where(valid, _conv_silu(q_ref[0], tq_ref, cwq_ref[...]), 0.0)
    k = jnp.where(valid, _conv_silu(k_ref[0], tk_ref, cwk_ref[...]), 0.0)
    vs_ref[...] = jnp.where(valid, _conv_silu(v_ref[0], tv_ref, cwv_ref[...]), 0.0)
    for h in range(GDN_K_HEADS):
        sl = slice(h * hd, (h + 1) * hd)
        qh, kh = q[:, sl], k[:, sl]
        qs_ref[:, sl] = qh * lax.rsqrt(jnp.sum(qh * qh, axis=1, keepdims=True) + 1e-6) * (hd ** -0.5)
        ks_ref[:, sl] = kh * lax.rsqrt(jnp.sum(kh * kh, axis=1, keepdims=True) + 1e-6)

    lane = lax.broadcasted_iota(I32, (tt, LANES), 1)
    rowi = lax.broadcasted_iota(I32, (tt, LANES), 0)
    live = valid & (lane < GDN_V_HEADS)
    g = jnp.where(live, -jnp.exp(alog_ref[...]) * _softplus(a_ref[0] + dtb_ref[...]), 0.0)
    beta = jnp.where(live, jax.nn.sigmoid(b_ref[0]), 0.0)
    gc = g
    s = 1
    while s < CHUNK:
        gc = gc + jnp.where((rowi % CHUNK) >= s, pltpu.roll(gc, s, 0), 0.0)
        s *= 2
    gc_next = pltpu.roll(gc, LANES - 1, 1)
    n_ch = tt // CHUNK
    e_rows = []
    for c in range(n_ch):
        e_rows.append(jnp.broadcast_to(jnp.exp(gc[(c + 1) * CHUNK - 1:(c + 1) * CHUNK, :]), (CHUNK, LANES)))
    e_ref[0] = jnp.concatenate(e_rows, axis=0)

    nt = (((1,), (1,)), ((), ()))
    l2 = lax.broadcasted_iota(I32, (CHUNK, 2 * half), 1)
    ii = lax.broadcasted_iota(I32, (CHUNK, 2 * half), 0)
    jj = l2 % half
    left = l2 < half
    r2 = lax.broadcasted_iota(I32, (2 * half, 2 * half), 0)
    c2 = lax.broadcasted_iota(I32, (2 * half, 2 * half), 1)
    eye2 = jnp.where(r2 == c2, 1.0, 0.0)
    bdot = lambda x, y: jnp.dot(x.astype(BF16), y.astype(BF16), preferred_element_type=F32)
    for c in range(n_ch):
        rs = slice(c * CHUNK, (c + 1) * CHUNK)
        gc_c, beta_c = gc[rs], beta[rs]
        gt = jnp.concatenate([gc_c, gc_next[rs]], axis=0).T
        glast = gc_c[CHUNK - 1:CHUNK, :]
        for pr in range(GDN_K_HEADS):
            ha, hb = 2 * pr, 2 * pr + 1
            ksl = slice(pr * hd, (pr + 1) * hd)
            kc = ks_ref[rs, ksl]
            qc = qs_ref[rs, ksl]
            kc16 = kc.astype(BF16)
            k2 = jnp.concatenate([kc16, kc16], axis=0)
            kk2 = lax.dot_general(kc16, k2, nt, preferred_element_type=F32)
            qk2 = lax.dot_general(qc.astype(BF16), k2, nt, preferred_element_type=F32)
            gca, gcb = gc_c[:, ha:ha + 1], gc_c[:, hb:hb + 1]
            ba, bb = beta_c[:, ha:ha + 1], beta_c[:, hb:hb + 1]
            gcol2 = jnp.where(left, gca, gcb)
            bcol2 = jnp.where(left, ba, bb)
            decay2 = jnp.exp(jnp.where(ii >= jj, gcol2 - gt[ha:ha + 1, :], -jnp.inf))
            m2 = jnp.where(ii > jj, bcol2 * kk2 * decay2, 0.0)
            at_ref[0, rs, pr * 2 * half:(pr + 1) * 2 * half] = (qk2 * decay2).astype(BF16)
            m = jnp.concatenate([jnp.where(left, m2, 0.0), jnp.where(left, 0.0, m2)], axis=0)
            tinv = eye2 - m
            pw = bdot(m, m)
            n_fac = CHUNK.bit_length() - 2
            for f in range(n_fac):
                tinv = tinv + bdot(tinv, pw)
                if f + 1 < n_fac:
                    pw = bdot(pw, pw)
            ega, egb = jnp.exp(gca), jnp.exp(gcb)
            va = vs_ref[rs, ha * hd:(ha + 1) * hd]
            vb = vs_ref[rs, hb * hd:(hb + 1) * hd]
            rhs = jnp.concatenate([jnp.concatenate([va * ba, kc * (ba * ega)], axis=1),
                                   jnp.concatenate([vb * bb, kc * (bb * egb)], axis=1)], axis=0)
            sol = bdot(tinv, rhs)
            osl = slice(ha * hd, (hb + 1) * hd)
            u_ref[0, rs, osl] = jnp.concatenate([sol[:CHUNK, :hd], sol[CHUNK:, :hd]], axis=1)
            w_ref[0, rs, osl] = jnp.concatenate([sol[:CHUNK, hd:], sol[CHUNK:, hd:]], axis=1).astype(BF16)
            qg_ref[0, rs, osl] = jnp.concatenate([qc * ega, qc * egb], axis=1).astype(BF16)
            kg_ref[0, rs, osl] = jnp.concatenate([kc * jnp.exp(glast[:, ha:ha + 1] - gca),
                                                  kc * jnp.exp(glast[:, hb:hb + 1] - gcb)], axis=1).astype(BF16)


def _gdn_prep(q, k, v, a, b, conv_w, a_log, dt_bias, t_real):
    B, TP, _ = q.shape
    tt = TIME_TILE
    cw = jnp.pad(conv_w.astype(F32), ((0, SUBLANES - CONV_WIDTH), (0, 0)))
    cwq, cwk, cwv = cw[:, :GDN_QK_WIDTH], cw[:, GDN_QK_WIDTH:2 * GDN_QK_WIDTH], cw[:, 2 * GDN_QK_WIDTH:]
    alog = jnp.pad(a_log.astype(F32), (0, LANES - GDN_V_HEADS)).reshape(1, LANES)
    dtb = jnp.pad(dt_bias.astype(F32), (0, LANES - GDN_V_HEADS)).reshape(1, LANES)
    tspec = lambda n: pl.BlockSpec((1, tt, n), lambda bb, j: (bb, j, 0))
    ws = (cwq, cwk, cwv, alog, dtb)
    VW, QW = GDN_V_WIDTH, GDN_QK_WIDTH
    outs = [(VW, F32), (VW, BF16), (VW, BF16), (VW, BF16), (GDN_K_HEADS * 2 * CHUNK, BF16), (LANES, F32)]
    return pl.pallas_call(
        functools.partial(_gdn_prep_kernel, t_real=t_real),
        grid=(B, TP // tt),
        in_specs=[tspec(QW), tspec(QW), tspec(VW), tspec(LANES), tspec(LANES)] + [_const_spec(w.shape) for w in ws],
        out_specs=[tspec(n) for n, _ in outs],
        out_shape=[jax.ShapeDtypeStruct((B, TP, n), dt) for n, dt in outs],
        scratch_shapes=[pltpu.VMEM((SUBLANES, QW), F32), pltpu.VMEM((SUBLANES, QW), F32),
                        pltpu.VMEM((SUBLANES, VW), F32), pltpu.VMEM((tt, QW), F32), pltpu.VMEM((tt, QW), F32),
                        pltpu.VMEM((tt, VW), F32)],
        compiler_params=_cparams("parallel", "arbitrary"),
        name="gdn_prep",
    )(q, k, v, a, b, *ws)


def _gdn_scan_kernel(u_ref, w_ref, qg_ref, kg_ref, at_ref, e_ref, z_ref, on_ref, o_ref, s_ref):
    j = pl.program_id(1)
    tt = u_ref.shape[1]
    hd = GDN_HEAD_DIM

    @pl.when(j == 0)
    def _():
        s_ref[...] = jnp.zeros_like(s_ref)

    lane = lax.broadcasted_iota(I32, (CHUNK, 2 * CHUNK), 1)
    zeros_half = jnp.zeros((CHUNK, hd), BF16)
    tn = (((0,), (0,)), ((), ()))
    for c in range(tt // CHUNK):
        rs = slice(c * CHUNK, (c + 1) * CHUNK)
        e_row = e_ref[0, c * CHUNK:c * CHUNK + 1, :]
        for h in range(GDN_V_HEADS):
            hs = slice(h * hd, (h + 1) * hd)
            pr, side = h // 2, h % 2
            s = s_ref[h]
            s16 = s.astype(BF16)
            vnew = u_ref[0, rs, hs] - jnp.dot(w_ref[0, rs, hs], s16, preferred_element_type=F32)
            v16 = vnew.astype(BF16)
            at2 = at_ref[0, rs, pr * 2 * CHUNK:(pr + 1) * 2 * CHUNK]
            if side == 0:
                at_h = jnp.where(lane < CHUNK, at2, jnp.zeros_like(at2))
                vpad = jnp.concatenate([v16, zeros_half], axis=0)
            else:
                at_h = jnp.where(lane >= CHUNK, at2, jnp.zeros_like(at2))
                vpad = jnp.concatenate([zeros_half, v16], axis=0)
            o = (jnp.dot(qg_ref[0, rs, hs], s16, preferred_element_type=F32)
                 + jnp.dot(at_h, vpad, preferred_element_type=F32))
            s_ref[h] = s * e_row[:, h:h + 1] + lax.dot_general(kg_ref[0, rs, hs], v16, tn,
                                                               preferred_element_type=F32)
            on = o * lax.rsqrt(jnp.mean(o * o, axis=1, keepdims=True) + 1e-6) * on_ref[...]
            z = z_ref[0, rs, hs]
            o_ref[0, rs, hs] = (on * (z * jax.nn.sigmoid(z))).astype(BF16)


def _gdn_scan(u, w, qg, kg, at, e, z, o_norm):
    B, TP, VW = u.shape
    tt = TIME_TILE
    tspec = lambda n: pl.BlockSpec((1, tt, n), lambda bb, j: (bb, j, 0))
    on = o_norm.reshape(1, GDN_HEAD_DIM).astype(F32)
    return pl.pallas_call(
        _gdn_scan_kernel,
        grid=(B, TP // tt),
        in_specs=[tspec(VW), tspec(VW), tspec(VW), tspec(VW), tspec(at.shape[-1]), tspec(LANES), tspec(VW),
                  _const_spec(on.shape)],
        out_specs=tspec(VW),
        out_shape=jax.ShapeDtypeStruct((B, TP, VW), BF16),
        scratch_shapes=[pltpu.VMEM((GDN_V_HEADS, GDN_HEAD_DIM, GDN_HEAD_DIM), F32)],
        compiler_params=_cparams("parallel", "arbitrary"),
        name="gdn_scan",
    )(u, w, qg, kg, at, e, z, on)


def _odd_layer(h, w_in, conv_w, a_log, dt_bias, o_norm, w_out, ln_g, ln_b, t_real):
    q, k, v, z, a, b = _odd_in_proj(h, w_in)
    u, w, qg, kg, at, e = _gdn_prep(q, k, v, a, b, conv_w, a_log, dt_bias, t_real)
    gated = _gdn_scan(u, w, qg, kg, at, e, z, o_norm)
    return _out_ln(h, [gated], [w_out], ln_g, ln_b, t_real)


def kernel(x, meta_tokens, even_w_in, even_kv_norm, even_w_uk, even_w_uv, even_conv_w, even_conv_b, even_rg_wa,
           even_rg_ba, even_rg_wx, even_rg_bx, even_rg_lambda, even_w_out, odd_w_in, odd_conv_w, odd_a_log,
           odd_dt_bias, odd_o_norm, odd_w_out, ln_g, ln_b, moe_group_w, moe_group_b, moe_expert_w, moe_expert_b,
           moe_w_gate, moe_w_up, moe_w_down):
    B, S, D = x.shape
    t_real = N_META + S
    tp = -(-(FRONT + t_real) // TIME_TILE) * TIME_TILE
    meta = jnp.broadcast_to(meta_tokens.astype(x.dtype)[None], (B, N_META, D))
    h = jnp.concatenate([jnp.zeros((B, FRONT, D), x.dtype), meta, x,
                         jnp.zeros((B, tp - FRONT - t_real, D), x.dtype)], axis=1)
    for layer in range(DEPTH):
        i = layer // 2
        if layer % 2 == 0:
            h = _even_layer(h, even_w_in[i], even_kv_norm[i], even_w_uk[i], even_w_uv[i], even_conv_w[i],
                            even_conv_b[i], even_rg_wa[i], even_rg_ba[i], even_rg_wx[i], even_rg_bx[i],
                            even_rg_lambda[i], even_w_out[i], ln_g[layer, 0], ln_b[layer, 0], t_real)
        else:
            h = _odd_layer(h, odd_w_in[i], odd_conv_w[i], odd_a_log[i], odd_dt_bias[i], odd_o_norm[i],
                           odd_w_out[i], ln_g[layer, 0], ln_b[layer, 0], t_real)
        h = _moe_layer(h, moe_group_w[layer], moe_group_b[layer], moe_expert_w[layer], moe_expert_b[layer],
                       moe_w_gate[layer], moe_w_up[layer], moe_w_down[layer], ln_g[layer, 1], ln_b[layer, 1], t_real)
    return h[:, FRONT:FRONT + t_real][:, N_META:]
```

```python
import functools
import math

import jax
import jax.numpy as jnp
from jax import lax
from jax.experimental import pallas as pl
from jax.experimental.pallas import tpu as pltpu

F32 = jnp.float32
BF16 = jnp.bfloat16
I32 = jnp.int32

D_MODEL = 1024
DEPTH = 2
N_META = 16
DN_ALPHA = (2 * DEPTH) ** 0.25

A_HEADS = 4
A_HEAD_DIM = 128
A_OUT = A_HEADS * A_HEAD_DIM
A_KV_RANK = 256
IDX_HEADS = 8
IDX_DIM = 64
IDX_TOPK_CAP = 256

RG_WIDTH = 512
RG_BLOCKS = 8
RG_C = 8.0
CONV_WIDTH = 4

GDN_K_HEADS = 8
GDN_V_HEADS = 16
GDN_HEAD_DIM = 128
GDN_QK_WIDTH = GDN_K_HEADS * GDN_HEAD_DIM
GDN_V_WIDTH = GDN_V_HEADS * GDN_HEAD_DIM
CHUNK = 64

MOE_GROUPS = 4
MOE_PER_GROUP = 8
N_EXPERTS = MOE_GROUPS * MOE_PER_GROUP
MOE_TOPK = 2
EXPERT_FF = 512
MOE_BLOCK = 256

LANES = 128
SUBLANES = 8
TIME_TILE = 128
FRONT = (-N_META) % CHUNK
NEG_BIG = -1e30
INT_MIN = -2 ** 31
VMEM_LIMIT = 56 * 1024 * 1024


def _cparams(*sem, flags=None):
    return pltpu.CompilerParams(dimension_semantics=sem, vmem_limit_bytes=VMEM_LIMIT, flags=flags)


def _const_spec(shape):
    nd = len(shape)
    return pl.BlockSpec(shape, lambda *_: (0,) * nd)


def _row_tile(tp):
    best = SUBLANES
    for t in range(SUBLANES, 641, SUBLANES):
        if tp % t == 0:
            best = t
    return best


def _valid_rows(p, t_real):
    return (p >= FRONT) & (p < FRONT + t_real)


def _layer_norm_rows(y, g, b):
    mu = jnp.mean(y, axis=-1, keepdims=True)
    yc = y - mu
    var = jnp.mean(yc * yc, axis=-1, keepdims=True)
    return yc * lax.rsqrt(var + 1e-5) * g + b


def _softplus(z):
    return jnp.maximum(z, 0.0) + jnp.log1p(jnp.exp(-jnp.abs(z)))


def _even_in_kernel(h_ref, wq_ref, wc_ref, wqi_ref, wki_ref, wwi_ref, wg_ref, wx_ref, kvn_ref, wukv_ref,
                    q_ref, qi_ref, ki_ref, wi_ref, gate_ref, xb_ref, kv_ref):
    a = h_ref[0].astype(BF16)
    dot = functools.partial(jnp.dot, preferred_element_type=F32)
    q_ref[0] = dot(a, wq_ref[...]).astype(BF16)
    qi_ref[0] = dot(a, wqi_ref[...]).astype(BF16)
    ki_ref[0] = dot(a, wki_ref[...]).astype(BF16)
    wi_ref[0] = dot(a, wwi_ref[...]) * (IDX_HEADS ** -0.5) * (IDX_DIM ** -0.5)
    gate_ref[0] = dot(a, wg_ref[...])
    xb_ref[0] = dot(a, wx_ref[...])
    c = dot(a, wc_ref[...])
    latent = c * lax.rsqrt(jnp.mean(c * c, axis=-1, keepdims=True) + 1e-6) * kvn_ref[...]
    kv_ref[0] = dot(latent.astype(BF16), wukv_ref[...]).astype(BF16)


def _even_in_proj(h, w_in, kv_norm, w_uk, w_uv):
    B, TP, D = h.shape
    tm = _row_tile(TP)
    o = [0]
    for p in (A_OUT, A_KV_RANK, IDX_HEADS * IDX_DIM, IDX_DIM, IDX_HEADS, RG_WIDTH, RG_WIDTH):
        o.append(o[-1] + p)
    wb = w_in.astype(BF16)
    wq, wc = wb[:, o[0]:o[1]], wb[:, o[1]:o[2]]
    wqi = wb[:, o[2]:o[3]].reshape(D, IDX_HEADS, IDX_DIM)
    wqi = jnp.pad(wqi, ((0, 0), (0, 0), (0, LANES - IDX_DIM))).reshape(D, IDX_HEADS * LANES)
    wki = jnp.pad(wb[:, o[3]:o[4]], ((0, 0), (0, LANES - IDX_DIM)))
    wwi = jnp.pad(wb[:, o[4]:o[5]], ((0, 0), (0, LANES - IDX_HEADS)))
    wg, wx = wb[:, o[5]:o[6]], wb[:, o[6]:o[7]]
    wukv = jnp.concatenate([w_uk, w_uv], axis=1).astype(BF16)
    kvn = kv_norm.reshape(1, A_KV_RANK).astype(F32)
    ws = (wq, wc, wqi, wki, wwi, wg, wx, kvn, wukv)
    outs = [(A_OUT, BF16), (IDX_HEADS * LANES, BF16), (LANES, BF16), (LANES, F32), (RG_WIDTH, F32),
            (RG_WIDTH, F32), (2 * A_HEAD_DIM, BF16)]
    return pl.pallas_call(
        _even_in_kernel,
        grid=(B, TP // tm),
        in_specs=[pl.BlockSpec((1, tm, D), lambda b, j: (b, j, 0))] + [_const_spec(w.shape) for w in ws],
        out_specs=[pl.BlockSpec((1, tm, n), lambda b, j: (b, j, 0)) for n, _ in outs],
        out_shape=[jax.ShapeDtypeStruct((B, TP, n), dt) for n, dt in outs],
        compiler_params=_cparams("parallel", "parallel"),
        name="even_in_proj",
    )(h, *ws)


def _dsa_body(nk, q_ref, qi_ref, wi_ref, ki_ref, kv_ref, o_ref, key_ref, bias_ref, vt_ref, n_sel, t_real):
    i = pl.program_id(1)
    tq = q_ref.shape[1]
    qi = qi_ref[0]
    ki = ki_ref[0, :nk, :]
    wit = wi_ref[0].T
    nt = (((1,), (1,)), ((), ()))
    score = jnp.zeros((nk, tq), F32)
    for h in range(IDX_HEADS):
        s = lax.dot_general(ki, qi[:, h * LANES:(h + 1) * LANES], nt, preferred_element_type=F32)
        score = score + wit[h:h + 1, :] * jnp.maximum(s, 0.0)

    kpos = lax.broadcasted_iota(I32, (nk, tq), 0)
    qpos = i * tq + lax.broadcasted_iota(I32, (nk, tq), 1)
    vis = (kpos <= qpos) & (kpos >= FRONT) & (kpos < FRONT + t_real)
    bits = lax.bitcast_convert_type(score, I32)
    key = jnp.where(bits < 0, bits ^ jnp.int32(0x7FFFFFFF), bits)
    key_ref[:nk, :] = jnp.where(vis, key, jnp.int32(INT_MIN))

    def over_keys(x, op):
        pair = jnp.add if op is jnp.sum else jnp.maximum
        group = LANES // SUBLANES
        x = x.reshape(nk // LANES, group, SUBLANES, x.shape[1])
        while x.shape[1] > 1:
            half = x.shape[1] // 2
            x = pair(x[:, :half], x[:, half:])
        return op(op(x[:, 0], axis=0), axis=0, keepdims=True)

    def count(mask):
        return over_keys(jnp.where(mask, 1, 0).astype(I32), jnp.sum)

    def any_query(mask):
        return jnp.max(jnp.where(mask, 1, 0))

    n_vis = count(vis)

    bits_per_check = 8

    def bit_cond(c):
        return (c[0] < 32) & (c[3] > 0)

    def bit_group(c):
        it0, cand, cnt_c, _ = c

        def bit_step(b, cc):
            cand, cnt_c = cc
            trial = cand + jnp.left_shift(jnp.int32(1), 31 - (it0 + b))
            cnt = count(key_ref[:nk, :] >= trial)
            take = cnt >= n_sel
            return jnp.where(take, trial, cand), jnp.where(take, cnt, cnt_c)

        cand, cnt_c = lax.fori_loop(0, bits_per_check, bit_step, (cand, cnt_c))
        return it0 + bits_per_check, cand, cnt_c, any_query((cnt_c != n_sel) & (n_vis > n_sel))

    start = (jnp.int32(0), jnp.full((1, tq), INT_MIN, I32), jnp.full((1, tq), nk, I32),
             any_query(n_vis > n_sel))
    _, thr, cnt_ge, _ = lax.while_loop(bit_cond, bit_group, start)
    keys = key_ref[:nk, :]
    bias_ref[:nk, :] = jnp.where((keys >= thr) & vis, 0.0, NEG_BIG)
    tie_queries = (cnt_ge > n_sel) & (thr > jnp.int32(INT_MIN))

    @pl.when(any_query(tie_queries) > 0)
    def _():
        keys = key_ref[:nk, :]
        eq = keys == thr
        need = n_sel - count(keys > thr)
        nbits = max(1, (nk - 1).bit_length())

        def idx_step(it, cand):
            trial = cand + jnp.left_shift(jnp.int32(1), nbits - 1 - it)
            return jnp.where(count(eq & (kpos < trial)) < need, trial, cand)

        last = lax.fori_loop(0, nbits, idx_step, jnp.zeros((1, tq), I32))
        sel = (keys > thr) | (eq & (kpos <= last))
        bias_ref[:nk, :] = jnp.where(sel & vis, 0.0, NEG_BIG)

    bias = bias_ref[:nk, :]
    q = q_ref[0]
    k = kv_ref[0, :nk, :A_HEAD_DIM]
    qs = jnp.concatenate([q[:, h * A_HEAD_DIM:(h + 1) * A_HEAD_DIM] for h in range(A_HEADS)], axis=0)
    lg = lax.dot_general(k, qs, nt, preferred_element_type=F32) * (A_HEAD_DIM ** -0.5)
    lg = lg + jnp.concatenate([bias] * A_HEADS, axis=1)
    m = over_keys(lg, jnp.max)
    p = jnp.exp(lg - m)
    l = over_keys(p, jnp.sum)
    ot = jnp.dot(vt_ref[:, :nk], p.astype(BF16), preferred_element_type=F32) / l
    for h in range(A_HEADS):
        o_ref[0, :, h * A_HEAD_DIM:(h + 1) * A_HEAD_DIM] = ot[:, h * tq:(h + 1) * tq].T.astype(BF16)


DSA_WIDTH_STEP = 3


def _dsa_kernel(q_ref, qi_ref, wi_ref, ki_ref, kv_ref, o_ref, key_ref, bias_ref, vt_ref, *, n_sel, t_real):
    i = pl.program_id(1)
    tq = q_ref.shape[1]
    tp = ki_ref.shape[1]
    n_tiles = tp // tq

    @pl.when(i == 0)
    def _():
        vt_ref[...] = kv_ref[0, :, A_HEAD_DIM:].astype(F32).T.astype(BF16)

    lo = 0
    while lo < n_tiles:
        hi = min(lo + DSA_WIDTH_STEP, n_tiles)

        @pl.when((i >= lo) & (i < hi))
        def _(hi=hi):
            _dsa_body(hi * tq, q_ref, qi_ref, wi_ref, ki_ref, kv_ref, o_ref, key_ref, bias_ref, vt_ref,
                      n_sel, t_real)

        lo = hi


def _dsa(q, qi, wi, ki, kv, t_real):
    B, TP, _ = q.shape
    tq = TIME_TILE
    n_sel = min(IDX_TOPK_CAP, t_real // 4)
    kern = functools.partial(_dsa_kernel, n_sel=n_sel, t_real=t_real)
    qspec = lambda n: pl.BlockSpec((1, tq, n), lambda b, i: (b, i, 0))
    kspec = lambda n: pl.BlockSpec((1, TP, n), lambda b, i: (b, 0, 0))
    return pl.pallas_call(
        kern,
        grid=(B, TP // tq),
        in_specs=[qspec(A_OUT), qspec(IDX_HEADS * LANES), qspec(LANES), kspec(LANES), kspec(2 * A_HEAD_DIM)],
        out_specs=qspec(A_OUT),
        out_shape=jax.ShapeDtypeStruct((B, TP, A_OUT), BF16),
        scratch_shapes=[pltpu.VMEM((TP, tq), I32), pltpu.VMEM((TP, tq), F32), pltpu.VMEM((A_HEAD_DIM, TP), BF16)],
        compiler_params=_cparams("parallel", "arbitrary"),
        name="dsa_attention",
    )(q, qi, wi, ki, kv)


def _shift_rows(x, s, fill):
    rows = lax.broadcasted_iota(I32, x.shape, 0)
    return jnp.where(rows >= s, pltpu.roll(x, s, 0), fill)


def _griffin_kernel(xb_ref, gate_ref, cw_ref, cb_ref, wg_ref, bg_ref, lam_ref, o_ref, tail_ref, car_ref, *, t_real):
    j = pl.program_id(1)
    tt = xb_ref.shape[1]

    @pl.when(j == 0)
    def _():
        tail_ref[...] = jnp.zeros_like(tail_ref)
        car_ref[...] = jnp.zeros_like(car_ref)

    x = xb_ref[0]
    win = jnp.concatenate([tail_ref[...], x], axis=0)
    tail_ref[...] = x[tt - SUBLANES:, :]
    xr = jnp.zeros_like(x) + cb_ref[...]
    for t in range(CONV_WIDTH):
        off = SUBLANES - (CONV_WIDTH - 1) + t
        xr = xr + cw_ref[t:t + 1, :] * win[off:off + tt, :]
    g = jnp.dot(xr.astype(BF16), wg_ref[...], preferred_element_type=F32) + bg_ref[...]
    r = jax.nn.sigmoid(g[:, :RG_WIDTH])
    ig = jax.nn.sigmoid(g[:, RG_WIDTH:])
    log_a = -RG_C * r * _softplus(-lam_ref[...])
    a = jnp.exp(log_a)
    th = jnp.tanh(log_a)
    u = jnp.sqrt(-2.0 * th / (1.0 - th)) * (ig * xr)
    p = j * tt + lax.broadcasted_iota(I32, x.shape, 0)
    u = jnp.where(_valid_rows(p, t_real), u, 0.0)
    s = 1
    while s < tt:
        u = u + a * _shift_rows(u, s, 0.0)
        a = a * _shift_rows(a, s, 1.0)
        s *= 2
    hcur = u + a * car_ref[0:1, :]
    car_ref[...] = jnp.broadcast_to(hcur[tt - 1:tt, :], car_ref.shape)
    o_ref[0] = hcur * jax.nn.gelu(gate_ref[0])


def _block_diag(w):
    nb, bs, _ = w.shape
    eye = jnp.eye(nb, dtype=w.dtype)
    return (w[:, :, None, :] * eye[:, None, :, None]).reshape(nb * bs, nb * bs)


def _griffin(xb, gate, conv_w, conv_b, rg_wa, rg_ba, rg_wx, rg_bx, rg_lambda, t_real):
    B, TP, R = xb.shape
    tt = TIME_TILE
    cw = jnp.pad(conv_w.astype(F32), ((0, SUBLANES - CONV_WIDTH), (0, 0)))
    cb = conv_b.reshape(1, R).astype(F32)
    wg = jnp.concatenate([_block_diag(rg_wa), _block_diag(rg_wx)], axis=1).astype(BF16)
    bg = jnp.concatenate([rg_ba, rg_bx]).reshape(1, 2 * R).astype(F32)
    lam = rg_lambda.reshape(1, R).astype(F32)
    tspec = pl.BlockSpec((1, tt, R), lambda b, j: (b, j, 0))
    ws = (cw, cb, wg, bg, lam)
    return pl.pallas_call(
        functools.partial(_griffin_kernel, t_real=t_real),
        grid=(B, TP // tt),
        in_specs=[tspec, tspec] + [_const_spec(w.shape) for w in ws],
        out_specs=tspec,
        out_shape=jax.ShapeDtypeStruct((B, TP, R), F32),
        scratch_shapes=[pltpu.VMEM((SUBLANES, R), F32), pltpu.VMEM((SUBLANES, R), F32)],
        compiler_params=_cparams("parallel", "arbitrary"),
        name="griffin_rglru",
    )(xb, gate, *ws)


def _out_ln_kernel(*refs, n_in, t_real):
    h_ref = refs[0]
    a_refs = refs[1:1 + n_in]
    w_refs = refs[1 + n_in:1 + 2 * n_in]
    g_ref, b_ref, o_ref = refs[1 + 2 * n_in:]
    j = pl.program_id(1)
    tm = h_ref.shape[1]
    y = DN_ALPHA * h_ref[0]
    for a_ref, w_ref in zip(a_refs, w_refs):
        y = y + jnp.dot(a_ref[0].astype(BF16), w_ref[...], preferred_element_type=F32)
    out = _layer_norm_rows(y, g_ref[...], b_ref[...])
    p = j * tm + lax.broadcasted_iota(I32, out.shape, 0)
    o_ref[0] = jnp.where(_valid_rows(p, t_real), out, 0.0)


def _out_ln(h, acts, ws, ln_g, ln_b, t_real):
    B, TP, D = h.shape
    tm = _row_tile(TP)
    ws = [w.astype(BF16) for w in ws]
    g = ln_g.reshape(1, D).astype(F32)
    b = ln_b.reshape(1, D).astype(F32)
    tspec = lambda n: pl.BlockSpec((1, tm, n), lambda bb, j: (bb, j, 0))
    return pl.pallas_call(
        functools.partial(_out_ln_kernel, n_in=len(acts), t_real=t_real),
        grid=(B, TP // tm),
        in_specs=[tspec(D)] + [tspec(a.shape[-1]) for a in acts] + [_const_spec(w.shape) for w in ws]
        + [_const_spec(g.shape), _const_spec(b.shape)],
        out_specs=tspec(D),
        out_shape=jax.ShapeDtypeStruct((B, TP, D), F32),
        compiler_params=_cparams("parallel", "parallel"),
        name="out_proj_ln",
    )(h, *acts, *ws, g, b)


E_LANE0, G_LANE0 = 0, N_EXPERTS
R_E1, R_E2, R_RANK1, R_RANK2, R_G1, R_G2 = range(6)


def _router_kernel(h_ref, w_ref, b_ref, rec_ref, rect_ref, cnt_ref, car_ref):
    t = pl.program_id(0)

    @pl.when(t == 0)
    def _():
        car_ref[...] = jnp.zeros_like(car_ref)

    tm = h_ref.shape[0]
    lg = jnp.dot(h_ref[...], w_ref[...], preferred_element_type=F32, precision=lax.Precision.HIGHEST) + b_ref[...]
    lane = lax.broadcasted_iota(I32, lg.shape, 1)
    neg_inf = -jnp.inf

    def first_argmax(x):
        m = jnp.max(x, axis=1, keepdims=True)
        return m, jnp.min(jnp.where(x == m, lane, jnp.int32(2 ** 30)), axis=1, keepdims=True)

    glog = jnp.where((lane >= G_LANE0) & (lane < G_LANE0 + MOE_GROUPS), lg, neg_inf)
    gmax, glane = first_argmax(glog)
    grp = glane - G_LANE0
    grp_gate = 1.0 / jnp.sum(jnp.exp(glog - gmax), axis=1, keepdims=True)
    in_grp = (lane >= grp * MOE_PER_GROUP) & (lane < (grp + 1) * MOE_PER_GROUP)
    elog = jnp.where(in_grp, lg, neg_inf)
    v1, e1 = first_argmax(elog)
    v2, e2 = first_argmax(jnp.where(lane == e1, neg_inf, elog))
    ex = jnp.exp(v2 - v1)
    g1 = grp_gate / (1.0 + ex)
    g2 = grp_gate * ex / (1.0 + ex)

    oh1 = jnp.where(lane == e1, 1.0, 0.0)
    oh2 = jnp.where(lane == e2, 1.0, 0.0)
    rows = lax.broadcasted_iota(I32, (tm, tm), 0)
    cols = lax.broadcasted_iota(I32, (tm, tm), 1)
    ltri = jnp.where(cols < rows, 1.0, 0.0).astype(BF16)
    p1 = jnp.dot(ltri, oh1.astype(BF16), preferred_element_type=F32)
    p2 = jnp.dot(ltri, oh2.astype(BF16), preferred_element_type=F32)
    car = car_ref[0:1, :]
    c1 = jnp.sum(oh1, axis=0, keepdims=True)
    c2 = jnp.sum(oh2, axis=0, keepdims=True)
    rank1 = jnp.sum(oh1 * (car + p1), axis=1, keepdims=True)
    rank2 = jnp.sum(oh2 * (car + c1 + p2), axis=1, keepdims=True)
    car = car + c1 + c2
    car_ref[...] = jnp.broadcast_to(car, car_ref.shape)
    cnt_ref[...] = jnp.broadcast_to(car, cnt_ref.shape)

    rec = jnp.zeros(lg.shape, F32)
    for ln, val in ((R_E1, e1.astype(F32)), (R_E2, e2.astype(F32)), (R_RANK1, rank1), (R_RANK2, rank2),
                    (R_G1, g1), (R_G2, g2)):
        rec = jnp.where(lane == ln, val, rec)
    rec_ref[...] = rec
    rect_ref[0] = rec.T[:SUBLANES, :]


def _moe_route(hf, group_w, group_b, expert_w, expert_b):
    N, D = hf.shape
    tm = MOE_BLOCK
    nt = N // tm
    w = jnp.zeros((D, LANES), F32)
    w = w.at[:, E_LANE0:E_LANE0 + N_EXPERTS].set(expert_w).at[:, G_LANE0:G_LANE0 + MOE_GROUPS].set(group_w)
    b = jnp.zeros((1, LANES), F32)
    b = b.at[0, E_LANE0:E_LANE0 + N_EXPERTS].set(expert_b).at[0, G_LANE0:G_LANE0 + MOE_GROUPS].set(group_b)
    return pl.pallas_call(
        _router_kernel,
        grid=(nt,),
        in_specs=[pl.BlockSpec((tm, D), lambda t: (t, 0)), _const_spec(w.shape), _const_spec(b.shape)],
        out_specs=[pl.BlockSpec((tm, LANES), lambda t: (t, 0)),
                   pl.BlockSpec((1, SUBLANES, tm), lambda t: (t, 0, 0)),
                   _const_spec((SUBLANES, LANES))],
        out_shape=[jax.ShapeDtypeStruct((N, LANES), F32),
                   jax.ShapeDtypeStruct((nt, SUBLANES, tm), F32),
                   jax.ShapeDtypeStruct((SUBLANES, LANES), F32)],
        scratch_shapes=[pltpu.VMEM((SUBLANES, LANES), F32)],
        compiler_params=_cparams("arbitrary"),
        name="moe_router",
    )(hf, w, b)


def _row_copy(src, si, dst, di, sem):
    return pltpu.make_async_copy(src.at[pl.ds(si, 1)], dst.at[pl.ds(di, 1)], sem)


def _load_route(idx_hbm, idx_smem, sem, t):
    cp = pltpu.make_async_copy(idx_hbm.at[t], idx_smem, sem)
    cp.start()
    cp.wait()


def _dispatch_kernel(pstart_ref, idx_hbm, h_ref, xs_in, xs_hbm, idx_smem, isem, sem):
    del xs_in
    t = pl.program_id(0)
    tm = MOE_BLOCK
    _load_route(idx_hbm, idx_smem, isem, t)

    def issue(r, c):
        for k in range(MOE_TOPK):
            slot = pstart_ref[idx_smem[k * tm + r]] + idx_smem[(MOE_TOPK + k) * tm + r]
            _row_copy(h_ref, r, xs_hbm, slot, sem).start()
        return c

    lax.fori_loop(0, tm, issue, 0, unroll=8)

    def drain(r, c):
        for k in range(MOE_TOPK):
            _row_copy(h_ref, 0, xs_hbm, 0, sem).wait()
        return c

    lax.fori_loop(0, tm, drain, 0, unroll=8)


def _moe_dispatch(hf, idx, pstart, n_slots):
    N, D = hf.shape
    nt = N // MOE_BLOCK
    xs0 = jnp.zeros((n_slots, D), F32)
    return pl.pallas_call(
        _dispatch_kernel,
        grid_spec=pltpu.PrefetchScalarGridSpec(
            num_scalar_prefetch=1,
            grid=(nt,),
            in_specs=[pl.BlockSpec(memory_space=pl.ANY), pl.BlockSpec((MOE_BLOCK, D), lambda t, ps: (t, 0)),
                      pl.BlockSpec(memory_space=pl.ANY)],
            out_specs=pl.BlockSpec(memory_space=pl.ANY),
            scratch_shapes=[pltpu.SMEM((2 * MOE_TOPK * MOE_BLOCK,), I32), pltpu.SemaphoreType.DMA,
                            pltpu.SemaphoreType.DMA],
        ),
        out_shape=jax.ShapeDtypeStruct((n_slots, D), F32),
        input_output_aliases={3: 0},
        compiler_params=_cparams("arbitrary"),
        name="moe_dispatch",
    )(pstart, idx, hf, xs0)


def _ffn_kernel(be_ref, nb_ref, x_ref, wg_ref, wu_ref, wd_ref, o_ref):
    del be_ref

    @pl.when(pl.program_id(0) < nb_ref[0])
    def _():
        x = x_ref[...].astype(BF16)
        hg = jnp.dot(x, wg_ref[...], preferred_element_type=F32)
        hu = jnp.dot(x, wu_ref[...], preferred_element_type=F32)
        hid = (hg * jax.nn.sigmoid(hg)) * hu
        o_ref[...] = jnp.dot(hid.astype(BF16), wd_ref[...], preferred_element_type=F32)

    @pl.when(pl.program_id(0) >= nb_ref[0])
    def _():
        o_ref[...] = jnp.zeros_like(o_ref)


def _moe_ffn(xs, block_expert, nb_used, w_gate, w_up, w_down):
    P, D = xs.shape
    nb = P // MOE_BLOCK
    wspec = lambda s: pl.BlockSpec((None,) + s, lambda b, be, nbu: (be[b], 0, 0))
    return pl.pallas_call(
        _ffn_kernel,
        grid_spec=pltpu.PrefetchScalarGridSpec(
            num_scalar_prefetch=2,
            grid=(nb,),
            in_specs=[pl.BlockSpec((MOE_BLOCK, D), lambda b, be, nbu: (b, 0)),
                      wspec((D, EXPERT_FF)), wspec((D, EXPERT_FF)), wspec((EXPERT_FF, D))],
            out_specs=pl.BlockSpec((MOE_BLOCK, D), lambda b, be, nbu: (b, 0)),
        ),
        out_shape=jax.ShapeDtypeStruct((P, D), F32),
        compiler_params=_cparams("arbitrary"),
        name="moe_expert_ffn",
    )(block_expert, nb_used, xs, w_gate.astype(BF16), w_up.astype(BF16), w_down.astype(BF16))


def _combine_kernel(pstart_ref, idx_hbm, ys_hbm, h_ref, rec_ref, g_ref, b_ref, o_ref, idx_smem, ybuf, isem, sem,
                    *, tp, t_real):
    t = pl.program_id(0)
    tm = MOE_BLOCK
    _load_route(idx_hbm, idx_smem, isem, t)

    def issue(r, c):
        for k in range(MOE_TOPK):
            slot = pstart_ref[idx_smem[k * tm + r]] + idx_smem[(MOE_TOPK + k) * tm + r]
            _row_copy(ys_hbm, slot, ybuf.at[k], r, sem).start()
        return c

    lax.fori_loop(0, tm, issue, 0, unroll=8)

    def drain(r, c):
        for k in range(MOE_TOPK):
            _row_copy(ys_hbm, 0, ybuf.at[k], 0, sem).wait()
        return c

    lax.fori_loop(0, tm, drain, 0, unroll=8)

    rec = rec_ref[...]
    y = DN_ALPHA * h_ref[...] + rec[:, R_G1:R_G1 + 1] * ybuf[0] + rec[:, R_G2:R_G2 + 1] * ybuf[1]
    out = _layer_norm_rows(y, g_ref[...], b_ref[...])
    n = (t * tm + lax.broadcasted_iota(I32, out.shape, 0)).astype(F32)
    p = n - jnp.floor((n + 0.5) / tp) * tp
    o_ref[...] = jnp.where((p >= FRONT) & (p < FRONT + t_real), out, 0.0)


def _moe_combine(hf, ys, idx, rec, pstart, ln_g, ln_b, tp, t_real):
    N, D = hf.shape
    nt = N // MOE_BLOCK
    g = ln_g.reshape(1, D).astype(F32)
    b = ln_b.reshape(1, D).astype(F32)
    anyspec = pl.BlockSpec(memory_space=pl.ANY)
    return pl.pallas_call(
        functools.partial(_combine_kernel, tp=tp, t_real=t_real),
        grid_spec=pltpu.PrefetchScalarGridSpec(
            num_scalar_prefetch=1,
            grid=(nt,),
            in_specs=[anyspec, anyspec,
                      pl.BlockSpec((MOE_BLOCK, D), lambda t, ps: (t, 0)),
                      pl.BlockSpec((MOE_BLOCK, LANES), lambda t, ps: (t, 0)),
                      pl.BlockSpec((1, D), lambda t, ps: (0, 0)), pl.BlockSpec((1, D), lambda t, ps: (0, 0))],
            out_specs=pl.BlockSpec((MOE_BLOCK, D), lambda t, ps: (t, 0)),
            scratch_shapes=[pltpu.SMEM((2 * MOE_TOPK * MOE_BLOCK,), I32),
                            pltpu.VMEM((MOE_TOPK, MOE_BLOCK, D), F32),
                            pltpu.SemaphoreType.DMA, pltpu.SemaphoreType.DMA],
        ),
        out_shape=jax.ShapeDtypeStruct((N, D), F32),
        compiler_params=_cparams("arbitrary"),
        name="moe_combine_ln",
    )(pstart, idx, ys, hf, rec, g, b)


def _moe_layer(h, group_w, group_b, expert_w, expert_b, w_gate, w_up, w_down, ln_g, ln_b, t_real):
    B, TP, D = h.shape
    hf = h.reshape(B * TP, D)
    N = B * TP
    rec, rect, cnt = _moe_route(hf, group_w, group_b, expert_w, expert_b)
    counts = cnt[0, :N_EXPERTS].astype(I32)
    padded = (counts + MOE_BLOCK - 1) // MOE_BLOCK * MOE_BLOCK
    pend = jnp.cumsum(padded)
    pstart = (pend - padded).astype(I32)
    n_blocks = (N * MOE_TOPK) // MOE_BLOCK + N_EXPERTS
    block_start = jnp.arange(n_blocks, dtype=I32) * MOE_BLOCK
    block_expert = jnp.minimum(jnp.searchsorted(pend, block_start, side='right'), N_EXPERTS - 1).astype(I32)
    nb_used = (pend[-1:] // MOE_BLOCK).astype(I32)
    idx = rect[:, :2 * MOE_TOPK, :].astype(I32).reshape(N // MOE_BLOCK, 2 * MOE_TOPK * MOE_BLOCK)
    xs = _moe_dispatch(hf, idx, pstart, n_blocks * MOE_BLOCK)
    ys = _moe_ffn(xs, block_expert, nb_used, w_gate, w_up, w_down)
    out = _moe_combine(hf, ys, idx, rec, pstart, ln_g, ln_b, TP, t_real)
    return out.reshape(B, TP, D)


def _even_layer(h, w_in, kv_norm, w_uk, w_uv, conv_w, conv_b, rg_wa, rg_ba, rg_wx, rg_bx, rg_lambda, w_out,
                ln_g, ln_b, t_real):
    q, qi, ki, wi, gate, xb, kv = _even_in_proj(h, w_in, kv_norm, w_uk, w_uv)
    attn = _dsa(q, qi, wi, ki, kv, t_real)
    rec = _griffin(xb, gate, conv_w, conv_b, rg_wa, rg_ba, rg_wx, rg_bx, rg_lambda, t_real)
    return _out_ln(h, [attn, rec], [w_out[:A_OUT], w_out[A_OUT:]], ln_g, ln_b, t_real)


def _odd_in_kernel(h_ref, wq_ref, wk_ref, wv_ref, wz_ref, wa_ref, wb_ref, q_ref, k_ref, v_ref, z_ref, a_ref, b_ref):
    a = h_ref[0].astype(BF16)
    for w_ref, o_ref in ((wq_ref, q_ref), (wk_ref, k_ref), (wv_ref, v_ref), (wz_ref, z_ref), (wa_ref, a_ref),
                         (wb_ref, b_ref)):
        o_ref[0] = jnp.dot(a, w_ref[...], preferred_element_type=F32)


def _odd_in_proj(h, w_in):
    B, TP, D = h.shape
    tm = _row_tile(TP) // 2 if _row_tile(TP) % 16 == 0 else _row_tile(TP)
    wb = w_in.astype(BF16)
    o = [0, GDN_QK_WIDTH, 2 * GDN_QK_WIDTH, 2 * GDN_QK_WIDTH + GDN_V_WIDTH, 2 * GDN_QK_WIDTH + 2 * GDN_V_WIDTH]
    o += [o[-1] + GDN_V_HEADS, o[-1] + 2 * GDN_V_HEADS]
    ws = [wb[:, o[i]:o[i + 1]] for i in range(4)]
    ws += [jnp.pad(wb[:, o[i]:o[i + 1]], ((0, 0), (0, LANES - GDN_V_HEADS))) for i in (4, 5)]
    widths = [GDN_QK_WIDTH, GDN_QK_WIDTH, GDN_V_WIDTH, GDN_V_WIDTH, LANES, LANES]
    return pl.pallas_call(
        _odd_in_kernel,
        grid=(B, TP // tm),
        in_specs=[pl.BlockSpec((1, tm, D), lambda b, j: (b, j, 0))] + [_const_spec(w.shape) for w in ws],
        out_specs=[pl.BlockSpec((1, tm, n), lambda b, j: (b, j, 0)) for n in widths],
        out_shape=[jax.ShapeDtypeStruct((B, TP, n), F32) for n in widths],
        compiler_params=_cparams("parallel", "parallel"),
        name="odd_in_proj",
    )(h, *ws)


def _conv_silu(x, tail_ref, cw):
    tt = x.shape[0]
    win = jnp.concatenate([tail_ref[...], x], axis=0)
    tail_ref[...] = x[tt - SUBLANES:, :]
    y = jnp.zeros_like(x)
    for t in range(CONV_WIDTH):
        off = SUBLANES - (CONV_WIDTH - 1) + t
        y = y + cw[t:t + 1, :] * win[off:off + tt, :]
    return y * jax.nn.sigmoid(y)


def _gdn_prep_kernel(q_ref, k_ref, v_ref, a_ref, b_ref, cwq_ref, cwk_ref, cwv_ref, alog_ref, dtb_ref,
                     u_ref, w_ref, qg_ref, kg_ref, at_ref, e_ref,
                     tq_ref, tk_ref, tv_ref, qs_ref, ks_ref, vs_ref, *, t_real):
    j = pl.program_id(1)
    tt = q_ref.shape[1]
    hd = GDN_HEAD_DIM
    half = CHUNK

    @pl.when(j == 0)
    def _():
        tq_ref[...] = jnp.zeros_like(tq_ref)
        tk_ref[...] = jnp.zeros_like(tk_ref)
        tv_ref[...] = jnp.zeros_like(tv_ref)

    rows1 = j * tt + lax.broadcasted_iota(I32, (tt, 1), 0)
    valid = _valid_rows(rows1, t_real)
    q = jnp.where(valid, _conv_silu(q_ref[0], tq_ref, cwq_ref[...]), 0.0)
    k = jnp.where(valid, _conv_silu(k_ref[0], tk_ref, cwk_ref[...]), 0.0)
    vs_ref[...] = jnp.where(valid, _conv_silu(v_ref[0], tv_ref, cwv_ref[...]), 0.0)
    for h in range(GDN_K_HEADS):
        sl = slice(h * hd, (h + 1) * hd)
        qh, kh = q[:, sl], k[:, sl]
        qs_ref[:, sl] = qh * lax.rsqrt(jnp.sum(qh * qh, axis=1, keepdims=True) + 1e-6) * (hd ** -0.5)
        ks_ref[:, sl] = kh * lax.rsqrt(jnp.sum(kh * kh, axis=1, keepdims=True) + 1e-6)

    lane = lax.broadcasted_iota(I32, (tt, LANES), 1)
    rowi = lax.broadcasted_iota(I32, (tt, LANES), 0)
    live = valid & (lane < GDN_V_HEADS)
    g = jnp.where(live, -jnp.exp(alog_ref[...]) * _softplus(a_ref[0] + dtb_ref[...]), 0.0)
    beta = jnp.where(live, jax.nn.sigmoid(b_ref[0]), 0.0)
    gc = g
    s = 1
    while s < CHUNK:
        gc = gc + jnp.where((rowi % CHUNK) >= s, pltpu.roll(gc, s, 0), 0.0)
        s *= 2
    gc_next = pltpu.roll(gc, LANES - 1, 1)
    n_ch = tt // CHUNK
    e_rows = []
    for c in range(n_ch):
        e_rows.append(jnp.broadcast_to(jnp.exp(gc[(c + 1) * CHUNK - 1:(c + 1) * CHUNK, :]), (CHUNK, LANES)))
    e_ref[0] = jnp.concatenate(e_rows, axis=0)

    nt = (((1,), (1,)), ((), ()))
    l2 = lax.broadcasted_iota(I32, (CHUNK, 2 * half), 1)
    ii = lax.broadcasted_iota(I32, (CHUNK, 2 * half), 0)
    jj = l2 % half
    left = l2 < half
    r2 = lax.broadcasted_iota(I32, (2 * half, 2 * half), 0)
    c2 = lax.broadcasted_iota(I32, (2 * half, 2 * half), 1)
    eye2 = jnp.where(r2 == c2, 1.0, 0.0)
    bdot = lambda x, y: jnp.dot(x.astype(BF16), y.astype(BF16), preferred_element_type=F32)
    for c in range(n_ch):
        rs = slice(c * CHUNK, (c + 1) * CHUNK)
        gc_c, beta_c = gc[rs], beta[rs]
        gt = jnp.concatenate([gc_c, gc_next[rs]], axis=0).T
        glast = gc_c[CHUNK - 1:CHUNK, :]
        pairs = range(GDN_K_HEADS)
        ms, tinvs, rhss = [], [], []
        for pr in pairs:
            ha, hb = 2 * pr, 2 * pr + 1
            ksl = slice(pr * hd, (pr + 1) * hd)
            osl = slice(ha * hd, (hb + 1) * hd)
            kc = ks_ref[rs, ksl]
            qc = qs_ref[rs, ksl]
            kc16 = kc.astype(BF16)
            k2 = jnp.concatenate([kc16, kc16], axis=0)
            kk2 = lax.dot_general(kc16, k2, nt, preferred_element_type=F32)
            qk2 = lax.dot_general(qc.astype(BF16), k2, nt, preferred_element_type=F32)
            gca, gcb = gc_c[:, ha:ha + 1], gc_c[:, hb:hb + 1]
            ba, bb = beta_c[:, ha:ha + 1], beta_c[:, hb:hb + 1]
            gcol2 = jnp.where(left, gca, gcb)
            bcol2 = jnp.where(left, ba, bb)
            decay2 = jnp.exp(jnp.where(ii >= jj, gcol2 - gt[ha:ha + 1, :], -jnp.inf))
            m2 = jnp.where(ii > jj, bcol2 * kk2 * decay2, 0.0)
            at_ref[0, rs, pr * 2 * half:(pr + 1) * 2 * half] = (qk2 * decay2).astype(BF16)
            m = jnp.concatenate([jnp.where(left, m2, 0.0), jnp.where(left, 0.0, m2)], axis=0)
            ms.append(m)
            tinvs.append(eye2 - m)
            ega, egb = jnp.exp(gca), jnp.exp(gcb)
            va = vs_ref[rs, ha * hd:(ha + 1) * hd]
            vb = vs_ref[rs, hb * hd:(hb + 1) * hd]
            rhss.append(jnp.concatenate([jnp.concatenate([va * ba, kc * (ba * ega)], axis=1),
                                         jnp.concatenate([vb * bb, kc * (bb * egb)], axis=1)],
                                        axis=0).astype(BF16))
            qg_ref[0, rs, osl] = jnp.concatenate([qc * ega, qc * egb], axis=1).astype(BF16)
            kg_ref[0, rs, osl] = jnp.concatenate([kc * jnp.exp(glast[:, ha:ha + 1] - gca),
                                                  kc * jnp.exp(glast[:, hb:hb + 1] - gcb)], axis=1).astype(BF16)
        pws = [bdot(m, m) for m in ms]
        n_fac = CHUNK.bit_length() - 2
        for f in range(n_fac):
            if f + 1 < n_fac:
                both = [bdot(jnp.concatenate([t, p], axis=0), p) for t, p in zip(tinvs, pws)]
                tinvs = [t + r[:2 * half] for t, r in zip(tinvs, both)]
                pws = [r[2 * half:] for r in both]
            else:
                tinvs = [t + bdot(t, p) for t, p in zip(tinvs, pws)]
        sols = [jnp.dot(t.astype(BF16), r, preferred_element_type=F32) for t, r in zip(tinvs, rhss)]
        for pr, sol in zip(pairs, sols):
            osl = slice(2 * pr * hd, (2 * pr + 2) * hd)
            u_ref[0, rs, osl] = jnp.concatenate([sol[:CHUNK, :hd], sol[CHUNK:, :hd]], axis=1)
            w_ref[0, rs, osl] = jnp.concatenate([sol[:CHUNK, hd:], sol[CHUNK:, hd:]], axis=1).astype(BF16)


def _gdn_prep(q, k, v, a, b, conv_w, a_log, dt_bias, t_real):
    B, TP, _ = q.shape
    tt = TIME_TILE
    cw = jnp.pad(conv_w.astype(F32), ((0, SUBLANES - CONV_WIDTH), (0, 0)))
    cwq, cwk, cwv = cw[:, :GDN_QK_WIDTH], cw[:, GDN_QK_WIDTH:2 * GDN_QK_WIDTH], cw[:, 2 * GDN_QK_WIDTH:]
    alog = jnp.pad(a_log.astype(F32), (0, LANES - GDN_V_HEADS)).reshape(1, LANES)
    dtb = jnp.pad(dt_bias.astype(F32), (0, LANES - GDN_V_HEADS)).reshape(1, LANES)
    tspec = lambda n: pl.BlockSpec((1, tt, n), lambda bb, j: (bb, j, 0))
    ws = (cwq, cwk, cwv, alog, dtb)
    VW, QW = GDN_V_WIDTH, GDN_QK_WIDTH
    outs = [(VW, F32), (VW, BF16), (VW, BF16), (VW, BF16), (GDN_K_HEADS * 2 * CHUNK, BF16), (LANES, F32)]
    return pl.pallas_call(
        functools.partial(_gdn_prep_kernel, t_real=t_real),
        grid=(B, TP // tt),
        in_specs=[tspec(QW), tspec(QW), tspec(VW), tspec(LANES), tspec(LANES)] + [_const_spec(w.shape) for w in ws],
        out_specs=[tspec(n) for n, _ in outs],
        out_shape=[jax.ShapeDtypeStruct((B, TP, n), dt) for n, dt in outs],
        scratch_shapes=[pltpu.VMEM((SUBLANES, QW), F32), pltpu.VMEM((SUBLANES, QW), F32),
                        pltpu.VMEM((SUBLANES, VW), F32), pltpu.VMEM((tt, QW), F32), pltpu.VMEM((tt, QW), F32),
                        pltpu.VMEM((tt, VW), F32)],
        compiler_params=_cparams("parallel", "arbitrary"),
        name="gdn_prep",
    )(q, k, v, a, b, *ws)


def _gdn_scan_kernel(u_ref, w_ref, qg_ref, kg_ref, at_ref, e_ref, z_ref, on_ref, o_ref, s_ref):
    j = pl.program_id(1)
    tt = u_ref.shape[1]
    hd = GDN_HEAD_DIM

    @pl.when(j == 0)
    def _():
        s_ref[...] = jnp.zeros_like(s_ref)

    lane = lax.broadcasted_iota(I32, (CHUNK, 2 * CHUNK), 1)
    tn = (((0,), (0,)), ((), ()))
    heads = range(GDN_V_HEADS)
    hsl = [slice(h * hd, (h + 1) * hd) for h in heads]
    for c in range(tt // CHUNK):
        rs = slice(c * CHUNK, (c + 1) * CHUNK)
        e_row = e_ref[0, c * CHUNK:c * CHUNK + 1, :]
        ss = [s_ref[h] for h in heads]
        ws = [jnp.dot(jnp.concatenate([w_ref[0, rs, hsl[h]], qg_ref[0, rs, hsl[h]]], axis=0), ss[h].astype(BF16),
                      preferred_element_type=F32) for h in heads]
        v16 = [(u_ref[0, rs, hsl[h]] - ws[h][:CHUNK]).astype(BF16) for h in heads]
        intra = []
        for pr in range(GDN_K_HEADS):
            at2 = at_ref[0, rs, pr * 2 * CHUNK:(pr + 1) * 2 * CHUNK]
            zero = jnp.zeros_like(at2)
            lhs = jnp.concatenate([jnp.where(lane < CHUNK, at2, zero), jnp.where(lane >= CHUNK, at2, zero)], axis=0)
            rhs = jnp.concatenate([v16[2 * pr], v16[2 * pr + 1]], axis=0)
            intra.append(jnp.dot(lhs, rhs, preferred_element_type=F32))
        for h in heads:
            s_ref[h] = ss[h] * e_row[:, h:h + 1] + lax.dot_general(kg_ref[0, rs, hsl[h]], v16[h], tn,
                                                                   preferred_element_type=F32)
        for h in heads:
            o = ws[h][CHUNK:] + intra[h // 2][(h % 2) * CHUNK:(h % 2 + 1) * CHUNK]
            on = o * lax.rsqrt(jnp.mean(o * o, axis=1, keepdims=True) + 1e-6) * on_ref[...]
            z = z_ref[0, rs, hsl[h]]
            o_ref[0, rs, hsl[h]] = (on * (z * jax.nn.sigmoid(z))).astype(BF16)


def _gdn_scan(u, w, qg, kg, at, e, z, o_norm):
    B, TP, VW = u.shape
    tt = TIME_TILE
    tspec = lambda n: pl.BlockSpec((1, tt, n), lambda bb, j: (bb, j, 0))
    on = o_norm.reshape(1, GDN_HEAD_DIM).astype(F32)
    return pl.pallas_call(
        _gdn_scan_kernel,
        grid=(B, TP // tt),
        in_specs=[tspec(VW), tspec(VW), tspec(VW), tspec(VW), tspec(at.shape[-1]), tspec(LANES), tspec(VW),
                  _const_spec(on.shape)],
        out_specs=tspec(VW),
        out_shape=jax.ShapeDtypeStruct((B, TP, VW), BF16),
        scratch_shapes=[pltpu.VMEM((GDN_V_HEADS, GDN_HEAD_DIM, GDN_HEAD_DIM), F32)],
        compiler_params=_cparams("parallel", "arbitrary"),
        name="gdn_scan",
    )(u, w, qg, kg, at, e, z, on)


def _odd_layer(h, w_in, conv_w, a_log, dt_bias, o_norm, w_out, ln_g, ln_b, t_real):
    q, k, v, z, a, b = _odd_in_proj(h, w_in)
    u, w, qg, kg, at, e = _gdn_prep(q, k, v, a, b, conv_w, a_log, dt_bias, t_real)
    gated = _gdn_scan(u, w, qg, kg, at, e, z, o_norm)
    return _out_ln(h, [gated], [w_out], ln_g, ln_b, t_real)


def kernel(x, meta_tokens, even_w_in, even_kv_norm, even_w_uk, even_w_uv, even_conv_w, even_conv_b, even_rg_wa,
           even_rg_ba, even_rg_wx, even_rg_bx, even_rg_lambda, even_w_out, odd_w_in, odd_conv_w, odd_a_log,
           odd_dt_bias, odd_o_norm, odd_w_out, ln_g, ln_b, moe_group_w, moe_group_b, moe_expert_w, moe_expert_b,
           moe_w_gate, moe_w_up, moe_w_down):
    B, S, D = x.shape
    t_real = N_META + S
    tp = -(-(FRONT + t_real) // TIME_TILE) * TIME_TILE
    meta = jnp.broadcast_to(meta_tokens.astype(x.dtype)[None], (B, N_META, D))
    h = jnp.concatenate([jnp.zeros((B, FRONT, D), x.dtype), meta, x,
                         jnp.zeros((B, tp - FRONT - t_real, D), x.dtype)], axis=1)
    for layer in range(DEPTH):
        i = layer // 2
        if layer % 2 == 0:
            h = _even_layer(h, even_w_in[i], even_kv_norm[i], even_w_uk[i], even_w_uv[i], even_conv_w[i],
                            even_conv_b[i], even_rg_wa[i], even_rg_ba[i], even_rg_wx[i], even_rg_bx[i],
                            even_rg_lambda[i], even_w_out[i], ln_g[layer, 0], ln_b[layer, 0], t_real)
        else:
            h = _odd_layer(h, odd_w_in[i], odd_conv_w[i], odd_a_log[i], odd_dt_bias[i], odd_o_norm[i],
                           odd_w_out[i], ln_g[layer, 0], ln_b[layer, 0], t_real)
        h = _moe_layer(h, moe_group_w[layer], moe_group_b[layer], moe_expert_w[layer], moe_expert_b[layer],
                       moe_w_gate[layer], moe_w_up[layer], moe_w_down[layer], ln_g[layer, 1], ln_b[layer, 1], t_real)
    return h[:, FRONT:FRONT + t_real][:, N_META:]
```

```python
import functools
import math

import jax
import jax.numpy as jnp
from jax import lax
from jax.experimental import pallas as pl
from jax.experimental.pallas import tpu as pltpu

F32 = jnp.float32
BF16 = jnp.bfloat16
I32 = jnp.int32

D_MODEL = 1024
DEPTH = 2
N_META = 16
DN_ALPHA = (2 * DEPTH) ** 0.25

A_HEADS = 4
A_HEAD_DIM = 128
A_OUT = A_HEADS * A_HEAD_DIM
A_KV_RANK = 256
IDX_HEADS = 8
IDX_DIM = 64
IDX_TOPK_CAP = 256

RG_WIDTH = 512
RG_BLOCKS = 8
RG_C = 8.0
CONV_WIDTH = 4

GDN_K_HEADS = 8
GDN_V_HEADS = 16
GDN_HEAD_DIM = 128
GDN_QK_WIDTH = GDN_K_HEADS * GDN_HEAD_DIM
GDN_V_WIDTH = GDN_V_HEADS * GDN_HEAD_DIM
CHUNK = 64

MOE_GROUPS = 4
MOE_PER_GROUP = 8
N_EXPERTS = MOE_GROUPS * MOE_PER_GROUP
MOE_TOPK = 2
EXPERT_FF = 512
MOE_BLOCK = 256

LANES = 128
SUBLANES = 8
TIME_TILE = 128
FRONT = (-N_META) % CHUNK
NEG_BIG = -1e30
INT_MIN = -2 ** 31
VMEM_LIMIT = 56 * 1024 * 1024


def _cparams(*sem, flags=None):
    return pltpu.CompilerParams(dimension_semantics=sem, vmem_limit_bytes=VMEM_LIMIT, flags=flags)


def _const_spec(shape):
    nd = len(shape)
    return pl.BlockSpec(shape, lambda *_: (0,) * nd)


def _row_tile(tp):
    best = SUBLANES
    for t in range(SUBLANES, 641, SUBLANES):
        if tp % t == 0:
            best = t
    return best


def _valid_rows(p, t_real):
    return (p >= FRONT) & (p < FRONT + t_real)


def _layer_norm_rows(y, g, b):
    mu = jnp.mean(y, axis=-1, keepdims=True)
    yc = y - mu
    var = jnp.mean(yc * yc, axis=-1, keepdims=True)
    return yc * lax.rsqrt(var + 1e-5) * g + b


def _softplus(z):
    return jnp.maximum(z, 0.0) + jnp.log1p(jnp.exp(-jnp.abs(z)))


def _even_in_kernel(h_ref, wq_ref, wc_ref, wqi_ref, wki_ref, wwi_ref, wg_ref, wx_ref, kvn_ref, wukv_ref,
                    q_ref, qi_ref, ki_ref, wi_ref, gate_ref, xb_ref, kv_ref):
    a = h_ref[0].astype(BF16)
    dot = functools.partial(jnp.dot, preferred_element_type=F32)
    q_ref[0] = dot(a, wq_ref[...]).astype(BF16)
    qi_ref[0] = dot(a, wqi_ref[...]).astype(BF16)
    ki_ref[0] = dot(a, wki_ref[...]).astype(BF16)
    wi_ref[0] = dot(a, wwi_ref[...]) * (IDX_HEADS ** -0.5) * (IDX_DIM ** -0.5)
    gate_ref[0] = dot(a, wg_ref[...])
    xb_ref[0] = dot(a, wx_ref[...])
    c = dot(a, wc_ref[...])
    latent = c * lax.rsqrt(jnp.mean(c * c, axis=-1, keepdims=True) + 1e-6) * kvn_ref[...]
    kv_ref[0] = dot(latent.astype(BF16), wukv_ref[...]).astype(BF16)


def _even_in_proj(h, w_in, kv_norm, w_uk, w_uv):
    B, TP, D = h.shape
    tm = _row_tile(TP)
    o = [0]
    for p in (A_OUT, A_KV_RANK, IDX_HEADS * IDX_DIM, IDX_DIM, IDX_HEADS, RG_WIDTH, RG_WIDTH):
        o.append(o[-1] + p)
    wb = w_in.astype(BF16)
    wq, wc = wb[:, o[0]:o[1]], wb[:, o[1]:o[2]]
    wqi = wb[:, o[2]:o[3]].reshape(D, IDX_HEADS, IDX_DIM)
    wqi = jnp.pad(wqi, ((0, 0), (0, 0), (0, LANES - IDX_DIM))).reshape(D, IDX_HEADS * LANES)
    wki = jnp.pad(wb[:, o[3]:o[4]], ((0, 0), (0, LANES - IDX_DIM)))
    wwi = jnp.pad(wb[:, o[4]:o[5]], ((0, 0), (0, LANES - IDX_HEADS)))
    wg, wx = wb[:, o[5]:o[6]], wb[:, o[6]:o[7]]
    wukv = jnp.concatenate([w_uk, w_uv], axis=1).astype(BF16)
    kvn = kv_norm.reshape(1, A_KV_RANK).astype(F32)
    ws = (wq, wc, wqi, wki, wwi, wg, wx, kvn, wukv)
    outs = [(A_OUT, BF16), (IDX_HEADS * LANES, BF16), (LANES, BF16), (LANES, F32), (RG_WIDTH, F32),
            (RG_WIDTH, F32), (2 * A_HEAD_DIM, BF16)]
    return pl.pallas_call(
        _even_in_kernel,
        grid=(B, TP // tm),
        in_specs=[pl.BlockSpec((1, tm, D), lambda b, j: (b, j, 0))] + [_const_spec(w.shape) for w in ws],
        out_specs=[pl.BlockSpec((1, tm, n), lambda b, j: (b, j, 0)) for n, _ in outs],
        out_shape=[jax.ShapeDtypeStruct((B, TP, n), dt) for n, dt in outs],
        compiler_params=_cparams("parallel", "parallel"),
        name="even_in_proj",
    )(h, *ws)


def _dsa_body(nk, q_ref, qi_ref, wi_ref, ki_ref, kv_ref, o_ref, key_ref, bias_ref, vt_ref, n_sel, t_real):
    i = pl.program_id(1)
    tq = q_ref.shape[1]
    qi = qi_ref[0]
    ki = ki_ref[0, :nk, :]
    wit = wi_ref[0].T
    nt = (((1,), (1,)), ((), ()))
    score = jnp.zeros((nk, tq), F32)
    for h in range(IDX_HEADS):
        s = lax.dot_general(ki, qi[:, h * LANES:(h + 1) * LANES], nt, preferred_element_type=F32)
        score = score + wit[h:h + 1, :] * jnp.maximum(s, 0.0)

    kpos = lax.broadcasted_iota(I32, (nk, tq), 0)
    qpos = i * tq + lax.broadcasted_iota(I32, (nk, tq), 1)
    bits = lax.bitcast_convert_type(score, I32)
    key = jnp.where(bits < 0, bits ^ jnp.int32(0x7FFFFFFF), bits)
    key_ref[:nk, :] = jnp.where(kpos <= qpos, key, jnp.int32(INT_MIN))
    key_ref[0:FRONT, :] = jnp.full((FRONT, tq), INT_MIN, I32)
    if nk > FRONT + t_real:
        key_ref[FRONT + t_real:nk, :] = jnp.full((nk - FRONT - t_real, tq), INT_MIN, I32)

    def over_keys(x, op):
        pair = jnp.add if op is jnp.sum else jnp.maximum
        group = LANES // SUBLANES
        x = x.reshape(nk // LANES, group, SUBLANES, x.shape[1])
        while x.shape[1] > 1:
            half = x.shape[1] // 2
            x = pair(x[:, :half], x[:, half:])
        return op(op(x[:, 0], axis=0), axis=0, keepdims=True)

    def count(mask):
        return over_keys(jnp.where(mask, 1, 0).astype(I32), jnp.sum)

    def any_query(mask):
        return jnp.max(jnp.where(mask, 1, 0))

    n_vis = count(key_ref[:nk, :] > jnp.int32(INT_MIN))

    bits_per_check = 8

    def bit_cond(c):
        return (c[0] < 32) & (c[3] > 0)

    def bit_group(c):
        it0, cand, cnt_c, _ = c

        def bit_step(b, cc):
            cand, cnt_c = cc
            trial = cand + jnp.left_shift(jnp.int32(1), 31 - (it0 + b))
            cnt = count(key_ref[:nk, :] >= trial)
            take = cnt >= n_sel
            return jnp.where(take, trial, cand), jnp.where(take, cnt, cnt_c)

        cand, cnt_c = lax.fori_loop(0, bits_per_check, bit_step, (cand, cnt_c))
        return it0 + bits_per_check, cand, cnt_c, any_query((cnt_c != n_sel) & (n_vis > n_sel))

    start = (jnp.int32(0), jnp.full((1, tq), INT_MIN, I32), jnp.full((1, tq), nk, I32),
             any_query(n_vis > n_sel))
    _, thr, cnt_ge, _ = lax.while_loop(bit_cond, bit_group, start)
    keys = key_ref[:nk, :]
    bias_ref[:nk, :] = jnp.where(keys >= jnp.maximum(thr, jnp.int32(INT_MIN + 1)), 0.0, NEG_BIG)
    tie_queries = (cnt_ge > n_sel) & (thr > jnp.int32(INT_MIN))

    @pl.when(any_query(tie_queries) > 0)
    def _():
        keys = key_ref[:nk, :]
        eq = keys == thr
        need = n_sel - count(keys > thr)
        nbits = max(1, (nk - 1).bit_length())

        def idx_step(it, cand):
            trial = cand + jnp.left_shift(jnp.int32(1), nbits - 1 - it)
            return jnp.where(count(eq & (kpos < trial)) < need, trial, cand)

        last = lax.fori_loop(0, nbits, idx_step, jnp.zeros((1, tq), I32))
        sel = (keys > thr) | (eq & (kpos <= last))
        bias_ref[:nk, :] = jnp.where(sel & (keys > jnp.int32(INT_MIN)), 0.0, NEG_BIG)

    bias = bias_ref[:nk, :]
    q = q_ref[0]
    k = kv_ref[0, :nk, :A_HEAD_DIM]
    qs = jnp.concatenate([q[:, h * A_HEAD_DIM:(h + 1) * A_HEAD_DIM] for h in range(A_HEADS)], axis=0)
    lg = lax.dot_general(k, qs, nt, preferred_element_type=F32) * (A_HEAD_DIM ** -0.5)
    lg = lg + jnp.concatenate([bias] * A_HEADS, axis=1)
    m = over_keys(lg, jnp.max)
    p = jnp.exp(lg - m)
    l = over_keys(p, jnp.sum)
    ot = jnp.dot(vt_ref[:, :nk], p.astype(BF16), preferred_element_type=F32) / l
    for h in range(A_HEADS):
        o_ref[0, :, h * A_HEAD_DIM:(h + 1) * A_HEAD_DIM] = ot[:, h * tq:(h + 1) * tq].T.astype(BF16)


DSA_WIDTH_STEP = 3


def _dsa_kernel(q_ref, qi_ref, wi_ref, ki_ref, kv_ref, o_ref, key_ref, bias_ref, vt_ref, *, n_sel, t_real):
    i = pl.program_id(1)
    tq = q_ref.shape[1]
    tp = ki_ref.shape[1]
    n_tiles = tp // tq

    @pl.when(i == 0)
    def _():
        vt_ref[...] = kv_ref[0, :, A_HEAD_DIM:].astype(F32).T.astype(BF16)

    lo = 0
    while lo < n_tiles:
        hi = min(lo + DSA_WIDTH_STEP, n_tiles)

        @pl.when((i >= lo) & (i < hi))
        def _(hi=hi):
            _dsa_body(hi * tq, q_ref, qi_ref, wi_ref, ki_ref, kv_ref, o_ref, key_ref, bias_ref, vt_ref,
                      n_sel, t_real)

        lo = hi


def _dsa(q, qi, wi, ki, kv, t_real):
    B, TP, _ = q.shape
    tq = TIME_TILE
    n_sel = min(IDX_TOPK_CAP, t_real // 4)
    kern = functools.partial(_dsa_kernel, n_sel=n_sel, t_real=t_real)
    qspec = lambda n: pl.BlockSpec((1, tq, n), lambda b, i: (b, i, 0))
    kspec = lambda n: pl.BlockSpec((1, TP, n), lambda b, i: (b, 0, 0))
    return pl.pallas_call(
        kern,
        grid=(B, TP // tq),
        in_specs=[qspec(A_OUT), qspec(IDX_HEADS * LANES), qspec(LANES), kspec(LANES), kspec(2 * A_HEAD_DIM)],
        out_specs=qspec(A_OUT),
        out_shape=jax.ShapeDtypeStruct((B, TP, A_OUT), BF16),
        scratch_shapes=[pltpu.VMEM((TP, tq), I32), pltpu.VMEM((TP, tq), F32), pltpu.VMEM((A_HEAD_DIM, TP), BF16)],
        compiler_params=_cparams("parallel", "arbitrary"),
        name="dsa_attention",
    )(q, qi, wi, ki, kv)


def _shift_rows(x, s, fill):
    rows = lax.broadcasted_iota(I32, x.shape, 0)
    return jnp.where(rows >= s, pltpu.roll(x, s, 0), fill)


def _causal_conv(x, win_ref, cw):
    tt = x.shape[0]
    win_ref[0:SUBLANES, :] = win_ref[tt:tt + SUBLANES, :]
    win_ref[SUBLANES:, :] = x
    y = cw[CONV_WIDTH - 1:CONV_WIDTH, :] * x
    for t in range(CONV_WIDTH - 1):
        off = SUBLANES - (CONV_WIDTH - 1) + t
        y = y + cw[t:t + 1, :] * win_ref[off:off + tt, :]
    return y


def _griffin_kernel(xb_ref, gate_ref, cw_ref, cb_ref, wg_ref, bg_ref, lam_ref, o_ref, tail_ref, car_ref, *, t_real):
    j = pl.program_id(1)
    tt = xb_ref.shape[1]

    @pl.when(j == 0)
    def _():
        tail_ref[...] = jnp.zeros_like(tail_ref)
        car_ref[...] = jnp.zeros_like(car_ref)

    xr = _causal_conv(xb_ref[0], tail_ref, cw_ref[...]) + cb_ref[...]
    g = jnp.dot(xr.astype(BF16), wg_ref[...], preferred_element_type=F32) + bg_ref[...]
    r = jax.nn.sigmoid(g[:, :RG_WIDTH])
    ig = jax.nn.sigmoid(g[:, RG_WIDTH:])
    log_a = -RG_C * r * _softplus(-lam_ref[...])
    a = jnp.exp(log_a)
    th = jnp.tanh(log_a)
    u = jnp.sqrt(-2.0 * th / (1.0 - th)) * (ig * xr)
    p = j * tt + lax.broadcasted_iota(I32, xr.shape, 0)
    u = jnp.where(_valid_rows(p, t_real), u, 0.0)
    s = 1
    while s < tt:
        u = u + a * _shift_rows(u, s, 0.0)
        a = a * _shift_rows(a, s, 1.0)
        s *= 2
    hcur = u + a * car_ref[0:1, :]
    car_ref[...] = jnp.broadcast_to(hcur[tt - 1:tt, :], car_ref.shape)
    o_ref[0] = hcur * jax.nn.gelu(gate_ref[0])


def _block_diag(w):
    nb, bs, _ = w.shape
    eye = jnp.eye(nb, dtype=w.dtype)
    return (w[:, :, None, :] * eye[:, None, :, None]).reshape(nb * bs, nb * bs)


def _griffin(xb, gate, conv_w, conv_b, rg_wa, rg_ba, rg_wx, rg_bx, rg_lambda, t_real):
    B, TP, R = xb.shape
    tt = TIME_TILE
    cw = jnp.pad(conv_w.astype(F32), ((0, SUBLANES - CONV_WIDTH), (0, 0)))
    cb = conv_b.reshape(1, R).astype(F32)
    wg = jnp.concatenate([_block_diag(rg_wa), _block_diag(rg_wx)], axis=1).astype(BF16)
    bg = jnp.concatenate([rg_ba, rg_bx]).reshape(1, 2 * R).astype(F32)
    lam = rg_lambda.reshape(1, R).astype(F32)
    tspec = pl.BlockSpec((1, tt, R), lambda b, j: (b, j, 0))
    ws = (cw, cb, wg, bg, lam)
    return pl.pallas_call(
        functools.partial(_griffin_kernel, t_real=t_real),
        grid=(B, TP // tt),
        in_specs=[tspec, tspec] + [_const_spec(w.shape) for w in ws],
        out_specs=tspec,
        out_shape=jax.ShapeDtypeStruct((B, TP, R), F32),
        scratch_shapes=[pltpu.VMEM((SUBLANES + tt, R), F32), pltpu.VMEM((SUBLANES, R), F32)],
        compiler_params=_cparams("parallel", "arbitrary"),
        name="griffin_rglru",
    )(xb, gate, *ws)


def _out_ln_kernel(*refs, n_in, t_real):
    h_ref = refs[0]
    a_refs = refs[1:1 + n_in]
    w_refs = refs[1 + n_in:1 + 2 * n_in]
    g_ref, b_ref, o_ref = refs[1 + 2 * n_in:]
    j = pl.program_id(1)
    tm = h_ref.shape[1]
    y = DN_ALPHA * h_ref[0]
    for a_ref, w_ref in zip(a_refs, w_refs):
        y = y + jnp.dot(a_ref[0].astype(BF16), w_ref[...], preferred_element_type=F32)
    out = _layer_norm_rows(y, g_ref[...], b_ref[...])
    p = j * tm + lax.broadcasted_iota(I32, out.shape, 0)
    o_ref[0] = jnp.where(_valid_rows(p, t_real), out, 0.0)


def _out_ln(h, acts, ws, ln_g, ln_b, t_real):
    B, TP, D = h.shape
    tm = _row_tile(TP)
    ws = [w.astype(BF16) for w in ws]
    g = ln_g.reshape(1, D).astype(F32)
    b = ln_b.reshape(1, D).astype(F32)
    tspec = lambda n: pl.BlockSpec((1, tm, n), lambda bb, j: (bb, j, 0))
    return pl.pallas_call(
        functools.partial(_out_ln_kernel, n_in=len(acts), t_real=t_real),
        grid=(B, TP // tm),
        in_specs=[tspec(D)] + [tspec(a.shape[-1]) for a in acts] + [_const_spec(w.shape) for w in ws]
        + [_const_spec(g.shape), _const_spec(b.shape)],
        out_specs=tspec(D),
        out_shape=jax.ShapeDtypeStruct((B, TP, D), F32),
        compiler_params=_cparams("parallel", "parallel"),
        name="out_proj_ln",
    )(h, *acts, *ws, g, b)


E_LANE0, G_LANE0 = 0, N_EXPERTS
R_E1, R_E2, R_RANK1, R_RANK2, R_G1, R_G2 = range(6)


def _router_kernel(h_ref, w_ref, b_ref, rec_ref, rect_ref, cnt_ref, car_ref):
    t = pl.program_id(0)

    @pl.when(t == 0)
    def _():
        car_ref[...] = jnp.zeros_like(car_ref)

    tm = h_ref.shape[0]
    h = h_ref[...]
    h_hi = h.astype(BF16)
    h_lo = (h - h_hi.astype(F32)).astype(BF16)
    both = jnp.dot(h_hi, w_ref[...], preferred_element_type=F32)
    lg = (both[:, :LANES] + both[:, LANES:]
          + jnp.dot(h_lo, w_ref[:, :LANES], preferred_element_type=F32) + b_ref[...])
    lane = lax.broadcasted_iota(I32, lg.shape, 1)
    neg_inf = -jnp.inf

    def first_argmax(x):
        m = jnp.max(x, axis=1, keepdims=True)
        return m, jnp.min(jnp.where(x == m, lane, jnp.int32(2 ** 30)), axis=1, keepdims=True)

    glog = jnp.where((lane >= G_LANE0) & (lane < G_LANE0 + MOE_GROUPS), lg, neg_inf)
    gmax, glane = first_argmax(glog)
    grp = glane - G_LANE0
    grp_gate = 1.0 / jnp.sum(jnp.exp(glog - gmax), axis=1, keepdims=True)
    in_grp = (lane >= grp * MOE_PER_GROUP) & (lane < (grp + 1) * MOE_PER_GROUP)
    elog = jnp.where(in_grp, lg, neg_inf)
    v1, e1 = first_argmax(elog)
    v2, e2 = first_argmax(jnp.where(lane == e1, neg_inf, elog))
    ex = jnp.exp(v2 - v1)
    g1 = grp_gate / (1.0 + ex)
    g2 = grp_gate * ex / (1.0 + ex)

    oh1 = jnp.where(lane == e1, 1.0, 0.0)
    oh2 = jnp.where(lane == e2, 1.0, 0.0)
    rows = lax.broadcasted_iota(I32, (tm, tm), 0)
    cols = lax.broadcasted_iota(I32, (tm, tm), 1)
    ltri = jnp.where(cols < rows, 1.0, 0.0).astype(BF16)
    p1 = jnp.dot(ltri, oh1.astype(BF16), preferred_element_type=F32)
    p2 = jnp.dot(ltri, oh2.astype(BF16), preferred_element_type=F32)
    car = car_ref[0:1, :]
    c1 = jnp.sum(oh1, axis=0, keepdims=True)
    c2 = jnp.sum(oh2, axis=0, keepdims=True)
    rank1 = jnp.sum(oh1 * (car + p1), axis=1, keepdims=True)
    rank2 = jnp.sum(oh2 * (car + c1 + p2), axis=1, keepdims=True)
    car = car + c1 + c2
    car_ref[...] = jnp.broadcast_to(car, car_ref.shape)
    cnt_ref[...] = jnp.broadcast_to(car, cnt_ref.shape)

    rec = jnp.zeros(lg.shape, F32)
    for ln, val in ((R_E1, e1.astype(F32)), (R_E2, e2.astype(F32)), (R_RANK1, rank1), (R_RANK2, rank2),
                    (R_G1, g1), (R_G2, g2)):
        rec = jnp.where(lane == ln, val, rec)
    rec_ref[...] = rec
    rect_ref[0] = rec.T[:SUBLANES, :]


def _moe_route(hf, group_w, group_b, expert_w, expert_b):
    N, D = hf.shape
    tm = MOE_BLOCK
    nt = N // tm
    w = jnp.zeros((D, LANES), F32)
    w = w.at[:, E_LANE0:E_LANE0 + N_EXPERTS].set(expert_w).at[:, G_LANE0:G_LANE0 + MOE_GROUPS].set(group_w)
    b = jnp.zeros((1, LANES), F32)
    b = b.at[0, E_LANE0:E_LANE0 + N_EXPERTS].set(expert_b).at[0, G_LANE0:G_LANE0 + MOE_GROUPS].set(group_b)
    w_hi = w.astype(BF16)
    w = jnp.concatenate([w_hi, (w - w_hi.astype(F32)).astype(BF16)], axis=1)
    return pl.pallas_call(
        _router_kernel,
        grid=(nt,),
        in_specs=[pl.BlockSpec((tm, D), lambda t: (t, 0)), _const_spec(w.shape), _const_spec(b.shape)],
        out_specs=[pl.BlockSpec((tm, LANES), lambda t: (t, 0)),
                   pl.BlockSpec((1, SUBLANES, tm), lambda t: (t, 0, 0)),
                   _const_spec((SUBLANES, LANES))],
        out_shape=[jax.ShapeDtypeStruct((N, LANES), F32),
                   jax.ShapeDtypeStruct((nt, SUBLANES, tm), F32),
                   jax.ShapeDtypeStruct((SUBLANES, LANES), F32)],
        scratch_shapes=[pltpu.VMEM((SUBLANES, LANES), F32)],
        compiler_params=_cparams("arbitrary"),
        name="moe_router",
    )(hf, w, b)


def _row_copy(src, si, dst, di, sem):
    return pltpu.make_async_copy(src.at[pl.ds(si, 1)], dst.at[pl.ds(di, 1)], sem)


def _load_route(idx_hbm, idx_smem, sem, t):
    cp = pltpu.make_async_copy(idx_hbm.at[t], idx_smem, sem)
    cp.start()
    cp.wait()


def _dispatch_kernel(pstart_ref, idx_hbm, h_ref, xs_in, xs_hbm, idx_smem, isem, sem):
    del xs_in
    t = pl.program_id(0)
    tm = MOE_BLOCK
    _load_route(idx_hbm, idx_smem, isem, t)

    for r in range(tm):
        for k in range(MOE_TOPK):
            slot = pstart_ref[idx_smem[k * tm + r]] + idx_smem[(MOE_TOPK + k) * tm + r]
            _row_copy(h_ref, r, xs_hbm, slot, sem).start()
    for _ in range(tm * MOE_TOPK):
        _row_copy(h_ref, 0, xs_hbm, 0, sem).wait()


def _moe_dispatch(hf, idx, pstart, n_slots):
    N, D = hf.shape
    nt = N // MOE_BLOCK
    xs0 = jnp.zeros((n_slots, D), F32)
    return pl.pallas_call(
        _dispatch_kernel,
        grid_spec=pltpu.PrefetchScalarGridSpec(
            num_scalar_prefetch=1,
            grid=(nt,),
            in_specs=[pl.BlockSpec(memory_space=pl.ANY), pl.BlockSpec((MOE_BLOCK, D), lambda t, ps: (t, 0)),
                      pl.BlockSpec(memory_space=pl.ANY)],
            out_specs=pl.BlockSpec(memory_space=pl.ANY),
            scratch_shapes=[pltpu.SMEM((2 * MOE_TOPK * MOE_BLOCK,), I32), pltpu.SemaphoreType.DMA,
                            pltpu.SemaphoreType.DMA],
        ),
        out_shape=jax.ShapeDtypeStruct((n_slots, D), F32),
        input_output_aliases={3: 0},
        compiler_params=_cparams("arbitrary"),
        name="moe_dispatch",
    )(pstart, idx, hf, xs0)


def _ffn_kernel(be_ref, nb_ref, x_ref, wg_ref, wu_ref, wd_ref, o_ref):
    del be_ref

    @pl.when(pl.program_id(0) < nb_ref[0])
    def _():
        x = x_ref[...].astype(BF16)
        hg = jnp.dot(x, wg_ref[...], preferred_element_type=F32)
        hu = jnp.dot(x, wu_ref[...], preferred_element_type=F32)
        hid = (hg * jax.nn.sigmoid(hg)) * hu
        o_ref[...] = jnp.dot(hid.astype(BF16), wd_ref[...], preferred_element_type=F32)

    @pl.when(pl.program_id(0) >= nb_ref[0])
    def _():
        o_ref[...] = jnp.zeros_like(o_ref)


def _moe_ffn(xs, block_expert, nb_used, w_gate, w_up, w_down):
    P, D = xs.shape
    nb = P // MOE_BLOCK
    wspec = lambda s: pl.BlockSpec((None,) + s, lambda b, be, nbu: (be[b], 0, 0))
    return pl.pallas_call(
        _ffn_kernel,
        grid_spec=pltpu.PrefetchScalarGridSpec(
            num_scalar_prefetch=2,
            grid=(nb,),
            in_specs=[pl.BlockSpec((MOE_BLOCK, D), lambda b, be, nbu: (b, 0)),
                      wspec((D, EXPERT_FF)), wspec((D, EXPERT_FF)), wspec((EXPERT_FF, D))],
            out_specs=pl.BlockSpec((MOE_BLOCK, D), lambda b, be, nbu: (b, 0)),
        ),
        out_shape=jax.ShapeDtypeStruct((P, D), F32),
        compiler_params=_cparams("arbitrary"),
        name="moe_expert_ffn",
    )(block_expert, nb_used, xs, w_gate.astype(BF16), w_up.astype(BF16), w_down.astype(BF16))


def _combine_kernel(pstart_ref, idx_hbm, ys_hbm, h_ref, rec_ref, g_ref, b_ref, o_ref, idx_smem, ybuf, isem, sem,
                    *, tp, t_real):
    t = pl.program_id(0)
    tm = MOE_BLOCK
    _load_route(idx_hbm, idx_smem, isem, t)

    half = tm // 2
    for hf in range(2):
        for r in range(hf * half, (hf + 1) * half):
            for k in range(MOE_TOPK):
                slot = pstart_ref[idx_smem[k * tm + r]] + idx_smem[(MOE_TOPK + k) * tm + r]
                _row_copy(ys_hbm, slot, ybuf.at[k], r, sem.at[hf]).start()
    for hf in range(2):
        for _ in range(half * MOE_TOPK):
            _row_copy(ys_hbm, 0, ybuf.at[0], 0, sem.at[hf]).wait()
        rs = slice(hf * half, (hf + 1) * half)
        rec = rec_ref[rs, :]
        y = (DN_ALPHA * h_ref[rs, :] + rec[:, R_G1:R_G1 + 1] * ybuf[0, rs, :]
             + rec[:, R_G2:R_G2 + 1] * ybuf[1, rs, :])
        out = _layer_norm_rows(y, g_ref[...], b_ref[...])
        n = (t * tm + hf * half + lax.broadcasted_iota(I32, out.shape, 0)).astype(F32)
        p = n - jnp.floor((n + 0.5) / tp) * tp
        o_ref[rs, :] = jnp.where((p >= FRONT) & (p < FRONT + t_real), out, 0.0)


def _moe_combine(hf, ys, idx, rec, pstart, ln_g, ln_b, tp, t_real):
    N, D = hf.shape
    nt = N // MOE_BLOCK
    g = ln_g.reshape(1, D).astype(F32)
    b = ln_b.reshape(1, D).astype(F32)
    anyspec = pl.BlockSpec(memory_space=pl.ANY)
    return pl.pallas_call(
        functools.partial(_combine_kernel, tp=tp, t_real=t_real),
        grid_spec=pltpu.PrefetchScalarGridSpec(
            num_scalar_prefetch=1,
            grid=(nt,),
            in_specs=[anyspec, anyspec,
                      pl.BlockSpec((MOE_BLOCK, D), lambda t, ps: (t, 0)),
                      pl.BlockSpec((MOE_BLOCK, LANES), lambda t, ps: (t, 0)),
                      pl.BlockSpec((1, D), lambda t, ps: (0, 0)), pl.BlockSpec((1, D), lambda t, ps: (0, 0))],
            out_specs=pl.BlockSpec((MOE_BLOCK, D), lambda t, ps: (t, 0)),
            scratch_shapes=[pltpu.SMEM((2 * MOE_TOPK * MOE_BLOCK,), I32),
                            pltpu.VMEM((MOE_TOPK, MOE_BLOCK, D), F32),
                            pltpu.SemaphoreType.DMA, pltpu.SemaphoreType.DMA((2,))],
        ),
        out_shape=jax.ShapeDtypeStruct((N, D), F32),
        compiler_params=_cparams("arbitrary"),
        name="moe_combine_ln",
    )(pstart, idx, ys, hf, rec, g, b)


def _moe_layer(h, group_w, group_b, expert_w, expert_b, w_gate, w_up, w_down, ln_g, ln_b, t_real):
    B, TP, D = h.shape
    hf = h.reshape(B * TP, D)
    N = B * TP
    rec, rect, cnt = _moe_route(hf, group_w, group_b, expert_w, expert_b)
    counts = cnt[0, :N_EXPERTS].astype(I32)
    padded = (counts + MOE_BLOCK - 1) // MOE_BLOCK * MOE_BLOCK
    pend = jnp.cumsum(padded)
    pstart = (pend - padded).astype(I32)
    n_blocks = (N * MOE_TOPK) // MOE_BLOCK + N_EXPERTS
    block_start = jnp.arange(n_blocks, dtype=I32) * MOE_BLOCK
    block_expert = jnp.minimum(jnp.sum((pend[None, :] <= block_start[:, None]).astype(I32), axis=1), N_EXPERTS - 1)
    nb_used = (pend[-1:] // MOE_BLOCK).astype(I32)
    idx = rect[:, :2 * MOE_TOPK, :].astype(I32).reshape(N // MOE_BLOCK, 2 * MOE_TOPK * MOE_BLOCK)
    xs = _moe_dispatch(hf, idx, pstart, n_blocks * MOE_BLOCK)
    ys = _moe_ffn(xs, block_expert, nb_used, w_gate, w_up, w_down)
    out = _moe_combine(hf, ys, idx, rec, pstart, ln_g, ln_b, TP, t_real)
    return out.reshape(B, TP, D)


def _even_layer(h, w_in, kv_norm, w_uk, w_uv, conv_w, conv_b, rg_wa, rg_ba, rg_wx, rg_bx, rg_lambda, w_out,
                ln_g, ln_b, t_real):
    q, qi, ki, wi, gate, xb, kv = _even_in_proj(h, w_in, kv_norm, w_uk, w_uv)
    attn = _dsa(q, qi, wi, ki, kv, t_real)
    rec = _griffin(xb, gate, conv_w, conv_b, rg_wa, rg_ba, rg_wx, rg_bx, rg_lambda, t_real)
    return _out_ln(h, [attn, rec], [w_out[:A_OUT], w_out[A_OUT:]], ln_g, ln_b, t_real)


def _odd_in_kernel(h_ref, wq_ref, wk_ref, wv_ref, wz_ref, wa_ref, wb_ref, q_ref, k_ref, v_ref, z_ref, a_ref, b_ref):
    a = h_ref[0].astype(BF16)
    for w_ref, o_ref in ((wq_ref, q_ref), (wk_ref, k_ref), (wv_ref, v_ref), (wz_ref, z_ref), (wa_ref, a_ref),
                         (wb_ref, b_ref)):
        o_ref[0] = jnp.dot(a, w_ref[...], preferred_element_type=F32)


def _odd_in_proj(h, w_in):
    B, TP, D = h.shape
    tm = _row_tile(TP) // 2 if _row_tile(TP) % 16 == 0 else _row_tile(TP)
    wb = w_in.astype(BF16)
    o = [0, GDN_QK_WIDTH, 2 * GDN_QK_WIDTH, 2 * GDN_QK_WIDTH + GDN_V_WIDTH, 2 * GDN_QK_WIDTH + 2 * GDN_V_WIDTH]
    o += [o[-1] + GDN_V_HEADS, o[-1] + 2 * GDN_V_HEADS]
    ws = [wb[:, o[i]:o[i + 1]] for i in range(4)]
    ws += [jnp.pad(wb[:, o[i]:o[i + 1]], ((0, 0), (0, LANES - GDN_V_HEADS))) for i in (4, 5)]
    widths = [GDN_QK_WIDTH, GDN_QK_WIDTH, GDN_V_WIDTH, GDN_V_WIDTH, LANES, LANES]
    return pl.pallas_call(
        _odd_in_kernel,
        grid=(B, TP // tm),
        in_specs=[pl.BlockSpec((1, tm, D), lambda b, j: (b, j, 0))] + [_const_spec(w.shape) for w in ws],
        out_specs=[pl.BlockSpec((1, tm, n), lambda b, j: (b, j, 0)) for n in widths],
        out_shape=[jax.ShapeDtypeStruct((B, TP, n), F32) for n in widths],
        compiler_params=_cparams("parallel", "parallel"),
        name="odd_in_proj",
    )(h, *ws)


def _conv_silu(x, win_ref, cw):
    y = _causal_conv(x, win_ref, cw)
    return y * jax.nn.sigmoid(y)


def _gdn_prep_kernel(q_ref, k_ref, v_ref, a_ref, b_ref, cwq_ref, cwk_ref, cwv_ref, alog_ref, dtb_ref,
                     u_ref, w_ref, qg_ref, kg_ref, at_ref, e_ref,
                     tq_ref, tk_ref, tv_ref, qs_ref, ks_ref, vs_ref, *, t_real):
    j = pl.program_id(1)
    tt = q_ref.shape[1]
    hd = GDN_HEAD_DIM
    half = CHUNK

    @pl.when(j == 0)
    def _():
        tq_ref[...] = jnp.zeros_like(tq_ref)
        tk_ref[...] = jnp.zeros_like(tk_ref)
        tv_ref[...] = jnp.zeros_like(tv_ref)

    rows1 = j * tt + lax.broadcasted_iota(I32, (tt, 1), 0)
    valid = _valid_rows(rows1, t_real)
    q = jnp.where(valid, _conv_silu(q_ref[0], tq_ref, cwq_ref[...]), 0.0)
    k = jnp.where(valid, _conv_silu(k_ref[0], tk_ref, cwk_ref[...]), 0.0)
    vs_ref[...] = jnp.where(valid, _conv_silu(v_ref[0], tv_ref, cwv_ref[...]), 0.0)
    for h in range(GDN_K_HEADS):
        sl = slice(h * hd, (h + 1) * hd)
        qh, kh = q[:, sl], k[:, sl]
        qs_ref[:, sl] = qh * lax.rsqrt(jnp.sum(qh * qh, axis=1, keepdims=True) + 1e-6) * (hd ** -0.5)
        ks_ref[:, sl] = kh * lax.rsqrt(jnp.sum(kh * kh, axis=1, keepdims=True) + 1e-6)

    lane = lax.broadcasted_iota(I32, (tt, LANES), 1)
    rowi = lax.broadcasted_iota(I32, (tt, LANES), 0)
    live = valid & (lane < GDN_V_HEADS)
    g = jnp.where(live, -jnp.exp(alog_ref[...]) * _softplus(a_ref[0] + dtb_ref[...]), 0.0)
    beta = jnp.where(live, jax.nn.sigmoid(b_ref[0]), 0.0)
    gc = g
    s = 1
    while s < CHUNK:
        gc = gc + jnp.where((rowi % CHUNK) >= s, pltpu.roll(gc, s, 0), 0.0)
        s *= 2
    gc_next = pltpu.roll(gc, LANES - 1, 1)
    n_ch = tt // CHUNK
    e_rows = []
    for c in range(n_ch):
        e_rows.append(jnp.broadcast_to(jnp.exp(gc[(c + 1) * CHUNK - 1:(c + 1) * CHUNK, :]), (CHUNK, LANES)))
    e_ref[0] = jnp.concatenate(e_rows, axis=0)

    nt = (((1,), (1,)), ((), ()))
    l2 = lax.broadcasted_iota(I32, (CHUNK, 2 * half), 1)
    ii = lax.broadcasted_iota(I32, (CHUNK, 2 * half), 0)
    jj = l2 % half
    left = l2 < half
    r2 = lax.broadcasted_iota(I32, (2 * half, 2 * half), 0)
    c2 = lax.broadcasted_iota(I32, (2 * half, 2 * half), 1)
    eye2 = jnp.where(r2 == c2, 1.0, 0.0)
    bdot = lambda x, y: jnp.dot(x.astype(BF16), y.astype(BF16), preferred_element_type=F32)
    for c in range(n_ch):
        rs = slice(c * CHUNK, (c + 1) * CHUNK)
        gc_c, beta_c = gc[rs], beta[rs]
        gt = jnp.concatenate([gc_c, gc_next[rs]], axis=0).T
        glast = gc_c[CHUNK - 1:CHUNK, :]
        pairs = range(GDN_K_HEADS)
        ms, tinvs, rhss = [], [], []
        for pr in pairs:
            ha, hb = 2 * pr, 2 * pr + 1
            ksl = slice(pr * hd, (pr + 1) * hd)
            osl = slice(ha * hd, (hb + 1) * hd)
            kc = ks_ref[rs, ksl]
            qc = qs_ref[rs, ksl]
            kc16 = kc.astype(BF16)
            k2 = jnp.concatenate([kc16, kc16], axis=0)
            kk2 = lax.dot_general(kc16, k2, nt, preferred_element_type=F32)
            qk2 = lax.dot_general(qc.astype(BF16), k2, nt, preferred_element_type=F32)
            gca, gcb = gc_c[:, ha:ha + 1], gc_c[:, hb:hb + 1]
            ba, bb = beta_c[:, ha:ha + 1], beta_c[:, hb:hb + 1]
            gcol2 = jnp.where(left, gca, gcb)
            bcol2 = jnp.where(left, ba, bb)
            decay2 = jnp.exp(jnp.where(ii >= jj, gcol2 - gt[ha:ha + 1, :], -jnp.inf))
            m2 = jnp.where(ii > jj, bcol2 * kk2 * decay2, 0.0)
            at_ref[0, rs, pr * 2 * half:(pr + 1) * 2 * half] = (qk2 * decay2).astype(BF16)
            m = jnp.concatenate([jnp.where(left, m2, 0.0), jnp.where(left, 0.0, m2)], axis=0)
            ms.append(m)
            tinvs.append(eye2 - m)
            ega, egb = jnp.exp(gca), jnp.exp(gcb)
            va = vs_ref[rs, ha * hd:(ha + 1) * hd]
            vb = vs_ref[rs, hb * hd:(hb + 1) * hd]
            rhss.append(jnp.concatenate([jnp.concatenate([va * ba, kc * (ba * ega)], axis=1),
                                         jnp.concatenate([vb * bb, kc * (bb * egb)], axis=1)],
                                        axis=0).astype(BF16))
            qg_ref[0, rs, osl] = jnp.concatenate([qc * ega, qc * egb], axis=1).astype(BF16)
            kg_ref[0, rs, osl] = jnp.concatenate([kc * jnp.exp(glast[:, ha:ha + 1] - gca),
                                                  kc * jnp.exp(glast[:, hb:hb + 1] - gcb)], axis=1).astype(BF16)
        pws = [bdot(m, m) for m in ms]
        n_fac = CHUNK.bit_length() - 2
        for f in range(n_fac):
            if f + 1 < n_fac:
                both = [bdot(jnp.concatenate([t, p], axis=0), p) for t, p in zip(tinvs, pws)]
                tinvs = [t + r[:2 * half] for t, r in zip(tinvs, both)]
                pws = [r[2 * half:] for r in both]
            else:
                tinvs = [t + bdot(t, p) for t, p in zip(tinvs, pws)]
        sols = [jnp.dot(t.astype(BF16), r, preferred_element_type=F32) for t, r in zip(tinvs, rhss)]
        for pr, sol in zip(pairs, sols):
            osl = slice(2 * pr * hd, (2 * pr + 2) * hd)
            u_ref[0, rs, osl] = jnp.concatenate([sol[:CHUNK, :hd], sol[CHUNK:, :hd]], axis=1)
            w_ref[0, rs, osl] = jnp.concatenate([sol[:CHUNK, hd:], sol[CHUNK:, hd:]], axis=1).astype(BF16)


def _gdn_prep(q, k, v, a, b, conv_w, a_log, dt_bias, t_real):
    B, TP, _ = q.shape
    tt = TIME_TILE
    cw = jnp.pad(conv_w.astype(F32), ((0, SUBLANES - CONV_WIDTH), (0, 0)))
    cwq, cwk, cwv = cw[:, :GDN_QK_WIDTH], cw[:, GDN_QK_WIDTH:2 * GDN_QK_WIDTH], cw[:, 2 * GDN_QK_WIDTH:]
    alog = jnp.pad(a_log.astype(F32), (0, LANES - GDN_V_HEADS)).reshape(1, LANES)
    dtb = jnp.pad(dt_bias.astype(F32), (0, LANES - GDN_V_HEADS)).reshape(1, LANES)
    tspec = lambda n: pl.BlockSpec((1, tt, n), lambda bb, j: (bb, j, 0))
    ws = (cwq, cwk, cwv, alog, dtb)
    VW, QW = GDN_V_WIDTH, GDN_QK_WIDTH
    outs = [(VW, F32), (VW, BF16), (VW, BF16), (VW, BF16), (GDN_K_HEADS * 2 * CHUNK, BF16), (LANES, F32)]
    return pl.pallas_call(
        functools.partial(_gdn_prep_kernel, t_real=t_real),
        grid=(B, TP // tt),
        in_specs=[tspec(QW), tspec(QW), tspec(VW), tspec(LANES), tspec(LANES)] + [_const_spec(w.shape) for w in ws],
        out_specs=[tspec(n) for n, _ in outs],
        out_shape=[jax.ShapeDtypeStruct((B, TP, n), dt) for n, dt in outs],
        scratch_shapes=[pltpu.VMEM((SUBLANES + tt, QW), F32), pltpu.VMEM((SUBLANES + tt, QW), F32),
                        pltpu.VMEM((SUBLANES + tt, VW), F32), pltpu.VMEM((tt, QW), F32), pltpu.VMEM((tt, QW), F32),
                        pltpu.VMEM((tt, VW), F32)],
        compiler_params=_cparams("parallel", "arbitrary"),
        name="gdn_prep",
    )(q, k, v, a, b, *ws)


def _gdn_scan_kernel(u_ref, w_ref, qg_ref, kg_ref, at_ref, e_ref, z_ref, on_ref, o_ref, s_ref):
    j = pl.program_id(1)
    tt = u_ref.shape[1]
    hd = GDN_HEAD_DIM

    @pl.when(j == 0)
    def _():
        s_ref[...] = jnp.zeros_like(s_ref)

    lane = lax.broadcasted_iota(I32, (CHUNK, 2 * CHUNK), 1)
    tn = (((0,), (0,)), ((), ()))
    heads = range(GDN_V_HEADS)
    hsl = [slice(h * hd, (h + 1) * hd) for h in heads]
    for c in range(tt // CHUNK):
        rs = slice(c * CHUNK, (c + 1) * CHUNK)
        e_row = e_ref[0, c * CHUNK:c * CHUNK + 1, :]
        ss = [s_ref[h] for h in heads]
        ws = [jnp.dot(jnp.concatenate([w_ref[0, rs, hsl[h]], qg_ref[0, rs, hsl[h]]], axis=0), ss[h].astype(BF16),
                      preferred_element_type=F32) for h in heads]
        v16 = [(u_ref[0, rs, hsl[h]] - ws[h][:CHUNK]).astype(BF16) for h in heads]
        intra = []
        for pr in range(GDN_K_HEADS):
            at2 = at_ref[0, rs, pr * 2 * CHUNK:(pr + 1) * 2 * CHUNK]
            zero = jnp.zeros_like(at2)
            lhs = jnp.concatenate([jnp.where(lane < CHUNK, at2, zero), jnp.where(lane >= CHUNK, at2, zero)], axis=0)
            rhs = jnp.concatenate([v16[2 * pr], v16[2 * pr + 1]], axis=0)
            intra.append(jnp.dot(lhs, rhs, preferred_element_type=F32))
        for h in heads:
            s_ref[h] = ss[h] * e_row[:, h:h + 1] + lax.dot_general(kg_ref[0, rs, hsl[h]], v16[h], tn,
                                                                   preferred_element_type=F32)
        for h in heads:
            o = ws[h][CHUNK:] + intra[h // 2][(h % 2) * CHUNK:(h % 2 + 1) * CHUNK]
            on = o * lax.rsqrt(jnp.mean(o * o, axis=1, keepdims=True) + 1e-6) * on_ref[...]
            z = z_ref[0, rs, hsl[h]]
            o_ref[0, rs, hsl[h]] = (on * (z * jax.nn.sigmoid(z))).astype(BF16)


def _gdn_scan(u, w, qg, kg, at, e, z, o_norm):
    B, TP, VW = u.shape
    tt = TIME_TILE
    tspec = lambda n: pl.BlockSpec((1, tt, n), lambda bb, j: (bb, j, 0))
    on = o_norm.reshape(1, GDN_HEAD_DIM).astype(F32)
    return pl.pallas_call(
        _gdn_scan_kernel,
        grid=(B, TP // tt),
        in_specs=[tspec(VW), tspec(VW), tspec(VW), tspec(VW), tspec(at.shape[-1]), tspec(LANES), tspec(VW),
                  _const_spec(on.shape)],
        out_specs=tspec(VW),
        out_shape=jax.ShapeDtypeStruct((B, TP, VW), BF16),
        scratch_shapes=[pltpu.VMEM((GDN_V_HEADS, GDN_HEAD_DIM, GDN_HEAD_DIM), F32)],
        compiler_params=_cparams("parallel", "arbitrary"),
        name="gdn_scan",
    )(u, w, qg, kg, at, e, z, on)


def _odd_layer(h, w_in, conv_w, a_log, dt_bias, o_norm, w_out, ln_g, ln_b, t_real):
    q, k, v, z, a, b = _odd_in_proj(h, w_in)
    u, w, qg, kg, at, e = _gdn_prep(q, k, v, a, b, conv_w, a_log, dt_bias, t_real)
    gated = _gdn_scan(u, w, qg, kg, at, e, z, o_norm)
    return _out_ln(h, [gated], [w_out], ln_g, ln_b, t_real)


def kernel(x, meta_tokens, even_w_in, even_kv_norm, even_w_uk, even_w_uv, even_conv_w, even_conv_b, even_rg_wa,
           even_rg_ba, even_rg_wx, even_rg_bx, even_rg_lambda, even_w_out, odd_w_in, odd_conv_w, odd_a_log,
           odd_dt_bias, odd_o_norm, odd_w_out, ln_g, ln_b, moe_group_w, moe_group_b, moe_expert_w, moe_expert_b,
           moe_w_gate, moe_w_up, moe_w_down):
    B, S, D = x.shape
    t_real = N_META + S
    tp = -(-(FRONT + t_real) // TIME_TILE) * TIME_TILE
    meta = jnp.broadcast_to(meta_tokens.astype(x.dtype)[None], (B, N_META, D))
    h = jnp.concatenate([jnp.zeros((B, FRONT, D), x.dtype), meta, x,
                         jnp.zeros((B, tp - FRONT - t_real, D), x.dtype)], axis=1)
    for layer in range(DEPTH):
        i = layer // 2
        if layer % 2 == 0:
            h = _even_layer(h, even_w_in[i], even_kv_norm[i], even_w_uk[i], even_w_uv[i], even_conv_w[i],
                            even_conv_b[i], even_rg_wa[i], even_rg_ba[i], even_rg_wx[i], even_rg_bx[i],
                            even_rg_lambda[i], even_w_out[i], ln_g[layer, 0], ln_b[layer, 0], t_real)
        else:
            h = _odd_layer(h, odd_w_in[i], odd_conv_w[i], odd_a_log[i], odd_dt_bias[i], odd_o_norm[i],
                           odd_w_out[i], ln_g[layer, 0], ln_b[layer, 0], t_real)
        h = _moe_layer(h, moe_group_w[layer], moe_group_b[layer], moe_expert_w[layer], moe_expert_b[layer],
                       moe_w_gate[layer], moe_w_up[layer], moe_w_down[layer], ln_g[layer, 1], ln_b[layer, 1], t_real)
    return h[:, FRONT:FRONT + t_real][:, N_META:]
```

```python
import functools
import math

import jax
import jax.numpy as jnp
from jax import lax
from jax.experimental import pallas as pl
from jax.experimental.pallas import tpu as pltpu

F32 = jnp.float32
BF16 = jnp.bfloat16
I32 = jnp.int32

D_MODEL = 1024
DEPTH = 2
N_META = 16
DN_ALPHA = (2 * DEPTH) ** 0.25

A_HEADS = 4
A_HEAD_DIM = 128
A_OUT = A_HEADS * A_HEAD_DIM
A_KV_RANK = 256
IDX_HEADS = 8
IDX_DIM = 64
IDX_TOPK_CAP = 256

RG_WIDTH = 512
RG_BLOCKS = 8
RG_C = 8.0
CONV_WIDTH = 4

GDN_K_HEADS = 8
GDN_V_HEADS = 16
GDN_HEAD_DIM = 128
GDN_QK_WIDTH = GDN_K_HEADS * GDN_HEAD_DIM
GDN_V_WIDTH = GDN_V_HEADS * GDN_HEAD_DIM
CHUNK = 64

MOE_GROUPS = 4
MOE_PER_GROUP = 8
N_EXPERTS = MOE_GROUPS * MOE_PER_GROUP
MOE_TOPK = 2
EXPERT_FF = 512
MOE_BLOCK = 256
FFN_BLOCK = 512

LANES = 128
SUBLANES = 8
TIME_TILE = 128
FRONT = (-N_META) % CHUNK
NEG_BIG = -1e30
INT_MIN = -2 ** 31
VMEM_LIMIT = 56 * 1024 * 1024


def _cparams(*sem, flags=None):
    return pltpu.CompilerParams(dimension_semantics=sem, vmem_limit_bytes=VMEM_LIMIT, flags=flags)


def _const_spec(shape):
    nd = len(shape)
    return pl.BlockSpec(shape, lambda *_: (0,) * nd)


def _row_tile(tp):
    best = SUBLANES
    for t in range(SUBLANES, 641, SUBLANES):
        if tp % t == 0:
            best = t
    return best


def _valid_rows(p, t_real):
    return (p >= FRONT) & (p < FRONT + t_real)


def _layer_norm_rows(y, g, b):
    mu = jnp.mean(y, axis=-1, keepdims=True)
    yc = y - mu
    var = jnp.mean(yc * yc, axis=-1, keepdims=True)
    return yc * lax.rsqrt(var + 1e-5) * g + b


def _softplus(z):
    return jnp.maximum(z, 0.0) + jnp.log1p(jnp.exp(-jnp.abs(z)))


def _even_in_kernel(h_ref, wq_ref, wc_ref, wqi_ref, wki_ref, wwi_ref, wg_ref, wx_ref, kvn_ref, wukv_ref,
                    q_ref, qi_ref, ki_ref, wi_ref, gate_ref, xb_ref, kv_ref):
    a = h_ref[0].astype(BF16)
    dot = functools.partial(jnp.dot, preferred_element_type=F32)
    q_ref[0] = dot(a, wq_ref[...]).astype(BF16)
    qi_ref[0] = dot(a, wqi_ref[...]).astype(BF16)
    ki_ref[0] = dot(a, wki_ref[...]).astype(BF16)
    wi_ref[0] = dot(a, wwi_ref[...]) * (IDX_HEADS ** -0.5) * (IDX_DIM ** -0.5)
    gate_ref[0] = dot(a, wg_ref[...])
    xb_ref[0] = dot(a, wx_ref[...])
    c = dot(a, wc_ref[...])
    latent = c * lax.rsqrt(jnp.mean(c * c, axis=-1, keepdims=True) + 1e-6) * kvn_ref[...]
    kv_ref[0] = dot(latent.astype(BF16), wukv_ref[...]).astype(BF16)


def _even_in_proj(h, w_in, kv_norm, w_uk, w_uv):
    B, TP, D = h.shape
    tm = _row_tile(TP)
    o = [0]
    for p in (A_OUT, A_KV_RANK, IDX_HEADS * IDX_DIM, IDX_DIM, IDX_HEADS, RG_WIDTH, RG_WIDTH):
        o.append(o[-1] + p)
    wb = w_in.astype(BF16)
    wq, wc = wb[:, o[0]:o[1]], wb[:, o[1]:o[2]]
    wqi = wb[:, o[2]:o[3]].reshape(D, IDX_HEADS, IDX_DIM)
    wqi = jnp.pad(wqi, ((0, 0), (0, 0), (0, LANES - IDX_DIM))).reshape(D, IDX_HEADS * LANES)
    wki = jnp.pad(wb[:, o[3]:o[4]], ((0, 0), (0, LANES - IDX_DIM)))
    wwi = jnp.pad(wb[:, o[4]:o[5]], ((0, 0), (0, LANES - IDX_HEADS)))
    wg, wx = wb[:, o[5]:o[6]], wb[:, o[6]:o[7]]
    wukv = jnp.concatenate([w_uk, w_uv], axis=1).astype(BF16)
    kvn = kv_norm.reshape(1, A_KV_RANK).astype(F32)
    ws = (wq, wc, wqi, wki, wwi, wg, wx, kvn, wukv)
    outs = [(A_OUT, BF16), (IDX_HEADS * LANES, BF16), (LANES, BF16), (LANES, F32), (RG_WIDTH, F32),
            (RG_WIDTH, F32), (2 * A_HEAD_DIM, BF16)]
    return pl.pallas_call(
        _even_in_kernel,
        grid=(B, TP // tm),
        in_specs=[pl.BlockSpec((1, tm, D), lambda b, j: (b, j, 0))] + [_const_spec(w.shape) for w in ws],
        out_specs=[pl.BlockSpec((1, tm, n), lambda b, j: (b, j, 0)) for n, _ in outs],
        out_shape=[jax.ShapeDtypeStruct((B, TP, n), dt) for n, dt in outs],
        compiler_params=_cparams("parallel", "parallel"),
        name="even_in_proj",
    )(h, *ws)


def _dsa_body(nk, q_ref, qi_ref, wi_ref, ki_ref, kv_ref, o_ref, key_ref, bias_ref, vt_ref, n_sel, t_real):
    i = pl.program_id(1)
    tq = q_ref.shape[1]
    qi = qi_ref[0]
    ki = ki_ref[0, :nk, :]
    wit = wi_ref[0].T
    nt = (((1,), (1,)), ((), ()))
    score = jnp.zeros((nk, tq), F32)
    for h in range(IDX_HEADS):
        s = lax.dot_general(ki, qi[:, h * LANES:(h + 1) * LANES], nt, preferred_element_type=F32)
        score = score + wit[h:h + 1, :] * jnp.maximum(s, 0.0)

    kpos = lax.broadcasted_iota(I32, (nk, tq), 0)
    qpos = i * tq + lax.broadcasted_iota(I32, (nk, tq), 1)
    bits = lax.bitcast_convert_type(score, I32)
    key = jnp.where(bits < 0, bits ^ jnp.int32(0x7FFFFFFF), bits)
    key_ref[:nk, :] = jnp.where(kpos <= qpos, key, jnp.int32(INT_MIN))
    key_ref[0:FRONT, :] = jnp.full((FRONT, tq), INT_MIN, I32)
    if nk > FRONT + t_real:
        key_ref[FRONT + t_real:nk, :] = jnp.full((nk - FRONT - t_real, tq), INT_MIN, I32)

    def over_keys(x, op):
        pair = jnp.add if op is jnp.sum else jnp.maximum
        group = 8
        x = x.reshape(nk // (group * SUBLANES), group, SUBLANES, x.shape[1])
        while x.shape[1] > 1:
            half = x.shape[1] // 2
            x = pair(x[:, :half], x[:, half:])
        return op(op(x[:, 0], axis=0), axis=0, keepdims=True)

    def count(mask):
        return over_keys(jnp.where(mask, 1, 0).astype(I32), jnp.sum)

    def any_query(mask):
        return jnp.max(jnp.where(mask, 1, 0))

    n_vis = count(key_ref[:nk, :] > jnp.int32(INT_MIN))

    bits_per_check = 8

    def bit_cond(c):
        return (c[0] < 32) & (c[3] > 0)

    def bit_group(c):
        it0, cand, cnt_c, _ = c

        def bit_step(b, cc):
            cand, cnt_c = cc
            trial = cand + jnp.left_shift(jnp.int32(1), 31 - (it0 + b))
            cnt = count(key_ref[:nk, :] >= trial)
            take = cnt >= n_sel
            return jnp.where(take, trial, cand), jnp.where(take, cnt, cnt_c)

        cand, cnt_c = lax.fori_loop(0, bits_per_check, bit_step, (cand, cnt_c))
        return it0 + bits_per_check, cand, cnt_c, any_query((cnt_c != n_sel) & (n_vis > n_sel))

    start = (jnp.int32(0), jnp.full((1, tq), INT_MIN, I32), jnp.full((1, tq), nk, I32),
             any_query(n_vis > n_sel))
    _, thr, cnt_ge, _ = lax.while_loop(bit_cond, bit_group, start)
    keys = key_ref[:nk, :]
    bias_ref[:nk, :] = jnp.where(keys >= jnp.maximum(thr, jnp.int32(INT_MIN + 1)), 0.0, NEG_BIG)
    tie_queries = (cnt_ge > n_sel) & (thr > jnp.int32(INT_MIN))

    @pl.when(any_query(tie_queries) > 0)
    def _():
        keys = key_ref[:nk, :]
        eq = keys == thr
        need = n_sel - count(keys > thr)
        nbits = max(1, (nk - 1).bit_length())

        def idx_step(it, cand):
            trial = cand + jnp.left_shift(jnp.int32(1), nbits - 1 - it)
            return jnp.where(count(eq & (kpos < trial)) < need, trial, cand)

        last = lax.fori_loop(0, nbits, idx_step, jnp.zeros((1, tq), I32))
        sel = (keys > thr) | (eq & (kpos <= last))
        bias_ref[:nk, :] = jnp.where(sel & (keys > jnp.int32(INT_MIN)), 0.0, NEG_BIG)

    bias = bias_ref[:nk, :]
    q = q_ref[0]
    k = kv_ref[0, :nk, :A_HEAD_DIM]
    qs = jnp.concatenate([q[:, h * A_HEAD_DIM:(h + 1) * A_HEAD_DIM] for h in range(A_HEADS)], axis=0)
    lg = lax.dot_general(k, qs, nt, preferred_element_type=F32) * (A_HEAD_DIM ** -0.5)
    lg = lg + jnp.concatenate([bias] * A_HEADS, axis=1)
    m = over_keys(lg, jnp.max)
    p = jnp.exp(lg - m)
    l = over_keys(p, jnp.sum)
    ot = jnp.dot(vt_ref[:, :nk], p.astype(BF16), preferred_element_type=F32) / l
    for h in range(A_HEADS):
        o_ref[0, :, h * A_HEAD_DIM:(h + 1) * A_HEAD_DIM] = ot[:, h * tq:(h + 1) * tq].T.astype(BF16)


DSA_WIDTH_STEP = 3


def _dsa_kernel(q_ref, qi_ref, wi_ref, ki_ref, kv_ref, o_ref, key_ref, bias_ref, vt_ref, *, n_sel, t_real):
    i = pl.program_id(1)
    tq = q_ref.shape[1]
    tp = ki_ref.shape[1]
    n_tiles = tp // tq

    @pl.when(i == 0)
    def _():
        vt_ref[...] = kv_ref[0, :, A_HEAD_DIM:].astype(F32).T.astype(BF16)

    lo = 0
    while lo < n_tiles:
        hi = min(lo + DSA_WIDTH_STEP, n_tiles)

        @pl.when((i >= lo) & (i < hi))
        def _(hi=hi):
            _dsa_body(hi * tq, q_ref, qi_ref, wi_ref, ki_ref, kv_ref, o_ref, key_ref, bias_ref, vt_ref,
                      n_sel, t_real)

        lo = hi


def _dsa(q, qi, wi, ki, kv, t_real):
    B, TP, _ = q.shape
    tq = TIME_TILE
    n_sel = min(IDX_TOPK_CAP, t_real // 4)
    kern = functools.partial(_dsa_kernel, n_sel=n_sel, t_real=t_real)
    qspec = lambda n: pl.BlockSpec((1, tq, n), lambda b, i: (b, i, 0))
    kspec = lambda n: pl.BlockSpec((1, TP, n), lambda b, i: (b, 0, 0))
    return pl.pallas_call(
        kern,
        grid=(B, TP // tq),
        in_specs=[qspec(A_OUT), qspec(IDX_HEADS * LANES), qspec(LANES), kspec(LANES), kspec(2 * A_HEAD_DIM)],
        out_specs=qspec(A_OUT),
        out_shape=jax.ShapeDtypeStruct((B, TP, A_OUT), BF16),
        scratch_shapes=[pltpu.VMEM((TP, tq), I32), pltpu.VMEM((TP, tq), F32), pltpu.VMEM((A_HEAD_DIM, TP), BF16)],
        compiler_params=_cparams("parallel", "arbitrary"),
        name="dsa_attention",
    )(q, qi, wi, ki, kv)


def _shift_rows(x, s, fill):
    rows = lax.broadcasted_iota(I32, x.shape, 0)
    return jnp.where(rows >= s, pltpu.roll(x, s, 0), fill)


def _causal_conv(x, win_ref, cw):
    tt = x.shape[0]
    win_ref[0:SUBLANES, :] = win_ref[tt:tt + SUBLANES, :]
    win_ref[SUBLANES:, :] = x
    y = cw[CONV_WIDTH - 1:CONV_WIDTH, :] * x
    for t in range(CONV_WIDTH - 1):
        off = SUBLANES - (CONV_WIDTH - 1) + t
        y = y + cw[t:t + 1, :] * win_ref[off:off + tt, :]
    return y


def _griffin_kernel(xb_ref, gate_ref, cw_ref, cb_ref, wg_ref, bg_ref, lam_ref, o_ref, tail_ref, car_ref, *, t_real):
    j = pl.program_id(1)
    tt = xb_ref.shape[1]

    @pl.when(j == 0)
    def _():
        tail_ref[...] = jnp.zeros_like(tail_ref)
        car_ref[...] = jnp.zeros_like(car_ref)

    xr = _causal_conv(xb_ref[0], tail_ref, cw_ref[...]) + cb_ref[...]
    g = jnp.dot(xr.astype(BF16), wg_ref[...], preferred_element_type=F32) + bg_ref[...]
    r = jax.nn.sigmoid(g[:, :RG_WIDTH])
    ig = jax.nn.sigmoid(g[:, RG_WIDTH:])
    log_a = -RG_C * r * _softplus(-lam_ref[...])
    a = jnp.exp(log_a)
    th = jnp.tanh(log_a)
    u = jnp.sqrt(-2.0 * th / (1.0 - th)) * (ig * xr)
    p = j * tt + lax.broadcasted_iota(I32, xr.shape, 0)
    u = jnp.where(_valid_rows(p, t_real), u, 0.0)
    s = 1
    while s < tt:
        u = u + a * _shift_rows(u, s, 0.0)
        a = a * _shift_rows(a, s, 1.0)
        s *= 2
    hcur = u + a * car_ref[0:1, :]
    car_ref[...] = jnp.broadcast_to(hcur[tt - 1:tt, :], car_ref.shape)
    o_ref[0] = hcur * jax.nn.gelu(gate_ref[0])


def _block_diag(w):
    nb, bs, _ = w.shape
    eye = jnp.eye(nb, dtype=w.dtype)
    return (w[:, :, None, :] * eye[:, None, :, None]).reshape(nb * bs, nb * bs)


def _griffin(xb, gate, conv_w, conv_b, rg_wa, rg_ba, rg_wx, rg_bx, rg_lambda, t_real):
    B, TP, R = xb.shape
    tt = TIME_TILE
    cw = jnp.pad(conv_w.astype(F32), ((0, SUBLANES - CONV_WIDTH), (0, 0)))
    cb = conv_b.reshape(1, R).astype(F32)
    wg = jnp.concatenate([_block_diag(rg_wa), _block_diag(rg_wx)], axis=1).astype(BF16)
    bg = jnp.concatenate([rg_ba, rg_bx]).reshape(1, 2 * R).astype(F32)
    lam = rg_lambda.reshape(1, R).astype(F32)
    tspec = pl.BlockSpec((1, tt, R), lambda b, j: (b, j, 0))
    ws = (cw, cb, wg, bg, lam)
    return pl.pallas_call(
        functools.partial(_griffin_kernel, t_real=t_real),
        grid=(B, TP // tt),
        in_specs=[tspec, tspec] + [_const_spec(w.shape) for w in ws],
        out_specs=tspec,
        out_shape=jax.ShapeDtypeStruct((B, TP, R), F32),
        scratch_shapes=[pltpu.VMEM((SUBLANES + tt, R), F32), pltpu.VMEM((SUBLANES, R), F32)],
        compiler_params=_cparams("parallel", "arbitrary"),
        name="griffin_rglru",
    )(xb, gate, *ws)


def _out_ln_kernel(*refs, n_in, t_real):
    h_ref = refs[0]
    a_refs = refs[1:1 + n_in]
    w_refs = refs[1 + n_in:1 + 2 * n_in]
    g_ref, b_ref, o_ref = refs[1 + 2 * n_in:]
    j = pl.program_id(1)
    tm = h_ref.shape[1]
    y = DN_ALPHA * h_ref[0]
    for a_ref, w_ref in zip(a_refs, w_refs):
        y = y + jnp.dot(a_ref[0].astype(BF16), w_ref[...], preferred_element_type=F32)
    out = _layer_norm_rows(y, g_ref[...], b_ref[...])
    p = j * tm + lax.broadcasted_iota(I32, out.shape, 0)
    o_ref[0] = jnp.where(_valid_rows(p, t_real), out, 0.0)


def _out_ln(h, acts, ws, ln_g, ln_b, t_real):
    B, TP, D = h.shape
    tm = _row_tile(TP)
    ws = [w.astype(BF16) for w in ws]
    g = ln_g.reshape(1, D).astype(F32)
    b = ln_b.reshape(1, D).astype(F32)
    tspec = lambda n: pl.BlockSpec((1, tm, n), lambda bb, j: (bb, j, 0))
    return pl.pallas_call(
        functools.partial(_out_ln_kernel, n_in=len(acts), t_real=t_real),
        grid=(B, TP // tm),
        in_specs=[tspec(D)] + [tspec(a.shape[-1]) for a in acts] + [_const_spec(w.shape) for w in ws]
        + [_const_spec(g.shape), _const_spec(b.shape)],
        out_specs=tspec(D),
        out_shape=jax.ShapeDtypeStruct((B, TP, D), F32),
        compiler_params=_cparams("parallel", "parallel"),
        name="out_proj_ln",
    )(h, *acts, *ws, g, b)


E_LANE0, G_LANE0 = 0, N_EXPERTS
R_E1, R_E2, R_RANK1, R_RANK2, R_G1, R_G2 = range(6)


def _router_kernel(h_ref, w_ref, b_ref, rec_ref, rect_ref, cnt_ref, car_ref):
    t = pl.program_id(0)

    @pl.when(t == 0)
    def _():
        car_ref[...] = jnp.zeros_like(car_ref)

    tm = h_ref.shape[0]
    h = h_ref[...]
    h_hi = h.astype(BF16)
    h_lo = (h - h_hi.astype(F32)).astype(BF16)
    both = jnp.dot(h_hi, w_ref[...], preferred_element_type=F32)
    lg = (both[:, :LANES] + both[:, LANES:]
          + jnp.dot(h_lo, w_ref[:, :LANES], preferred_element_type=F32) + b_ref[...])
    lane = lax.broadcasted_iota(I32, lg.shape, 1)
    neg_inf = -jnp.inf

    def first_argmax(x):
        m = jnp.max(x, axis=1, keepdims=True)
        return m, jnp.min(jnp.where(x == m, lane, jnp.int32(2 ** 30)), axis=1, keepdims=True)

    glog = jnp.where((lane >= G_LANE0) & (lane < G_LANE0 + MOE_GROUPS), lg, neg_inf)
    gmax, glane = first_argmax(glog)
    grp = glane - G_LANE0
    grp_gate = 1.0 / jnp.sum(jnp.exp(glog - gmax), axis=1, keepdims=True)
    in_grp = (lane >= grp * MOE_PER_GROUP) & (lane < (grp + 1) * MOE_PER_GROUP)
    elog = jnp.where(in_grp, lg, neg_inf)
    v1, e1 = first_argmax(elog)
    v2, e2 = first_argmax(jnp.where(lane == e1, neg_inf, elog))
    ex = jnp.exp(v2 - v1)
    g1 = grp_gate / (1.0 + ex)
    g2 = grp_gate * ex / (1.0 + ex)

    oh1 = jnp.where(lane == e1, 1.0, 0.0)
    oh2 = jnp.where(lane == e2, 1.0, 0.0)
    rows = lax.broadcasted_iota(I32, (tm, tm), 0)
    cols = lax.broadcasted_iota(I32, (tm, tm), 1)
    ltri = jnp.where(cols < rows, 1.0, 0.0).astype(BF16)
    p1 = jnp.dot(ltri, oh1.astype(BF16), preferred_element_type=F32)
    p2 = jnp.dot(ltri, oh2.astype(BF16), preferred_element_type=F32)
    car = car_ref[0:1, :]
    c1 = jnp.sum(oh1, axis=0, keepdims=True)
    c2 = jnp.sum(oh2, axis=0, keepdims=True)
    rank1 = jnp.sum(oh1 * (car + p1), axis=1, keepdims=True)
    rank2 = jnp.sum(oh2 * (car + c1 + p2), axis=1, keepdims=True)
    car = car + c1 + c2
    car_ref[...] = jnp.broadcast_to(car, car_ref.shape)
    cnt_ref[...] = jnp.broadcast_to(car, cnt_ref.shape)

    rec = jnp.zeros(lg.shape, F32)
    for ln, val in ((R_E1, e1.astype(F32)), (R_E2, e2.astype(F32)), (R_RANK1, rank1), (R_RANK2, rank2),
                    (R_G1, g1), (R_G2, g2)):
        rec = jnp.where(lane == ln, val, rec)
    rec_ref[...] = rec
    rect_ref[0] = rec.T[:SUBLANES, :]


def _moe_route(hf, group_w, group_b, expert_w, expert_b):
    N, D = hf.shape
    tm = MOE_BLOCK
    nt = N // tm
    w = jnp.zeros((D, LANES), F32)
    w = w.at[:, E_LANE0:E_LANE0 + N_EXPERTS].set(expert_w).at[:, G_LANE0:G_LANE0 + MOE_GROUPS].set(group_w)
    b = jnp.zeros((1, LANES), F32)
    b = b.at[0, E_LANE0:E_LANE0 + N_EXPERTS].set(expert_b).at[0, G_LANE0:G_LANE0 + MOE_GROUPS].set(group_b)
    w_hi = w.astype(BF16)
    w = jnp.concatenate([w_hi, (w - w_hi.astype(F32)).astype(BF16)], axis=1)
    return pl.pallas_call(
        _router_kernel,
        grid=(nt,),
        in_specs=[pl.BlockSpec((tm, D), lambda t: (t, 0)), _const_spec(w.shape), _const_spec(b.shape)],
        out_specs=[pl.BlockSpec((tm, LANES), lambda t: (t, 0)),
                   pl.BlockSpec((1, SUBLANES, tm), lambda t: (t, 0, 0)),
                   _const_spec((SUBLANES, LANES))],
        out_shape=[jax.ShapeDtypeStruct((N, LANES), F32),
                   jax.ShapeDtypeStruct((nt, SUBLANES, tm), F32),
                   jax.ShapeDtypeStruct((SUBLANES, LANES), F32)],
        scratch_shapes=[pltpu.VMEM((SUBLANES, LANES), F32)],
        compiler_params=_cparams("arbitrary"),
        name="moe_router",
    )(hf, w, b)


def _row_copy(src, si, dst, di, sem):
    return pltpu.make_async_copy(src.at[pl.ds(si, 1)], dst.at[pl.ds(di, 1)], sem)


def _route_copy(idx_hbm, idx_smem, sems, t):
    return pltpu.make_async_copy(idx_hbm.at[t], idx_smem.at[t % 2], sems.at[t % 2])


def _slot(pstart_ref, idx_smem, t, k, r):
    b = t % 2
    return pstart_ref[idx_smem[b, k * MOE_BLOCK + r]] + idx_smem[b, (MOE_TOPK + k) * MOE_BLOCK + r]


N_HBUF = 3


def _dispatch_kernel(pstart_ref, idx_hbm, h_hbm, xs_in, xs_hbm, idx_smem, hbuf, isems, hsems, ssems):
    del xs_in
    t = pl.program_id(0)
    nt = pl.num_programs(0)
    tm = MOE_BLOCK

    def tile_copy(u):
        return pltpu.make_async_copy(h_hbm.at[pl.ds(u * tm, tm)], hbuf.at[u % N_HBUF], hsems.at[u % N_HBUF])

    def wait_scatters(u):
        for _ in range(tm * MOE_TOPK):
            _row_copy(hbuf.at[0], 0, xs_hbm, 0, ssems.at[u % 2]).wait()

    @pl.when(t == 0)
    def _():
        _route_copy(idx_hbm, idx_smem, isems, t).start()
        tile_copy(t).start()

    _route_copy(idx_hbm, idx_smem, isems, t).wait()
    tile_copy(t).wait()

    @pl.when(t + 1 < nt)
    def _():
        _route_copy(idx_hbm, idx_smem, isems, t + 1).start()
        tile_copy(t + 1).start()

    src = hbuf.at[t % N_HBUF]
    for r in range(tm):
        for k in range(MOE_TOPK):
            _row_copy(src, r, xs_hbm, _slot(pstart_ref, idx_smem, t, k, r), ssems.at[t % 2]).start()

    @pl.when(t > 0)
    def _():
        wait_scatters(t - 1)

    @pl.when(t == nt - 1)
    def _():
        wait_scatters(t)


def _moe_dispatch(hf, idx, pstart, n_slots):
    N, D = hf.shape
    nt = N // MOE_BLOCK
    xs0 = jnp.zeros((n_slots, D), F32)
    return pl.pallas_call(
        _dispatch_kernel,
        grid_spec=pltpu.PrefetchScalarGridSpec(
            num_scalar_prefetch=1,
            grid=(nt,),
            in_specs=[pl.BlockSpec(memory_space=pl.ANY)] * 3,
            out_specs=pl.BlockSpec(memory_space=pl.ANY),
            scratch_shapes=[pltpu.SMEM((2, 2 * MOE_TOPK * MOE_BLOCK), I32),
                            pltpu.VMEM((N_HBUF, MOE_BLOCK, D), F32),
                            pltpu.SemaphoreType.DMA((2,)), pltpu.SemaphoreType.DMA((N_HBUF,)),
                            pltpu.SemaphoreType.DMA((2,))],
        ),
        out_shape=jax.ShapeDtypeStruct((n_slots, D), F32),
        input_output_aliases={3: 0},
        compiler_params=_cparams("arbitrary"),
        name="moe_dispatch",
    )(pstart, idx, hf, xs0)


def _ffn_kernel(be_ref, nb_ref, x_ref, wg_ref, wu_ref, wd_ref, o_ref):
    del be_ref

    @pl.when(pl.program_id(0) < nb_ref[0])
    def _():
        x = x_ref[...].astype(BF16)
        hg = jnp.dot(x, wg_ref[...], preferred_element_type=F32)
        hu = jnp.dot(x, wu_ref[...], preferred_element_type=F32)
        hid = (hg * jax.nn.sigmoid(hg)) * hu
        o_ref[...] = jnp.dot(hid.astype(BF16), wd_ref[...], preferred_element_type=F32)

    @pl.when(pl.program_id(0) >= nb_ref[0])
    def _():
        o_ref[...] = jnp.zeros_like(o_ref)


def _moe_ffn(xs, block_expert, nb_used, w_gate, w_up, w_down):
    P, D = xs.shape
    nb = P // FFN_BLOCK
    wspec = lambda s: pl.BlockSpec((None,) + s, lambda b, be, nbu: (be[b], 0, 0))
    return pl.pallas_call(
        _ffn_kernel,
        grid_spec=pltpu.PrefetchScalarGridSpec(
            num_scalar_prefetch=2,
            grid=(nb,),
            in_specs=[pl.BlockSpec((FFN_BLOCK, D), lambda b, be, nbu: (b, 0)),
                      wspec((D, EXPERT_FF)), wspec((D, EXPERT_FF)), wspec((EXPERT_FF, D))],
            out_specs=pl.BlockSpec((FFN_BLOCK, D), lambda b, be, nbu: (b, 0)),
        ),
        out_shape=jax.ShapeDtypeStruct((P, D), F32),
        compiler_params=_cparams("arbitrary"),
        name="moe_expert_ffn",
    )(block_expert, nb_used, xs, w_gate.astype(BF16), w_up.astype(BF16), w_down.astype(BF16))


def _combine_kernel(pstart_ref, idx_hbm, ys_hbm, h_ref, rec_ref, g_ref, b_ref, o_ref, idx_smem, ybuf, isems, gsems,
                    *, tp, t_real):
    t = pl.program_id(0)
    nt = pl.num_programs(0)
    tm = MOE_BLOCK

    def issue_gathers(u):
        dst = ybuf.at[u % 2]
        for r in range(tm):
            for k in range(MOE_TOPK):
                _row_copy(ys_hbm, _slot(pstart_ref, idx_smem, u, k, r), dst.at[k], r, gsems.at[u % 2]).start()

    @pl.when(t == 0)
    def _():
        _route_copy(idx_hbm, idx_smem, isems, t).start()
        _route_copy(idx_hbm, idx_smem, isems, t).wait()
        issue_gathers(t)

        @pl.when(nt > 1)
        def _():
            _route_copy(idx_hbm, idx_smem, isems, t + 1).start()

    @pl.when(t + 1 < nt)
    def _():
        _route_copy(idx_hbm, idx_smem, isems, t + 1).wait()
        issue_gathers(t + 1)

    @pl.when(t + 2 < nt)
    def _():
        _route_copy(idx_hbm, idx_smem, isems, t + 2).start()

    for _ in range(tm * MOE_TOPK):
        _row_copy(ys_hbm, 0, ybuf.at[0, 0], 0, gsems.at[t % 2]).wait()
    b = t % 2
    rec = rec_ref[...]
    y = DN_ALPHA * h_ref[...] + rec[:, R_G1:R_G1 + 1] * ybuf[b, 0] + rec[:, R_G2:R_G2 + 1] * ybuf[b, 1]
    out = _layer_norm_rows(y, g_ref[...], b_ref[...])
    n = (t * tm + lax.broadcasted_iota(I32, out.shape, 0)).astype(F32)
    p = n - jnp.floor((n + 0.5) / tp) * tp
    o_ref[...] = jnp.where((p >= FRONT) & (p < FRONT + t_real), out, 0.0)


def _moe_combine(hf, ys, idx, rec, pstart, ln_g, ln_b, tp, t_real):
    N, D = hf.shape
    nt = N // MOE_BLOCK
    g = ln_g.reshape(1, D).astype(F32)
    b = ln_b.reshape(1, D).astype(F32)
    anyspec = pl.BlockSpec(memory_space=pl.ANY)
    return pl.pallas_call(
        functools.partial(_combine_kernel, tp=tp, t_real=t_real),
        grid_spec=pltpu.PrefetchScalarGridSpec(
            num_scalar_prefetch=1,
            grid=(nt,),
            in_specs=[anyspec, anyspec,
                      pl.BlockSpec((MOE_BLOCK, D), lambda t, ps: (t, 0)),
                      pl.BlockSpec((MOE_BLOCK, LANES), lambda t, ps: (t, 0)),
                      pl.BlockSpec((1, D), lambda t, ps: (0, 0)), pl.BlockSpec((1, D), lambda t, ps: (0, 0))],
            out_specs=pl.BlockSpec((MOE_BLOCK, D), lambda t, ps: (t, 0)),
            scratch_shapes=[pltpu.SMEM((2, 2 * MOE_TOPK * MOE_BLOCK), I32),
                            pltpu.VMEM((2, MOE_TOPK, MOE_BLOCK, D), F32),
                            pltpu.SemaphoreType.DMA((2,)), pltpu.SemaphoreType.DMA((2,))],
        ),
        out_shape=jax.ShapeDtypeStruct((N, D), F32),
        compiler_params=_cparams("arbitrary"),
        name="moe_combine_ln",
    )(pstart, idx, ys, hf, rec, g, b)


def _moe_layer(h, group_w, group_b, expert_w, expert_b, w_gate, w_up, w_down, ln_g, ln_b, t_real):
    B, TP, D = h.shape
    hf = h.reshape(B * TP, D)
    N = B * TP
    rec, rect, cnt = _moe_route(hf, group_w, group_b, expert_w, expert_b)
    counts = cnt[0, :N_EXPERTS].astype(I32)
    padded = (counts + FFN_BLOCK - 1) // FFN_BLOCK * FFN_BLOCK
    pend = jnp.cumsum(padded)
    pstart = (pend - padded).astype(I32)
    n_blocks = -(-(N * MOE_TOPK) // FFN_BLOCK) + N_EXPERTS
    block_start = jnp.arange(n_blocks, dtype=I32) * FFN_BLOCK
    block_expert = jnp.minimum(jnp.sum((pend[None, :] <= block_start[:, None]).astype(I32), axis=1), N_EXPERTS - 1)
    nb_used = (pend[-1:] // FFN_BLOCK).astype(I32)
    idx = rect[:, :2 * MOE_TOPK, :].astype(I32).reshape(N // MOE_BLOCK, 2 * MOE_TOPK * MOE_BLOCK)
    xs = _moe_dispatch(hf, idx, pstart, n_blocks * FFN_BLOCK)
    ys = _moe_ffn(xs, block_expert, nb_used, w_gate, w_up, w_down)
    out = _moe_combine(hf, ys, idx, rec, pstart, ln_g, ln_b, TP, t_real)
    return out.reshape(B, TP, D)


def _even_layer(h, w_in, kv_norm, w_uk, w_uv, conv_w, conv_b, rg_wa, rg_ba, rg_wx, rg_bx, rg_lambda, w_out,
                ln_g, ln_b, t_real):
    q, qi, ki, wi, gate, xb, kv = _even_in_proj(h, w_in, kv_norm, w_uk, w_uv)
    attn = _dsa(q, qi, wi, ki, kv, t_real)
    rec = _griffin(xb, gate, conv_w, conv_b, rg_wa, rg_ba, rg_wx, rg_bx, rg_lambda, t_real)
    return _out_ln(h, [attn, rec], [w_out[:A_OUT], w_out[A_OUT:]], ln_g, ln_b, t_real)


def _odd_in_kernel(h_ref, wq_ref, wk_ref, wv_ref, wz_ref, wa_ref, wb_ref, q_ref, k_ref, v_ref, z_ref, a_ref, b_ref):
    a = h_ref[0].astype(BF16)
    for w_ref, o_ref in ((wq_ref, q_ref), (wk_ref, k_ref), (wv_ref, v_ref), (wz_ref, z_ref), (wa_ref, a_ref),
                         (wb_ref, b_ref)):
        o_ref[0] = jnp.dot(a, w_ref[...], preferred_element_type=F32)


def _odd_in_proj(h, w_in):
    B, TP, D = h.shape
    tm = _row_tile(TP) // 2 if _row_tile(TP) % 16 == 0 else _row_tile(TP)
    wb = w_in.astype(BF16)
    o = [0, GDN_QK_WIDTH, 2 * GDN_QK_WIDTH, 2 * GDN_QK_WIDTH + GDN_V_WIDTH, 2 * GDN_QK_WIDTH + 2 * GDN_V_WIDTH]
    o += [o[-1] + GDN_V_HEADS, o[-1] + 2 * GDN_V_HEADS]
    ws = [wb[:, o[i]:o[i + 1]] for i in range(4)]
    ws += [jnp.pad(wb[:, o[i]:o[i + 1]], ((0, 0), (0, LANES - GDN_V_HEADS))) for i in (4, 5)]
    widths = [GDN_QK_WIDTH, GDN_QK_WIDTH, GDN_V_WIDTH, GDN_V_WIDTH, LANES, LANES]
    return pl.pallas_call(
        _odd_in_kernel,
        grid=(B, TP // tm),
        in_specs=[pl.BlockSpec((1, tm, D), lambda b, j: (b, j, 0))] + [_const_spec(w.shape) for w in ws],
        out_specs=[pl.BlockSpec((1, tm, n), lambda b, j: (b, j, 0)) for n in widths],
        out_shape=[jax.ShapeDtypeStruct((B, TP, n), F32) for n in widths],
        compiler_params=_cparams("parallel", "parallel"),
        name="odd_in_proj",
    )(h, *ws)


def _conv_silu(x, win_ref, cw):
    y = _causal_conv(x, win_ref, cw)
    return y * jax.nn.sigmoid(y)


def _gdn_prep_kernel(q_ref, k_ref, v_ref, a_ref, b_ref, cwq_ref, cwk_ref, cwv_ref, alog_ref, dtb_ref,
                     u_ref, w_ref, qg_ref, kg_ref, at_ref, e_ref,
                     tq_ref, tk_ref, tv_ref, qs_ref, ks_ref, vs_ref, *, t_real):
    j = pl.program_id(1)
    tt = q_ref.shape[1]
    hd = GDN_HEAD_DIM
    half = CHUNK

    @pl.when(j == 0)
    def _():
        tq_ref[...] = jnp.zeros_like(tq_ref)
        tk_ref[...] = jnp.zeros_like(tk_ref)
        tv_ref[...] = jnp.zeros_like(tv_ref)

    rows1 = j * tt + lax.broadcasted_iota(I32, (tt, 1), 0)
    valid = _valid_rows(rows1, t_real)
    q = jnp.where(valid, _conv_silu(q_ref[0], tq_ref, cwq_ref[...]), 0.0)
    k = jnp.where(valid, _conv_silu(k_ref[0], tk_ref, cwk_ref[...]), 0.0)
    vs_ref[...] = jnp.where(valid, _conv_silu(v_ref[0], tv_ref, cwv_ref[...]), 0.0)
    for h in range(GDN_K_HEADS):
        sl = slice(h * hd, (h + 1) * hd)
        qh, kh = q[:, sl], k[:, sl]
        qs_ref[:, sl] = qh * lax.rsqrt(jnp.sum(qh * qh, axis=1, keepdims=True) + 1e-6) * (hd ** -0.5)
        ks_ref[:, sl] = kh * lax.rsqrt(jnp.sum(kh * kh, axis=1, keepdims=True) + 1e-6)

    lane = lax.broadcasted_iota(I32, (tt, LANES), 1)
    rowi = lax.broadcasted_iota(I32, (tt, LANES), 0)
    live = valid & (lane < GDN_V_HEADS)
    g = jnp.where(live, -jnp.exp(alog_ref[...]) * _softplus(a_ref[0] + dtb_ref[...]), 0.0)
    beta = jnp.where(live, jax.nn.sigmoid(b_ref[0]), 0.0)
    gc = g
    s = 1
    while s < CHUNK:
        gc = gc + jnp.where((rowi % CHUNK) >= s, pltpu.roll(gc, s, 0), 0.0)
        s *= 2
    gc_next = pltpu.roll(gc, LANES - 1, 1)
    n_ch = tt // CHUNK
    e_rows = []
    for c in range(n_ch):
        e_rows.append(jnp.broadcast_to(jnp.exp(gc[(c + 1) * CHUNK - 1:(c + 1) * CHUNK, :]), (CHUNK, LANES)))
    e_ref[0] = jnp.concatenate(e_rows, axis=0)

    nt = (((1,), (1,)), ((), ()))
    l2 = lax.broadcasted_iota(I32, (CHUNK, 2 * half), 1)
    ii = lax.broadcasted_iota(I32, (CHUNK, 2 * half), 0)
    jj = l2 % half
    left = l2 < half
    r2 = lax.broadcasted_iota(I32, (2 * half, 2 * half), 0)
    c2 = lax.broadcasted_iota(I32, (2 * half, 2 * half), 1)
    eye2 = jnp.where(r2 == c2, 1.0, 0.0)
    bdot = lambda x, y: jnp.dot(x.astype(BF16), y.astype(BF16), preferred_element_type=F32)
    for c in range(n_ch):
        rs = slice(c * CHUNK, (c + 1) * CHUNK)
        gc_c, beta_c = gc[rs], beta[rs]
        gt = jnp.concatenate([gc_c, gc_next[rs]], axis=0).T
        glast = gc_c[CHUNK - 1:CHUNK, :]
        pairs = range(GDN_K_HEADS)
        ms, tinvs, rhss = [], [], []
        for pr in pairs:
            ha, hb = 2 * pr, 2 * pr + 1
            ksl = slice(pr * hd, (pr + 1) * hd)
            osl = slice(ha * hd, (hb + 1) * hd)
            kc = ks_ref[rs, ksl]
            qc = qs_ref[rs, ksl]
            kc16 = kc.astype(BF16)
            k2 = jnp.concatenate([kc16, kc16], axis=0)
            kk2 = lax.dot_general(kc16, k2, nt, preferred_element_type=F32)
            qk2 = lax.dot_general(qc.astype(BF16), k2, nt, preferred_element_type=F32)
            gca, gcb = gc_c[:, ha:ha + 1], gc_c[:, hb:hb + 1]
            ba, bb = beta_c[:, ha:ha + 1], beta_c[:, hb:hb + 1]
            gcol2 = jnp.where(left, gca, gcb)
            bcol2 = jnp.where(left, ba, bb)
            decay2 = jnp.exp(jnp.where(ii >= jj, gcol2 - gt[ha:ha + 1, :], -jnp.inf))
            m2 = jnp.where(ii > jj, bcol2 * kk2 * decay2, 0.0)
            at_ref[0, rs, pr * 2 * half:(pr + 1) * 2 * half] = (qk2 * decay2).astype(BF16)
            m = jnp.concatenate([jnp.where(left, m2, 0.0), jnp.where(left, 0.0, m2)], axis=0)
            ms.append(m)
            tinvs.append(eye2 - m)
            ega, egb = jnp.exp(gca), jnp.exp(gcb)
            va = vs_ref[rs, ha * hd:(ha + 1) * hd]
            vb = vs_ref[rs, hb * hd:(hb + 1) * hd]
            rhss.append(jnp.concatenate([jnp.concatenate([va * ba, kc * (ba * ega)], axis=1),
                                         jnp.concatenate([vb * bb, kc * (bb * egb)], axis=1)],
                                        axis=0).astype(BF16))
            qg_ref[0, rs, osl] = jnp.concatenate([qc * ega, qc * egb], axis=1).astype(BF16)
            kg_ref[0, rs, osl] = jnp.concatenate([kc * jnp.exp(glast[:, ha:ha + 1] - gca),
                                                  kc * jnp.exp(glast[:, hb:hb + 1] - gcb)], axis=1).astype(BF16)
        pws = [bdot(m, m) for m in ms]
        n_fac = CHUNK.bit_length() - 2
        for f in range(n_fac):
            if f + 1 < n_fac:
                both = [bdot(jnp.concatenate([t, p], axis=0), p) for t, p in zip(tinvs, pws)]
                tinvs = [t + r[:2 * half] for t, r in zip(tinvs, both)]
                pws = [r[2 * half:] for r in both]
            else:
                tinvs = [t + bdot(t, p) for t, p in zip(tinvs, pws)]
        sols = [jnp.dot(t.astype(BF16), r, preferred_element_type=F32) for t, r in zip(tinvs, rhss)]
        for pr, sol in zip(pairs, sols):
            osl = slice(2 * pr * hd, (2 * pr + 2) * hd)
            u_ref[0, rs, osl] = jnp.concatenate([sol[:CHUNK, :hd], sol[CHUNK:, :hd]], axis=1)
            w_ref[0, rs, osl] = jnp.concatenate([sol[:CHUNK, hd:], sol[CHUNK:, hd:]], axis=1).astype(BF16)


def _gdn_prep(q, k, v, a, b, conv_w, a_log, dt_bias, t_real):
    B, TP, _ = q.shape
    tt = TIME_TILE
    cw = jnp.pad(conv_w.astype(F32), ((0, SUBLANES - CONV_WIDTH), (0, 0)))
    cwq, cwk, cwv = cw[:, :GDN_QK_WIDTH], cw[:, GDN_QK_WIDTH:2 * GDN_QK_WIDTH], cw[:, 2 * GDN_QK_WIDTH:]
    alog = jnp.pad(a_log.astype(F32), (0, LANES - GDN_V_HEADS)).reshape(1, LANES)
    dtb = jnp.pad(dt_bias.astype(F32), (0, LANES - GDN_V_HEADS)).reshape(1, LANES)
    tspec = lambda n: pl.BlockSpec((1, tt, n), lambda bb, j: (bb, j, 0))
    ws = (cwq, cwk, cwv, alog, dtb)
    VW, QW = GDN_V_WIDTH, GDN_QK_WIDTH
    outs = [(VW, F32), (VW, BF16), (VW, BF16), (VW, BF16), (GDN_K_HEADS * 2 * CHUNK, BF16), (LANES, F32)]
    return pl.pallas_call(
        functools.partial(_gdn_prep_kernel, t_real=t_real),
        grid=(B, TP // tt),
        in_specs=[tspec(QW), tspec(QW), tspec(VW), tspec(LANES), tspec(LANES)] + [_const_spec(w.shape) for w in ws],
        out_specs=[tspec(n) for n, _ in outs],
        out_shape=[jax.ShapeDtypeStruct((B, TP, n), dt) for n, dt in outs],
        scratch_shapes=[pltpu.VMEM((SUBLANES + tt, QW), F32), pltpu.VMEM((SUBLANES + tt, QW), F32),
                        pltpu.VMEM((SUBLANES + tt, VW), F32), pltpu.VMEM((tt, QW), F32), pltpu.VMEM((tt, QW), F32),
                        pltpu.VMEM((tt, VW), F32)],
        compiler_params=_cparams("parallel", "arbitrary"),
        name="gdn_prep",
    )(q, k, v, a, b, *ws)


def _gdn_scan_kernel(u_ref, w_ref, qg_ref, kg_ref, at_ref, e_ref, z_ref, on_ref, o_ref, s_ref):
    j = pl.program_id(1)
    tt = u_ref.shape[1]
    hd = GDN_HEAD_DIM

    @pl.when(j == 0)
    def _():
        s_ref[...] = jnp.zeros_like(s_ref)

    lane = lax.broadcasted_iota(I32, (CHUNK, 2 * CHUNK), 1)
    tn = (((0,), (0,)), ((), ()))
    heads = range(GDN_V_HEADS)
    hsl = [slice(h * hd, (h + 1) * hd) for h in heads]
    for c in range(tt // CHUNK):
        rs = slice(c * CHUNK, (c + 1) * CHUNK)
        e_row = e_ref[0, c * CHUNK:c * CHUNK + 1, :]
        ss = [s_ref[h] for h in heads]
        ws = [jnp.dot(jnp.concatenate([w_ref[0, rs, hsl[h]], qg_ref[0, rs, hsl[h]]], axis=0), ss[h].astype(BF16),
                      preferred_element_type=F32) for h in heads]
        v16 = [(u_ref[0, rs, hsl[h]] - ws[h][:CHUNK]).astype(BF16) for h in heads]
        intra = []
        for pr in range(GDN_K_HEADS):
            at2 = at_ref[0, rs, pr * 2 * CHUNK:(pr + 1) * 2 * CHUNK]
            zero = jnp.zeros_like(at2)
            lhs = jnp.concatenate([jnp.where(lane < CHUNK, at2, zero), jnp.where(lane >= CHUNK, at2, zero)], axis=0)
            rhs = jnp.concatenate([v16[2 * pr], v16[2 * pr + 1]], axis=0)
            intra.append(jnp.dot(lhs, rhs, preferred_element_type=F32))
        for h in heads:
            s_ref[h] = ss[h] * e_row[:, h:h + 1] + lax.dot_general(kg_ref[0, rs, hsl[h]], v16[h], tn,
                                                                   preferred_element_type=F32)
        for h in heads:
            o = ws[h][CHUNK:] + intra[h // 2][(h % 2) * CHUNK:(h % 2 + 1) * CHUNK]
            on = o * lax.rsqrt(jnp.mean(o * o, axis=1, keepdims=True) + 1e-6) * on_ref[...]
            z = z_ref[0, rs, hsl[h]]
            o_ref[0, rs, hsl[h]] = (on * (z * jax.nn.sigmoid(z))).astype(BF16)


def _gdn_scan(u, w, qg, kg, at, e, z, o_norm):
    B, TP, VW = u.shape
    tt = TIME_TILE
    tspec = lambda n: pl.BlockSpec((1, tt, n), lambda bb, j: (bb, j, 0))
    on = o_norm.reshape(1, GDN_HEAD_DIM).astype(F32)
    return pl.pallas_call(
        _gdn_scan_kernel,
        grid=(B, TP // tt),
        in_specs=[tspec(VW), tspec(VW), tspec(VW), tspec(VW), tspec(at.shape[-1]), tspec(LANES), tspec(VW),
                  _const_spec(on.shape)],
        out_specs=tspec(VW),
        out_shape=jax.ShapeDtypeStruct((B, TP, VW), BF16),
        scratch_shapes=[pltpu.VMEM((GDN_V_HEADS, GDN_HEAD_DIM, GDN_HEAD_DIM), F32)],
        compiler_params=_cparams("parallel", "arbitrary"),
        name="gdn_scan",
    )(u, w, qg, kg, at, e, z, on)


def _odd_layer(h, w_in, conv_w, a_log, dt_bias, o_norm, w_out, ln_g, ln_b, t_real):
    q, k, v, z, a, b = _odd_in_proj(h, w_in)
    u, w, qg, kg, at, e = _gdn_prep(q, k, v, a, b, conv_w, a_log, dt_bias, t_real)
    gated = _gdn_scan(u, w, qg, kg, at, e, z, o_norm)
    return _out_ln(h, [gated], [w_out], ln_g, ln_b, t_real)


def kernel(x, meta_tokens, even_w_in, even_kv_norm, even_w_uk, even_w_uv, even_conv_w, even_conv_b, even_rg_wa,
           even_rg_ba, even_rg_wx, even_rg_bx, even_rg_lambda, even_w_out, odd_w_in, odd_conv_w, odd_a_log,
           odd_dt_bias, odd_o_norm, odd_w_out, ln_g, ln_b, moe_group_w, moe_group_b, moe_expert_w, moe_expert_b,
           moe_w_gate, moe_w_up, moe_w_down):
    B, S, D = x.shape
    t_real = N_META + S
    tp = -(-(FRONT + t_real) // TIME_TILE) * TIME_TILE
    meta = jnp.broadcast_to(meta_tokens.astype(x.dtype)[None], (B, N_META, D))
    h = jnp.concatenate([jnp.zeros((B, FRONT, D), x.dtype), meta, x,
                         jnp.zeros((B, tp - FRONT - t_real, D), x.dtype)], axis=1)
    for layer in range(DEPTH):
        i = layer // 2
        if layer % 2 == 0:
            h = _even_layer(h, even_w_in[i], even_kv_norm[i], even_w_uk[i], even_w_uv[i], even_conv_w[i],
                            even_conv_b[i], even_rg_wa[i], even_rg_ba[i], even_rg_wx[i], even_rg_bx[i],
                            even_rg_lambda[i], even_w_out[i], ln_g[layer, 0], ln_b[layer, 0], t_real)
        else:
            h = _odd_layer(h, odd_w_in[i], odd_conv_w[i], odd_a_log[i], odd_dt_bias[i], odd_o_norm[i],
                           odd_w_out[i], ln_g[layer, 0], ln_b[layer, 0], t_real)
        h = _moe_layer(h, moe_group_w[layer], moe_group_b[layer], moe_expert_w[layer], moe_expert_b[layer],
                       moe_w_gate[layer], moe_w_up[layer], moe_w_down[layer], ln_g[layer, 1], ln_b[layer, 1], t_real)
    return h[:, FRONT:FRONT + t_real][:, N_META:]
```

```python
import functools
import math

import jax
import jax.numpy as jnp
from jax import lax
from jax.experimental import pallas as pl
from jax.experimental.pallas import tpu as pltpu

F32 = jnp.float32
BF16 = jnp.bfloat16
I32 = jnp.int32

D_MODEL = 1024
DEPTH = 2
N_META = 16
DN_ALPHA = (2 * DEPTH) ** 0.25

A_HEADS = 4
A_HEAD_DIM = 128
A_OUT = A_HEADS * A_HEAD_DIM
A_KV_RANK = 256
IDX_HEADS = 8
IDX_DIM = 64
IDX_TOPK_CAP = 256

RG_WIDTH = 512
RG_BLOCKS = 8
RG_C = 8.0
CONV_WIDTH = 4

GDN_K_HEADS = 8
GDN_V_HEADS = 16
GDN_HEAD_DIM = 128
GDN_QK_WIDTH = GDN_K_HEADS * GDN_HEAD_DIM
GDN_V_WIDTH = GDN_V_HEADS * GDN_HEAD_DIM
CHUNK = 64

MOE_GROUPS = 4
MOE_PER_GROUP = 8
N_EXPERTS = MOE_GROUPS * MOE_PER_GROUP
MOE_TOPK = 2
EXPERT_FF = 512
MOE_BLOCK = 256
FFN_BLOCK = 512

LANES = 128
SUBLANES = 8
TIME_TILE = 128
FRONT = (-N_META) % CHUNK
NEG_BIG = -1e30
INT_MIN = -2 ** 31
VMEM_LIMIT = 56 * 1024 * 1024


def _cparams(*sem, flags=None):
    return pltpu.CompilerParams(dimension_semantics=sem, vmem_limit_bytes=VMEM_LIMIT, flags=flags)


def _const_spec(shape):
    nd = len(shape)
    return pl.BlockSpec(shape, lambda *_: (0,) * nd)


def _row_tile(tp):
    best = SUBLANES
    for t in range(SUBLANES, 641, SUBLANES):
        if tp % t == 0:
            best = t
    return best


def _valid_rows(p, t_real):
    return (p >= FRONT) & (p < FRONT + t_real)


def _layer_norm_rows(y, g, b):
    mu = jnp.mean(y, axis=-1, keepdims=True)
    yc = y - mu
    var = jnp.mean(yc * yc, axis=-1, keepdims=True)
    return yc * lax.rsqrt(var + 1e-5) * g + b


def _softplus(z):
    return jnp.maximum(z, 0.0) + jnp.log1p(jnp.exp(-jnp.abs(z)))


def _even_in_kernel(h_ref, wq_ref, wc_ref, wqi_ref, wki_ref, wwi_ref, wg_ref, wx_ref, kvn_ref, wukv_ref,
                    q_ref, qi_ref, ki_ref, wi_ref, gate_ref, xb_ref, kv_ref):
    a = h_ref[0].astype(BF16)
    dot = functools.partial(jnp.dot, preferred_element_type=F32)
    q_ref[0] = dot(a, wq_ref[...]).astype(BF16)
    qi_ref[0] = dot(a, wqi_ref[...]).astype(BF16)
    ki_ref[0] = dot(a, wki_ref[...]).astype(BF16)
    wi_ref[0] = dot(a, wwi_ref[...]) * (IDX_HEADS ** -0.5) * (IDX_DIM ** -0.5)
    gate_ref[0] = dot(a, wg_ref[...])
    xb_ref[0] = dot(a, wx_ref[...])
    c = dot(a, wc_ref[...])
    latent = c * lax.rsqrt(jnp.mean(c * c, axis=-1, keepdims=True) + 1e-6) * kvn_ref[...]
    kv_ref[0] = dot(latent.astype(BF16), wukv_ref[...]).astype(BF16)


def _even_in_proj(h, w_in, kv_norm, w_uk, w_uv):
    B, TP, D = h.shape
    tm = _row_tile(TP)
    o = [0]
    for p in (A_OUT, A_KV_RANK, IDX_HEADS * IDX_DIM, IDX_DIM, IDX_HEADS, RG_WIDTH, RG_WIDTH):
        o.append(o[-1] + p)
    wb = w_in.astype(BF16)
    wq, wc = wb[:, o[0]:o[1]], wb[:, o[1]:o[2]]
    wqi = wb[:, o[2]:o[3]].reshape(D, IDX_HEADS, IDX_DIM)
    wqi = jnp.pad(wqi, ((0, 0), (0, 0), (0, LANES - IDX_DIM))).reshape(D, IDX_HEADS * LANES)
    wki = jnp.pad(wb[:, o[3]:o[4]], ((0, 0), (0, LANES - IDX_DIM)))
    wwi = jnp.pad(wb[:, o[4]:o[5]], ((0, 0), (0, LANES - IDX_HEADS)))
    wg, wx = wb[:, o[5]:o[6]], wb[:, o[6]:o[7]]
    wukv = jnp.concatenate([w_uk, w_uv], axis=1).astype(BF16)
    kvn = kv_norm.reshape(1, A_KV_RANK).astype(F32)
    ws = (wq, wc, wqi, wki, wwi, wg, wx, kvn, wukv)
    outs = [(A_OUT, BF16), (IDX_HEADS * LANES, BF16), (LANES, BF16), (LANES, F32), (RG_WIDTH, F32),
            (RG_WIDTH, F32), (2 * A_HEAD_DIM, BF16)]
    return pl.pallas_call(
        _even_in_kernel,
        grid=(B, TP // tm),
        in_specs=[pl.BlockSpec((1, tm, D), lambda b, j: (b, j, 0))] + [_const_spec(w.shape) for w in ws],
        out_specs=[pl.BlockSpec((1, tm, n), lambda b, j: (b, j, 0)) for n, _ in outs],
        out_shape=[jax.ShapeDtypeStruct((B, TP, n), dt) for n, dt in outs],
        compiler_params=_cparams("parallel", "parallel"),
        name="even_in_proj",
    )(h, *ws)


def _dsa_body(nk, q_ref, qi_ref, wi_ref, ki_ref, kv_ref, o_ref, key_ref, bias_ref, vt_ref, n_sel, t_real):
    i = pl.program_id(1)
    tq = q_ref.shape[1]
    qi = qi_ref[0]
    ki = ki_ref[0, :nk, :]
    wit = wi_ref[0].T
    nt = (((1,), (1,)), ((), ()))
    score = jnp.zeros((nk, tq), F32)
    for h in range(IDX_HEADS):
        s = lax.dot_general(ki, qi[:, h * LANES:(h + 1) * LANES], nt, preferred_element_type=F32)
        score = score + wit[h:h + 1, :] * jnp.maximum(s, 0.0)

    kpos = lax.broadcasted_iota(I32, (nk, tq), 0)
    qpos = i * tq + lax.broadcasted_iota(I32, (nk, tq), 1)
    bits = lax.bitcast_convert_type(score, I32)
    key = jnp.where(bits < 0, bits ^ jnp.int32(0x7FFFFFFF), bits)
    key_ref[:nk, :] = jnp.where(kpos <= qpos, key, jnp.int32(INT_MIN))
    key_ref[0:FRONT, :] = jnp.full((FRONT, tq), INT_MIN, I32)
    if nk > FRONT + t_real:
        key_ref[FRONT + t_real:nk, :] = jnp.full((nk - FRONT - t_real, tq), INT_MIN, I32)

    def over_keys(x, op):
        pair = jnp.add if op is jnp.sum else jnp.maximum
        group = 8
        x = x.reshape(nk // (group * SUBLANES), group, SUBLANES, x.shape[1])
        while x.shape[1] > 1:
            half = x.shape[1] // 2
            x = pair(x[:, :half], x[:, half:])
        return op(op(x[:, 0], axis=0), axis=0, keepdims=True)

    def count(mask):
        return over_keys(jnp.where(mask, 1, 0).astype(I32), jnp.sum)

    def any_query(mask):
        return jnp.max(jnp.where(mask, 1, 0))

    n_vis = count(key_ref[:nk, :] > jnp.int32(INT_MIN))

    first_bits, later_bits = 20, 4

    def bit_cond(c):
        return (c[0] < 32) & (c[3] > 0)

    def bit_group(c):
        it0, cand, cnt_c, _ = c
        n_bits = jnp.where(it0 == 0, first_bits, later_bits)

        def bit_step(b, cc):
            cand, cnt_c = cc
            trial = cand + jnp.left_shift(jnp.int32(1), 31 - (it0 + b))
            cnt = count(key_ref[:nk, :] >= trial)
            take = cnt >= n_sel
            return jnp.where(take, trial, cand), jnp.where(take, cnt, cnt_c)

        cand, cnt_c = lax.fori_loop(0, n_bits, bit_step, (cand, cnt_c))
        return it0 + n_bits, cand, cnt_c, any_query((cnt_c != n_sel) & (n_vis > n_sel))

    start = (jnp.int32(0), jnp.full((1, tq), INT_MIN, I32), jnp.full((1, tq), nk, I32),
             any_query(n_vis > n_sel))
    _, thr, cnt_ge, _ = lax.while_loop(bit_cond, bit_group, start)
    keys = key_ref[:nk, :]
    bias_ref[:nk, :] = jnp.where(keys >= jnp.maximum(thr, jnp.int32(INT_MIN + 1)), 0.0, NEG_BIG)
    tie_queries = (cnt_ge > n_sel) & (thr > jnp.int32(INT_MIN))

    @pl.when(any_query(tie_queries) > 0)
    def _():
        keys = key_ref[:nk, :]
        eq = keys == thr
        need = n_sel - count(keys > thr)
        nbits = max(1, (nk - 1).bit_length())

        def idx_step(it, cand):
            trial = cand + jnp.left_shift(jnp.int32(1), nbits - 1 - it)
            return jnp.where(count(eq & (kpos < trial)) < need, trial, cand)

        last = lax.fori_loop(0, nbits, idx_step, jnp.zeros((1, tq), I32))
        sel = (keys > thr) | (eq & (kpos <= last))
        bias_ref[:nk, :] = jnp.where(sel & (keys > jnp.int32(INT_MIN)), 0.0, NEG_BIG)

    bias = bias_ref[:nk, :]
    q = q_ref[0]
    k = kv_ref[0, :nk, :A_HEAD_DIM]
    qs = jnp.concatenate([q[:, h * A_HEAD_DIM:(h + 1) * A_HEAD_DIM] for h in range(A_HEADS)], axis=0)
    lg = lax.dot_general(k, qs, nt, preferred_element_type=F32) * (A_HEAD_DIM ** -0.5)
    lg = lg + jnp.concatenate([bias] * A_HEADS, axis=1)
    m = over_keys(lg, jnp.max)
    p = jnp.exp(lg - m)
    l = over_keys(p, jnp.sum)
    ot = jnp.dot(vt_ref[:, :nk], p.astype(BF16), preferred_element_type=F32) / l
    for h in range(A_HEADS):
        o_ref[0, :, h * A_HEAD_DIM:(h + 1) * A_HEAD_DIM] = ot[:, h * tq:(h + 1) * tq].T.astype(BF16)


DSA_WIDTH_STEP = 3


def _dsa_kernel(q_ref, qi_ref, wi_ref, ki_ref, kv_ref, o_ref, key_ref, bias_ref, vt_ref, *, n_sel, t_real):
    i = pl.program_id(1)
    tq = q_ref.shape[1]
    tp = ki_ref.shape[1]
    n_tiles = tp // tq

    @pl.when(i == 0)
    def _():
        vt_ref[...] = kv_ref[0, :, A_HEAD_DIM:].astype(F32).T.astype(BF16)

    lo = 0
    while lo < n_tiles:
        hi = min(lo + DSA_WIDTH_STEP, n_tiles)

        @pl.when((i >= lo) & (i < hi))
        def _(hi=hi):
            _dsa_body(hi * tq, q_ref, qi_ref, wi_ref, ki_ref, kv_ref, o_ref, key_ref, bias_ref, vt_ref,
                      n_sel, t_real)

        lo = hi


def _dsa(q, qi, wi, ki, kv, t_real):
    B, TP, _ = q.shape
    tq = TIME_TILE
    n_sel = min(IDX_TOPK_CAP, t_real // 4)
    kern = functools.partial(_dsa_kernel, n_sel=n_sel, t_real=t_real)
    qspec = lambda n: pl.BlockSpec((1, tq, n), lambda b, i: (b, i, 0))
    kspec = lambda n: pl.BlockSpec((1, TP, n), lambda b, i: (b, 0, 0))
    return pl.pallas_call(
        kern,
        grid=(B, TP // tq),
        in_specs=[qspec(A_OUT), qspec(IDX_HEADS * LANES), qspec(LANES), kspec(LANES), kspec(2 * A_HEAD_DIM)],
        out_specs=qspec(A_OUT),
        out_shape=jax.ShapeDtypeStruct((B, TP, A_OUT), BF16),
        scratch_shapes=[pltpu.VMEM((TP, tq), I32), pltpu.VMEM((TP, tq), F32), pltpu.VMEM((A_HEAD_DIM, TP), BF16)],
        compiler_params=_cparams("parallel", "arbitrary"),
        name="dsa_attention",
    )(q, qi, wi, ki, kv)


def _shift_rows(x, s, fill):
    rows = lax.broadcasted_iota(I32, x.shape, 0)
    return jnp.where(rows >= s, pltpu.roll(x, s, 0), fill)


def _causal_conv(x, win_ref, cw):
    tt = x.shape[0]
    win_ref[0:SUBLANES, :] = win_ref[tt:tt + SUBLANES, :]
    win_ref[SUBLANES:, :] = x
    y = cw[CONV_WIDTH - 1:CONV_WIDTH, :] * x
    for t in range(CONV_WIDTH - 1):
        off = SUBLANES - (CONV_WIDTH - 1) + t
        y = y + cw[t:t + 1, :] * win_ref[off:off + tt, :]
    return y


def _griffin_kernel(xb_ref, gate_ref, cw_ref, cb_ref, wg_ref, bg_ref, lam_ref, o_ref, tail_ref, car_ref, *, t_real):
    j = pl.program_id(1)
    tt = xb_ref.shape[1]

    @pl.when(j == 0)
    def _():
        tail_ref[...] = jnp.zeros_like(tail_ref)
        car_ref[...] = jnp.zeros_like(car_ref)

    xr = _causal_conv(xb_ref[0], tail_ref, cw_ref[...]) + cb_ref[...]
    g = jnp.dot(xr.astype(BF16), wg_ref[...], preferred_element_type=F32) + bg_ref[...]
    r = jax.nn.sigmoid(g[:, :RG_WIDTH])
    ig = jax.nn.sigmoid(g[:, RG_WIDTH:])
    log_a = -RG_C * r * _softplus(-lam_ref[...])
    a = jnp.exp(log_a)
    th = jnp.tanh(log_a)
    u = jnp.sqrt(-2.0 * th / (1.0 - th)) * (ig * xr)
    p = j * tt + lax.broadcasted_iota(I32, xr.shape, 0)
    u = jnp.where(_valid_rows(p, t_real), u, 0.0)
    s = 1
    while s < tt:
        u = u + a * _shift_rows(u, s, 0.0)
        a = a * _shift_rows(a, s, 1.0)
        s *= 2
    hcur = u + a * car_ref[0:1, :]
    car_ref[...] = jnp.broadcast_to(hcur[tt - 1:tt, :], car_ref.shape)
    o_ref[0] = hcur * jax.nn.gelu(gate_ref[0])


def _block_diag(w):
    nb, bs, _ = w.shape
    eye = jnp.eye(nb, dtype=w.dtype)
    return (w[:, :, None, :] * eye[:, None, :, None]).reshape(nb * bs, nb * bs)


def _griffin(xb, gate, conv_w, conv_b, rg_wa, rg_ba, rg_wx, rg_bx, rg_lambda, t_real):
    B, TP, R = xb.shape
    tt = TIME_TILE
    cw = jnp.pad(conv_w.astype(F32), ((0, SUBLANES - CONV_WIDTH), (0, 0)))
    cb = conv_b.reshape(1, R).astype(F32)
    wg = jnp.concatenate([_block_diag(rg_wa), _block_diag(rg_wx)], axis=1).astype(BF16)
    bg = jnp.concatenate([rg_ba, rg_bx]).reshape(1, 2 * R).astype(F32)
    lam = rg_lambda.reshape(1, R).astype(F32)
    tspec = pl.BlockSpec((1, tt, R), lambda b, j: (b, j, 0))
    ws = (cw, cb, wg, bg, lam)
    return pl.pallas_call(
        functools.partial(_griffin_kernel, t_real=t_real),
        grid=(B, TP // tt),
        in_specs=[tspec, tspec] + [_const_spec(w.shape) for w in ws],
        out_specs=tspec,
        out_shape=jax.ShapeDtypeStruct((B, TP, R), F32),
        scratch_shapes=[pltpu.VMEM((SUBLANES + tt, R), F32), pltpu.VMEM((SUBLANES, R), F32)],
        compiler_params=_cparams("parallel", "arbitrary"),
        name="griffin_rglru",
    )(xb, gate, *ws)


def _out_ln_kernel(*refs, n_in, t_real):
    h_ref = refs[0]
    a_refs = refs[1:1 + n_in]
    w_refs = refs[1 + n_in:1 + 2 * n_in]
    g_ref, b_ref, o_ref = refs[1 + 2 * n_in:]
    j = pl.program_id(1)
    tm = h_ref.shape[1]
    n_chunks = 4 if tm % (4 * SUBLANES) == 0 else 1
    tc = tm // n_chunks
    mixes = []
    for c in range(n_chunks):
        rs = slice(c * tc, (c + 1) * tc)
        mix = None
        for a_ref, w_ref in zip(a_refs, w_refs):
            d = jnp.dot(a_ref[0, rs, :].astype(BF16), w_ref[...], preferred_element_type=F32)
            mix = d if mix is None else mix + d
        mixes.append(mix)
    for c in range(n_chunks):
        rs = slice(c * tc, (c + 1) * tc)
        out = _layer_norm_rows(DN_ALPHA * h_ref[0, rs, :] + mixes[c], g_ref[...], b_ref[...])
        p = j * tm + c * tc + lax.broadcasted_iota(I32, out.shape, 0)
        o_ref[0, rs, :] = jnp.where(_valid_rows(p, t_real), out, 0.0)


def _out_ln(h, acts, ws, ln_g, ln_b, t_real):
    B, TP, D = h.shape
    tm = _row_tile(TP)
    ws = [w.astype(BF16) for w in ws]
    g = ln_g.reshape(1, D).astype(F32)
    b = ln_b.reshape(1, D).astype(F32)
    tspec = lambda n: pl.BlockSpec((1, tm, n), lambda bb, j: (bb, j, 0))
    return pl.pallas_call(
        functools.partial(_out_ln_kernel, n_in=len(acts), t_real=t_real),
        grid=(B, TP // tm),
        in_specs=[tspec(D)] + [tspec(a.shape[-1]) for a in acts] + [_const_spec(w.shape) for w in ws]
        + [_const_spec(g.shape), _const_spec(b.shape)],
        out_specs=tspec(D),
        out_shape=jax.ShapeDtypeStruct((B, TP, D), F32),
        compiler_params=_cparams("parallel", "parallel"),
        name="out_proj_ln",
    )(h, *acts, *ws, g, b)


E_LANE0, G_LANE0 = 0, N_EXPERTS
R_E1, R_E2, R_RANK1, R_RANK2, R_G1, R_G2 = range(6)


def _router_kernel(h_ref, w_ref, b_ref, rec_ref, rect_ref, cnt_ref, car_ref):
    t = pl.program_id(0)

    @pl.when(t == 0)
    def _():
        car_ref[...] = jnp.zeros_like(car_ref)

    tm = h_ref.shape[0]
    h = h_ref[...]
    h_hi = h.astype(BF16)
    h_lo = (h - h_hi.astype(F32)).astype(BF16)
    both = jnp.dot(h_hi, w_ref[...], preferred_element_type=F32)
    lg = (both[:, :LANES] + both[:, LANES:]
          + jnp.dot(h_lo, w_ref[:, :LANES], preferred_element_type=F32) + b_ref[...])
    lane = lax.broadcasted_iota(I32, lg.shape, 1)
    neg_inf = -jnp.inf

    def first_argmax(x):
        m = jnp.max(x, axis=1, keepdims=True)
        return m, jnp.min(jnp.where(x == m, lane, jnp.int32(2 ** 30)), axis=1, keepdims=True)

    glog = jnp.where((lane >= G_LANE0) & (lane < G_LANE0 + MOE_GROUPS), lg, neg_inf)
    gmax, glane = first_argmax(glog)
    grp = glane - G_LANE0
    grp_gate = 1.0 / jnp.sum(jnp.exp(glog - gmax), axis=1, keepdims=True)
    in_grp = (lane >= grp * MOE_PER_GROUP) & (lane < (grp + 1) * MOE_PER_GROUP)
    elog = jnp.where(in_grp, lg, neg_inf)
    v1, e1 = first_argmax(elog)
    v2, e2 = first_argmax(jnp.where(lane == e1, neg_inf, elog))
    ex = jnp.exp(v2 - v1)
    g1 = grp_gate / (1.0 + ex)
    g2 = grp_gate * ex / (1.0 + ex)

    oh1 = jnp.where(lane == e1, 1.0, 0.0)
    oh2 = jnp.where(lane == e2, 1.0, 0.0)
    rows = lax.broadcasted_iota(I32, (tm, tm), 0)
    cols = lax.broadcasted_iota(I32, (tm, tm), 1)
    ltri = jnp.where(cols < rows, 1.0, 0.0).astype(BF16)
    p1 = jnp.dot(ltri, oh1.astype(BF16), preferred_element_type=F32)
    p2 = jnp.dot(ltri, oh2.astype(BF16), preferred_element_type=F32)
    car = car_ref[0:1, :]
    c1 = jnp.sum(oh1, axis=0, keepdims=True)
    c2 = jnp.sum(oh2, axis=0, keepdims=True)
    rank1 = jnp.sum(oh1 * (car + p1), axis=1, keepdims=True)
    rank2 = jnp.sum(oh2 * (car + c1 + p2), axis=1, keepdims=True)
    car = car + c1 + c2
    car_ref[...] = jnp.broadcast_to(car, car_ref.shape)
    cnt_ref[...] = jnp.broadcast_to(car, cnt_ref.shape)

    rec = jnp.zeros(lg.shape, F32)
    for ln, val in ((R_E1, e1.astype(F32)), (R_E2, e2.astype(F32)), (R_RANK1, rank1), (R_RANK2, rank2),
                    (R_G1, g1), (R_G2, g2)):
        rec = jnp.where(lane == ln, val, rec)
    rec_ref[...] = rec
    rect_ref[0] = rec.T[:SUBLANES, :]


def _moe_route(hf, group_w, group_b, expert_w, expert_b):
    N, D = hf.shape
    tm = 2 * MOE_BLOCK if N % (2 * MOE_BLOCK) == 0 else MOE_BLOCK
    nt = N // tm
    w = jnp.zeros((D, LANES), F32)
    w = w.at[:, E_LANE0:E_LANE0 + N_EXPERTS].set(expert_w).at[:, G_LANE0:G_LANE0 + MOE_GROUPS].set(group_w)
    b = jnp.zeros((1, LANES), F32)
    b = b.at[0, E_LANE0:E_LANE0 + N_EXPERTS].set(expert_b).at[0, G_LANE0:G_LANE0 + MOE_GROUPS].set(group_b)
    w_hi = w.astype(BF16)
    w = jnp.concatenate([w_hi, (w - w_hi.astype(F32)).astype(BF16)], axis=1)
    return pl.pallas_call(
        _router_kernel,
        grid=(nt,),
        in_specs=[pl.BlockSpec((tm, D), lambda t: (t, 0)), _const_spec(w.shape), _const_spec(b.shape)],
        out_specs=[pl.BlockSpec((tm, LANES), lambda t: (t, 0)),
                   pl.BlockSpec((1, SUBLANES, tm), lambda t: (t, 0, 0)),
                   _const_spec((SUBLANES, LANES))],
        out_shape=[jax.ShapeDtypeStruct((N, LANES), F32),
                   jax.ShapeDtypeStruct((nt, SUBLANES, tm), F32),
                   jax.ShapeDtypeStruct((SUBLANES, LANES), F32)],
        scratch_shapes=[pltpu.VMEM((SUBLANES, LANES), F32)],
        compiler_params=_cparams("arbitrary"),
        name="moe_router",
    )(hf, w, b)


def _row_copy(src, si, dst, di, sem):
    return pltpu.make_async_copy(src.at[pl.ds(si, 1)], dst.at[pl.ds(di, 1)], sem)


def _route_copy(idx_hbm, idx_smem, sems, t):
    return pltpu.make_async_copy(idx_hbm.at[t], idx_smem.at[t % 2], sems.at[t % 2])


def _slot(pstart_ref, idx_smem, t, k, r):
    b = t % 2
    return pstart_ref[idx_smem[b, k * MOE_BLOCK + r]] + idx_smem[b, (MOE_TOPK + k) * MOE_BLOCK + r]


N_HBUF = 3


def _dispatch_kernel(pstart_ref, idx_hbm, h_hbm, xs_in, xs_hbm, idx_smem, hbuf, isems, hsems, ssems):
    del xs_in
    t = pl.program_id(0)
    nt = pl.num_programs(0)
    tm = MOE_BLOCK

    def tile_copy(u):
        return pltpu.make_async_copy(h_hbm.at[pl.ds(u * tm, tm)], hbuf.at[u % N_HBUF], hsems.at[u % N_HBUF])

    def wait_scatters(u):
        for _ in range(tm * MOE_TOPK):
            _row_copy(hbuf.at[0], 0, xs_hbm, 0, ssems.at[u % 2]).wait()

    @pl.when(t == 0)
    def _():
        _route_copy(idx_hbm, idx_smem, isems, t).start()
        tile_copy(t).start()

    _route_copy(idx_hbm, idx_smem, isems, t).wait()
    tile_copy(t).wait()

    @pl.when(t + 1 < nt)
    def _():
        _route_copy(idx_hbm, idx_smem, isems, t + 1).start()
        tile_copy(t + 1).start()

    src = hbuf.at[t % N_HBUF]
    for r in range(tm):
        for k in range(MOE_TOPK):
            _row_copy(src, r, xs_hbm, _slot(pstart_ref, idx_smem, t, k, r), ssems.at[t % 2]).start()

    @pl.when(t > 0)
    def _():
        wait_scatters(t - 1)

    @pl.when(t == nt - 1)
    def _():
        wait_scatters(t)


def _moe_dispatch(hf, idx, pstart, n_slots):
    N, D = hf.shape
    nt = N // MOE_BLOCK
    xs0 = jnp.zeros((n_slots, D), F32)
    return pl.pallas_call(
        _dispatch_kernel,
        grid_spec=pltpu.PrefetchScalarGridSpec(
            num_scalar_prefetch=1,
            grid=(nt,),
            in_specs=[pl.BlockSpec(memory_space=pl.ANY)] * 3,
            out_specs=pl.BlockSpec(memory_space=pl.ANY),
            scratch_shapes=[pltpu.SMEM((2, 2 * MOE_TOPK * MOE_BLOCK), I32),
                            pltpu.VMEM((N_HBUF, MOE_BLOCK, D), F32),
                            pltpu.SemaphoreType.DMA((2,)), pltpu.SemaphoreType.DMA((N_HBUF,)),
                            pltpu.SemaphoreType.DMA((2,))],
        ),
        out_shape=jax.ShapeDtypeStruct((n_slots, D), F32),
        input_output_aliases={3: 0},
        compiler_params=_cparams("arbitrary"),
        name="moe_dispatch",
    )(pstart, idx, hf, xs0)


def _ffn_kernel(be_ref, nb_ref, x_ref, wg_ref, wu_ref, wd_ref, o_ref, wg16, wu16, wd16):
    b = pl.program_id(0)

    @pl.when((b == 0) | (be_ref[b] != be_ref[jnp.maximum(b - 1, 0)]))
    def _():
        wg16[...] = wg_ref[...].astype(BF16)
        wu16[...] = wu_ref[...].astype(BF16)
        wd16[...] = wd_ref[...].astype(BF16)

    @pl.when(b < nb_ref[0])
    def _():
        n_chunks = x_ref.shape[0] // TIME_TILE
        rows = [slice(c * TIME_TILE, (c + 1) * TIME_TILE) for c in range(n_chunks)]
        xs = [x_ref[r, :].astype(BF16) for r in rows]
        hgs = [jnp.dot(x, wg16[...], preferred_element_type=F32) for x in xs]
        hus = [jnp.dot(x, wu16[...], preferred_element_type=F32) for x in xs]
        hids = [((hg * jax.nn.sigmoid(hg)) * hu).astype(BF16) for hg, hu in zip(hgs, hus)]
        for r, hid in zip(rows, hids):
            o_ref[r, :] = jnp.dot(hid, wd16[...], preferred_element_type=F32)

    @pl.when(pl.program_id(0) >= nb_ref[0])
    def _():
        o_ref[...] = jnp.zeros_like(o_ref)


def _moe_ffn(xs, block_expert, nb_used, w_gate, w_up, w_down):
    P, D = xs.shape
    nb = P // FFN_BLOCK
    wspec = lambda s: pl.BlockSpec((None,) + s, lambda b, be, nbu: (be[b], 0, 0))
    return pl.pallas_call(
        _ffn_kernel,
        grid_spec=pltpu.PrefetchScalarGridSpec(
            num_scalar_prefetch=2,
            grid=(nb,),
            in_specs=[pl.BlockSpec((FFN_BLOCK, D), lambda b, be, nbu: (b, 0)),
                      wspec((D, EXPERT_FF)), wspec((D, EXPERT_FF)), wspec((EXPERT_FF, D))],
            out_specs=pl.BlockSpec((FFN_BLOCK, D), lambda b, be, nbu: (b, 0)),
            scratch_shapes=[pltpu.VMEM((D, EXPERT_FF), BF16), pltpu.VMEM((D, EXPERT_FF), BF16),
                            pltpu.VMEM((EXPERT_FF, D), BF16)],
        ),
        out_shape=jax.ShapeDtypeStruct((P, D), F32),
        compiler_params=_cparams("arbitrary"),
        name="moe_expert_ffn",
    )(block_expert, nb_used, xs, w_gate, w_up, w_down)


def _combine_kernel(pstart_ref, idx_hbm, ys_hbm, h_ref, rec_ref, g_ref, b_ref, o_ref, idx_smem, ybuf, isems, gsems,
                    *, tp, t_real):
    t = pl.program_id(0)
    nt = pl.num_programs(0)
    tm = MOE_BLOCK

    def issue_gathers(u):
        dst = ybuf.at[u % 2]
        for r in range(tm):
            for k in range(MOE_TOPK):
                _row_copy(ys_hbm, _slot(pstart_ref, idx_smem, u, k, r), dst.at[k], r, gsems.at[u % 2]).start()

    @pl.when(t == 0)
    def _():
        _route_copy(idx_hbm, idx_smem, isems, t).start()
        _route_copy(idx_hbm, idx_smem, isems, t).wait()
        issue_gathers(t)

        @pl.when(nt > 1)
        def _():
            _route_copy(idx_hbm, idx_smem, isems, t + 1).start()

    @pl.when(t + 1 < nt)
    def _():
        _route_copy(idx_hbm, idx_smem, isems, t + 1).wait()
        issue_gathers(t + 1)

    @pl.when(t + 2 < nt)
    def _():
        _route_copy(idx_hbm, idx_smem, isems, t + 2).start()

    for _ in range(tm * MOE_TOPK):
        _row_copy(ys_hbm, 0, ybuf.at[0, 0], 0, gsems.at[t % 2]).wait()
    b = t % 2
    rec = rec_ref[...]
    y = DN_ALPHA * h_ref[...] + rec[:, R_G1:R_G1 + 1] * ybuf[b, 0] + rec[:, R_G2:R_G2 + 1] * ybuf[b, 1]
    out = _layer_norm_rows(y, g_ref[...], b_ref[...])
    n = (t * tm + lax.broadcasted_iota(I32, out.shape, 0)).astype(F32)
    p = n - jnp.floor((n + 0.5) / tp) * tp
    o_ref[...] = jnp.where((p >= FRONT) & (p < FRONT + t_real), out, 0.0)


def _moe_combine(hf, ys, idx, rec, pstart, ln_g, ln_b, tp, t_real):
    N, D = hf.shape
    nt = N // MOE_BLOCK
    g = ln_g.reshape(1, D).astype(F32)
    b = ln_b.reshape(1, D).astype(F32)
    anyspec = pl.BlockSpec(memory_space=pl.ANY)
    return pl.pallas_call(
        functools.partial(_combine_kernel, tp=tp, t_real=t_real),
        grid_spec=pltpu.PrefetchScalarGridSpec(
            num_scalar_prefetch=1,
            grid=(nt,),
            in_specs=[anyspec, anyspec,
                      pl.BlockSpec((MOE_BLOCK, D), lambda t, ps: (t, 0)),
                      pl.BlockSpec((MOE_BLOCK, LANES), lambda t, ps: (t, 0)),
                      pl.BlockSpec((1, D), lambda t, ps: (0, 0)), pl.BlockSpec((1, D), lambda t, ps: (0, 0))],
            out_specs=pl.BlockSpec((MOE_BLOCK, D), lambda t, ps: (t, 0)),
            scratch_shapes=[pltpu.SMEM((2, 2 * MOE_TOPK * MOE_BLOCK), I32),
                            pltpu.VMEM((2, MOE_TOPK, MOE_BLOCK, D), F32),
                            pltpu.SemaphoreType.DMA((2,)), pltpu.SemaphoreType.DMA((2,))],
        ),
        out_shape=jax.ShapeDtypeStruct((N, D), F32),
        compiler_params=_cparams("arbitrary"),
        name="moe_combine_ln",
    )(pstart, idx, ys, hf, rec, g, b)


def _moe_layer(h, group_w, group_b, expert_w, expert_b, w_gate, w_up, w_down, ln_g, ln_b, t_real):
    B, TP, D = h.shape
    hf = h.reshape(B * TP, D)
    N = B * TP
    rec, rect, cnt = _moe_route(hf, group_w, group_b, expert_w, expert_b)
    counts = cnt[0, :N_EXPERTS].astype(I32)
    padded = (counts + FFN_BLOCK - 1) // FFN_BLOCK * FFN_BLOCK
    pend = jnp.cumsum(padded)
    pstart = (pend - padded).astype(I32)
    n_blocks = -(-(N * MOE_TOPK) // FFN_BLOCK) + N_EXPERTS
    block_start = jnp.arange(n_blocks, dtype=I32) * FFN_BLOCK
    block_expert = jnp.minimum(jnp.sum((pend[None, :] <= block_start[:, None]).astype(I32), axis=1), N_EXPERTS - 1)
    nb_used = (pend[-1:] // FFN_BLOCK).astype(I32)
    nf = 2 * MOE_TOPK
    idx = rect[:, :nf, :].astype(I32).reshape(rect.shape[0], nf, rect.shape[2] // MOE_BLOCK, MOE_BLOCK)
    idx = idx.transpose(0, 2, 1, 3).reshape(N // MOE_BLOCK, nf * MOE_BLOCK)
    xs = _moe_dispatch(hf, idx, pstart, n_blocks * FFN_BLOCK)
    ys = _moe_ffn(xs, block_expert, nb_used, w_gate, w_up, w_down)
    out = _moe_combine(hf, ys, idx, rec, pstart, ln_g, ln_b, TP, t_real)
    return out.reshape(B, TP, D)


def _even_layer(h, w_in, kv_norm, w_uk, w_uv, conv_w, conv_b, rg_wa, rg_ba, rg_wx, rg_bx, rg_lambda, w_out,
                ln_g, ln_b, t_real):
    q, qi, ki, wi, gate, xb, kv = _even_in_proj(h, w_in, kv_norm, w_uk, w_uv)
    attn = _dsa(q, qi, wi, ki, kv, t_real)
    rec = _griffin(xb, gate, conv_w, conv_b, rg_wa, rg_ba, rg_wx, rg_bx, rg_lambda, t_real)
    return _out_ln(h, [attn, rec], [w_out[:A_OUT], w_out[A_OUT:]], ln_g, ln_b, t_real)


def _odd_in_kernel(h_ref, wq_ref, wk_ref, wv_ref, wz_ref, wa_ref, wb_ref, q_ref, k_ref, v_ref, z_ref, a_ref, b_ref):
    a = h_ref[0].astype(BF16)
    for w_ref, o_ref in ((wq_ref, q_ref), (wk_ref, k_ref), (wv_ref, v_ref), (wz_ref, z_ref), (wa_ref, a_ref),
                         (wb_ref, b_ref)):
        o_ref[0] = jnp.dot(a, w_ref[...], preferred_element_type=F32).astype(o_ref.dtype)


def _odd_in_proj(h, w_in):
    B, TP, D = h.shape
    tm = _row_tile(TP)
    wb = w_in.astype(BF16)
    o = [0, GDN_QK_WIDTH, 2 * GDN_QK_WIDTH, 2 * GDN_QK_WIDTH + GDN_V_WIDTH, 2 * GDN_QK_WIDTH + 2 * GDN_V_WIDTH]
    o += [o[-1] + GDN_V_HEADS, o[-1] + 2 * GDN_V_HEADS]
    ws = [wb[:, o[i]:o[i + 1]] for i in range(4)]
    ws += [jnp.pad(wb[:, o[i]:o[i + 1]], ((0, 0), (0, LANES - GDN_V_HEADS))) for i in (4, 5)]
    widths = [GDN_QK_WIDTH, GDN_QK_WIDTH, GDN_V_WIDTH, GDN_V_WIDTH, LANES, LANES]
    return pl.pallas_call(
        _odd_in_kernel,
        grid=(B, TP // tm),
        in_specs=[pl.BlockSpec((1, tm, D), lambda b, j: (b, j, 0))] + [_const_spec(w.shape) for w in ws],
        out_specs=[pl.BlockSpec((1, tm, n), lambda b, j: (b, j, 0)) for n in widths],
        out_shape=[jax.ShapeDtypeStruct((B, TP, n), BF16 if n > LANES else F32) for n in widths],
        compiler_params=_cparams("parallel", "parallel"),
        name="odd_in_proj",
    )(h, *ws)


def _conv_silu(x, win_ref, cw):
    y = _causal_conv(x, win_ref, cw)
    return y * jax.nn.sigmoid(y)


def _gdn_prep_kernel(q_ref, k_ref, v_ref, a_ref, b_ref, cwq_ref, cwk_ref, cwv_ref, alog_ref, dtb_ref,
                     u_ref, w_ref, qg_ref, kg_ref, at_ref, e_ref,
                     tq_ref, tk_ref, tv_ref, qs_ref, ks_ref, vs_ref, *, t_real):
    j = pl.program_id(1)
    tt = q_ref.shape[1]
    hd = GDN_HEAD_DIM
    half = CHUNK

    @pl.when(j == 0)
    def _():
        tq_ref[...] = jnp.zeros_like(tq_ref)
        tk_ref[...] = jnp.zeros_like(tk_ref)
        tv_ref[...] = jnp.zeros_like(tv_ref)

    rows1 = j * tt + lax.broadcasted_iota(I32, (tt, 1), 0)
    valid = _valid_rows(rows1, t_real)
    q = jnp.where(valid, _conv_silu(q_ref[0].astype(F32), tq_ref, cwq_ref[...]), 0.0)
    k = jnp.where(valid, _conv_silu(k_ref[0].astype(F32), tk_ref, cwk_ref[...]), 0.0)
    vs_ref[...] = jnp.where(valid, _conv_silu(v_ref[0].astype(F32), tv_ref, cwv_ref[...]), 0.0)
    for h in range(GDN_K_HEADS):
        sl = slice(h * hd, (h + 1) * hd)
        qh, kh = q[:, sl], k[:, sl]
        qs_ref[:, sl] = qh * lax.rsqrt(jnp.sum(qh * qh, axis=1, keepdims=True) + 1e-6) * (hd ** -0.5)
        ks_ref[:, sl] = kh * lax.rsqrt(jnp.sum(kh * kh, axis=1, keepdims=True) + 1e-6)

    lane = lax.broadcasted_iota(I32, (tt, LANES), 1)
    rowi = lax.broadcasted_iota(I32, (tt, LANES), 0)
    live = valid & (lane < GDN_V_HEADS)
    g = jnp.where(live, -jnp.exp(alog_ref[...]) * _softplus(a_ref[0] + dtb_ref[...]), 0.0)
    beta = jnp.where(live, jax.nn.sigmoid(b_ref[0]), 0.0)
    gc = g
    s = 1
    while s < CHUNK:
        gc = gc + jnp.where((rowi % CHUNK) >= s, pltpu.roll(gc, s, 0), 0.0)
        s *= 2
    gc_next = pltpu.roll(gc, LANES - 1, 1)
    n_ch = tt // CHUNK
    e_rows = []
    for c in range(n_ch):
        e_rows.append(jnp.broadcast_to(jnp.exp(gc[(c + 1) * CHUNK - 1:(c + 1) * CHUNK, :]), (CHUNK, LANES)))
    e_ref[0] = jnp.concatenate(e_rows, axis=0)

    nt = (((1,), (1,)), ((), ()))
    l2 = lax.broadcasted_iota(I32, (CHUNK, 2 * half), 1)
    ii = lax.broadcasted_iota(I32, (CHUNK, 2 * half), 0)
    jj = l2 % half
    left = l2 < half
    r2 = lax.broadcasted_iota(I32, (2 * half, 2 * half), 0)
    c2 = lax.broadcasted_iota(I32, (2 * half, 2 * half), 1)
    eye2 = jnp.where(r2 == c2, 1.0, 0.0)
    bdot = lambda x, y: jnp.dot(x.astype(BF16), y.astype(BF16), preferred_element_type=F32)
    pair_group = GDN_K_HEADS
    for c, g0 in [(c, g0) for c in range(n_ch) for g0 in range(0, GDN_K_HEADS, pair_group)]:
        rs = slice(c * CHUNK, (c + 1) * CHUNK)
        gc_c, beta_c = gc[rs], beta[rs]
        gt = jnp.concatenate([gc_c, gc_next[rs]], axis=0).T
        glast = gc_c[CHUNK - 1:CHUNK, :]
        pairs = range(g0, g0 + pair_group)
        ms, tinvs, rhss = [], [], []
        for pr in pairs:
            ha, hb = 2 * pr, 2 * pr + 1
            ksl = slice(pr * hd, (pr + 1) * hd)
            osl = slice(ha * hd, (hb + 1) * hd)
            kc = ks_ref[rs, ksl]
            qc = qs_ref[rs, ksl]
            kc16 = kc.astype(BF16)
            k2 = jnp.concatenate([kc16, kc16], axis=0)
            kk2 = lax.dot_general(kc16, k2, nt, preferred_element_type=F32)
            qk2 = lax.dot_general(qc.astype(BF16), k2, nt, preferred_element_type=F32)
            gca, gcb = gc_c[:, ha:ha + 1], gc_c[:, hb:hb + 1]
            ba, bb = beta_c[:, ha:ha + 1], beta_c[:, hb:hb + 1]
            gcol2 = jnp.where(left, gca, gcb)
            bcol2 = jnp.where(left, ba, bb)
            decay2 = jnp.exp(jnp.where(ii >= jj, gcol2 - gt[ha:ha + 1, :], -jnp.inf))
            m2 = jnp.where(ii > jj, bcol2 * kk2 * decay2, 0.0)
            at_ref[0, rs, pr * 2 * half:(pr + 1) * 2 * half] = (qk2 * decay2).astype(BF16)
            m = jnp.concatenate([jnp.where(left, m2, 0.0), jnp.where(left, 0.0, m2)], axis=0)
            ms.append(m)
            tinvs.append(eye2 - m)
            ega, egb = jnp.exp(gca), jnp.exp(gcb)
            va = vs_ref[rs, ha * hd:(ha + 1) * hd]
            vb = vs_ref[rs, hb * hd:(hb + 1) * hd]
            rhss.append(jnp.concatenate([jnp.concatenate([va * ba, kc * (ba * ega)], axis=1),
                                         jnp.concatenate([vb * bb, kc * (bb * egb)], axis=1)],
                                        axis=0).astype(BF16))
            qg_ref[0, rs, osl] = jnp.concatenate([qc * ega, qc * egb], axis=1).astype(BF16)
            kg_ref[0, rs, osl] = jnp.concatenate([kc * jnp.exp(glast[:, ha:ha + 1] - gca),
                                                  kc * jnp.exp(glast[:, hb:hb + 1] - gcb)], axis=1).astype(BF16)
        pws = [bdot(m, m) for m in ms]
        n_fac = CHUNK.bit_length() - 2
        for f in range(n_fac):
            if f + 1 < n_fac:
                both = [bdot(jnp.concatenate([t, p], axis=0), p) for t, p in zip(tinvs, pws)]
                tinvs = [t + r[:2 * half] for t, r in zip(tinvs, both)]
                pws = [r[2 * half:] for r in both]
            else:
                tinvs = [t + bdot(t, p) for t, p in zip(tinvs, pws)]
        sols = [jnp.dot(t.astype(BF16), r, preferred_element_type=F32) for t, r in zip(tinvs, rhss)]
        for pr, sol in zip(pairs, sols):
            osl = slice(2 * pr * hd, (2 * pr + 2) * hd)
            u_ref[0, rs, osl] = jnp.concatenate([sol[:CHUNK, :hd], sol[CHUNK:, :hd]], axis=1)
            w_ref[0, rs, osl] = jnp.concatenate([sol[:CHUNK, hd:], sol[CHUNK:, hd:]], axis=1).astype(BF16)


def _gdn_prep(q, k, v, a, b, conv_w, a_log, dt_bias, t_real):
    B, TP, _ = q.shape
    tt = TIME_TILE
    cw = jnp.pad(conv_w.astype(F32), ((0, SUBLANES - CONV_WIDTH), (0, 0)))
    cwq, cwk, cwv = cw[:, :GDN_QK_WIDTH], cw[:, GDN_QK_WIDTH:2 * GDN_QK_WIDTH], cw[:, 2 * GDN_QK_WIDTH:]
    alog = jnp.pad(a_log.astype(F32), (0, LANES - GDN_V_HEADS)).reshape(1, LANES)
    dtb = jnp.pad(dt_bias.astype(F32), (0, LANES - GDN_V_HEADS)).reshape(1, LANES)
    tspec = lambda n: pl.BlockSpec((1, tt, n), lambda bb, j: (bb, j, 0))
    ws = (cwq, cwk, cwv, alog, dtb)
    VW, QW = GDN_V_WIDTH, GDN_QK_WIDTH
    outs = [(VW, F32), (VW, BF16), (VW, BF16), (VW, BF16), (GDN_K_HEADS * 2 * CHUNK, BF16), (LANES, F32)]
    return pl.pallas_call(
        functools.partial(_gdn_prep_kernel, t_real=t_real),
        grid=(B, TP // tt),
        in_specs=[tspec(QW), tspec(QW), tspec(VW), tspec(LANES), tspec(LANES)] + [_const_spec(w.shape) for w in ws],
        out_specs=[tspec(n) for n, _ in outs],
        out_shape=[jax.ShapeDtypeStruct((B, TP, n), dt) for n, dt in outs],
        scratch_shapes=[pltpu.VMEM((SUBLANES + tt, QW), F32), pltpu.VMEM((SUBLANES + tt, QW), F32),
                        pltpu.VMEM((SUBLANES + tt, VW), F32), pltpu.VMEM((tt, QW), F32), pltpu.VMEM((tt, QW), F32),
                        pltpu.VMEM((tt, VW), F32)],
        compiler_params=_cparams("parallel", "arbitrary"),
        name="gdn_prep",
    )(q, k, v, a, b, *ws)


def _gdn_scan_kernel(u_ref, w_ref, qg_ref, kg_ref, at_ref, e_ref, z_ref, on_ref, o_ref, s_ref):
    j = pl.program_id(1)
    tt = u_ref.shape[1]
    hd = GDN_HEAD_DIM

    @pl.when(j == 0)
    def _():
        s_ref[...] = jnp.zeros_like(s_ref)

    lane = lax.broadcasted_iota(I32, (CHUNK, 2 * CHUNK), 1)
    tn = (((0,), (0,)), ((), ()))
    heads = range(GDN_V_HEADS)
    hsl = [slice(h * hd, (h + 1) * hd) for h in heads]
    for c in range(tt // CHUNK):
        rs = slice(c * CHUNK, (c + 1) * CHUNK)
        e_row = e_ref[0, c * CHUNK:c * CHUNK + 1, :]
        ss = [s_ref[h] for h in heads]
        ws = [jnp.dot(jnp.concatenate([w_ref[0, rs, hsl[h]], qg_ref[0, rs, hsl[h]]], axis=0), ss[h].astype(BF16),
                      preferred_element_type=F32) for h in heads]
        v16 = [(u_ref[0, rs, hsl[h]] - ws[h][:CHUNK]).astype(BF16) for h in heads]
        intra = []
        for pr in range(GDN_K_HEADS):
            at2 = at_ref[0, rs, pr * 2 * CHUNK:(pr + 1) * 2 * CHUNK]
            zero = jnp.zeros_like(at2)
            lhs = jnp.concatenate([jnp.where(lane < CHUNK, at2, zero), jnp.where(lane >= CHUNK, at2, zero)], axis=0)
            rhs = jnp.concatenate([v16[2 * pr], v16[2 * pr + 1]], axis=0)
            intra.append(jnp.dot(lhs, rhs, preferred_element_type=F32))
        for h in heads:
            s_ref[h] = ss[h] * e_row[:, h:h + 1] + lax.dot_general(kg_ref[0, rs, hsl[h]], v16[h], tn,
                                                                   preferred_element_type=F32)
        for h in heads:
            o = ws[h][CHUNK:] + intra[h // 2][(h % 2) * CHUNK:(h % 2 + 1) * CHUNK]
            on = o * lax.rsqrt(jnp.mean(o * o, axis=1, keepdims=True) + 1e-6) * on_ref[...]
            z = z_ref[0, rs, hsl[h]].astype(F32)
            o_ref[0, rs, hsl[h]] = (on * (z * jax.nn.sigmoid(z))).astype(BF16)


def _gdn_scan(u, w, qg, kg, at, e, z, o_norm):
    B, TP, VW = u.shape
    tt = TIME_TILE
    tspec = lambda n: pl.BlockSpec((1, tt, n), lambda bb, j: (bb, j, 0))
    on = o_norm.reshape(1, GDN_HEAD_DIM).astype(F32)
    return pl.pallas_call(
        _gdn_scan_kernel,
        grid=(B, TP // tt),
        in_specs=[tspec(VW), tspec(VW), tspec(VW), tspec(VW), tspec(at.shape[-1]), tspec(LANES), tspec(VW),
                  _const_spec(on.shape)],
        out_specs=tspec(VW),
        out_shape=jax.ShapeDtypeStruct((B, TP, VW), BF16),
        scratch_shapes=[pltpu.VMEM((GDN_V_HEADS, GDN_HEAD_DIM, GDN_HEAD_DIM), F32)],
        compiler_params=_cparams("parallel", "arbitrary"),
        name="gdn_scan",
    )(u, w, qg, kg, at, e, z, on)


def _odd_layer(h, w_in, conv_w, a_log, dt_bias, o_norm, w_out, ln_g, ln_b, t_real):
    q, k, v, z, a, b = _odd_in_proj(h, w_in)
    u, w, qg, kg, at, e = _gdn_prep(q, k, v, a, b, conv_w, a_log, dt_bias, t_real)
    gated = _gdn_scan(u, w, qg, kg, at, e, z, o_norm)
    return _out_ln(h, [gated], [w_out], ln_g, ln_b, t_real)


def kernel(x, meta_tokens, even_w_in, even_kv_norm, even_w_uk, even_w_uv, even_conv_w, even_conv_b, even_rg_wa,
           even_rg_ba, even_rg_wx, even_rg_bx, even_rg_lambda, even_w_out, odd_w_in, odd_conv_w, odd_a_log,
           odd_dt_bias, odd_o_norm, odd_w_out, ln_g, ln_b, moe_group_w, moe_group_b, moe_expert_w, moe_expert_b,
           moe_w_gate, moe_w_up, moe_w_down):
    B, S, D = x.shape
    t_real = N_META + S
    tp = -(-(FRONT + t_real) // TIME_TILE) * TIME_TILE
    meta = jnp.broadcast_to(meta_tokens.astype(x.dtype)[None], (B, N_META, D))
    h = jnp.concatenate([jnp.zeros((B, FRONT, D), x.dtype), meta, x,
                         jnp.zeros((B, tp - FRONT - t_real, D), x.dtype)], axis=1)
    for layer in range(DEPTH):
        i = layer // 2
        if layer % 2 == 0:
            h = _even_layer(h, even_w_in[i], even_kv_norm[i], even_w_uk[i], even_w_uv[i], even_conv_w[i],
                            even_conv_b[i], even_rg_wa[i], even_rg_ba[i], even_rg_wx[i], even_rg_bx[i],
                            even_rg_lambda[i], even_w_out[i], ln_g[layer, 0], ln_b[layer, 0], t_real)
        else:
            h = _odd_layer(h, odd_w_in[i], odd_conv_w[i], odd_a_log[i], odd_dt_bias[i], odd_o_norm[i],
                           odd_w_out[i], ln_g[layer, 0], ln_b[layer, 0], t_real)
        h = _moe_layer(h, moe_group_w[layer], moe_group_b[layer], moe_expert_w[layer], moe_expert_b[layer],
                       moe_w_gate[layer], moe_w_up[layer], moe_w_down[layer], ln_g[layer, 1], ln_b[layer, 1], t_real)
    return h[:, FRONT:FRONT + t_real][:, N_META:]
```

```python
import functools
import math

import jax
import jax.numpy as jnp
from jax import lax
from jax.experimental import pallas as pl
from jax.experimental.pallas import tpu as pltpu

F32 = jnp.float32
BF16 = jnp.bfloat16
I32 = jnp.int32

D_MODEL = 1024
DEPTH = 2
N_META = 16
DN_ALPHA = (2 * DEPTH) ** 0.25

A_HEADS = 4
A_HEAD_DIM = 128
A_OUT = A_HEADS * A_HEAD_DIM
A_KV_RANK = 256
IDX_HEADS = 8
IDX_DIM = 64
IDX_TOPK_CAP = 256

RG_WIDTH = 512
RG_BLOCKS = 8
RG_C = 8.0
CONV_WIDTH = 4

GDN_K_HEADS = 8
GDN_V_HEADS = 16
GDN_HEAD_DIM = 128
GDN_QK_WIDTH = GDN_K_HEADS * GDN_HEAD_DIM
GDN_V_WIDTH = GDN_V_HEADS * GDN_HEAD_DIM
CHUNK = 64

MOE_GROUPS = 4
MOE_PER_GROUP = 8
N_EXPERTS = MOE_GROUPS * MOE_PER_GROUP
MOE_TOPK = 2
EXPERT_FF = 512
MOE_BLOCK = 256
FFN_BLOCK = 512

LANES = 128
SUBLANES = 8
TIME_TILE = 128
FRONT = (-N_META) % CHUNK
NEG_BIG = -1e30
INT_MIN = -2 ** 31
VMEM_LIMIT = 56 * 1024 * 1024


def _cparams(*sem, flags=None):
    return pltpu.CompilerParams(dimension_semantics=sem, vmem_limit_bytes=VMEM_LIMIT, flags=flags)


def _const_spec(shape):
    nd = len(shape)
    return pl.BlockSpec(shape, lambda *_: (0,) * nd)


def _row_tile(tp):
    best = SUBLANES
    for t in range(SUBLANES, 641, SUBLANES):
        if tp % t == 0:
            best = t
    return best


def _valid_rows(p, t_real):
    return (p >= FRONT) & (p < FRONT + t_real)


def _layer_norm_rows(y, g, b):
    mu = jnp.mean(y, axis=-1, keepdims=True)
    yc = y - mu
    var = jnp.mean(yc * yc, axis=-1, keepdims=True)
    return yc * lax.rsqrt(var + 1e-5) * g + b


def _softplus(z):
    return jnp.maximum(z, 0.0) + jnp.log(1.0 + jnp.exp(-jnp.abs(z)))


def _even_in_kernel(h_ref, wq_ref, wc_ref, wqi_ref, wki_ref, wwi_ref, wg_ref, wx_ref, kvn_ref, wukv_ref,
                    q_ref, qi_ref, ki_ref, wi_ref, gate_ref, xb_ref, kv_ref):
    a = h_ref[0].astype(BF16)
    dot = functools.partial(jnp.dot, preferred_element_type=F32)
    q_ref[0] = dot(a, wq_ref[...]).astype(BF16)
    qi_ref[0] = dot(a, wqi_ref[...]).astype(BF16)
    ki_ref[0] = dot(a, wki_ref[...]).astype(BF16)
    wi_ref[0] = dot(a, wwi_ref[...]) * (IDX_HEADS ** -0.5) * (IDX_DIM ** -0.5)
    gate_ref[0] = dot(a, wg_ref[...])
    xb_ref[0] = dot(a, wx_ref[...])
    c = dot(a, wc_ref[...])
    latent = c * lax.rsqrt(jnp.mean(c * c, axis=-1, keepdims=True) + 1e-6) * kvn_ref[...]
    kv_ref[0] = dot(latent.astype(BF16), wukv_ref[...]).astype(BF16)


def _even_in_proj(h, w_in, kv_norm, w_uk, w_uv):
    B, TP, D = h.shape
    tm = _row_tile(TP)
    o = [0]
    for p in (A_OUT, A_KV_RANK, IDX_HEADS * IDX_DIM, IDX_DIM, IDX_HEADS, RG_WIDTH, RG_WIDTH):
        o.append(o[-1] + p)
    wb = w_in.astype(BF16)
    wq, wc = wb[:, o[0]:o[1]], wb[:, o[1]:o[2]]
    wqi = wb[:, o[2]:o[3]].reshape(D, IDX_HEADS, IDX_DIM)
    wqi = jnp.pad(wqi, ((0, 0), (0, 0), (0, LANES - IDX_DIM))).reshape(D, IDX_HEADS * LANES)
    wki = jnp.pad(wb[:, o[3]:o[4]], ((0, 0), (0, LANES - IDX_DIM)))
    wwi = jnp.pad(wb[:, o[4]:o[5]], ((0, 0), (0, LANES - IDX_HEADS)))
    wg, wx = wb[:, o[5]:o[6]], wb[:, o[6]:o[7]]
    wukv = jnp.concatenate([w_uk, w_uv], axis=1).astype(BF16)
    kvn = kv_norm.reshape(1, A_KV_RANK).astype(F32)
    ws = (wq, wc, wqi, wki, wwi, wg, wx, kvn, wukv)
    outs = [(A_OUT, BF16), (IDX_HEADS * LANES, BF16), (LANES, BF16), (LANES, F32), (RG_WIDTH, F32),
            (RG_WIDTH, F32), (2 * A_HEAD_DIM, BF16)]
    return pl.pallas_call(
        _even_in_kernel,
        grid=(B, TP // tm),
        in_specs=[pl.BlockSpec((1, tm, D), lambda b, j: (b, j, 0))] + [_const_spec(w.shape) for w in ws],
        out_specs=[pl.BlockSpec((1, tm, n), lambda b, j: (b, j, 0)) for n, _ in outs],
        out_shape=[jax.ShapeDtypeStruct((B, TP, n), dt) for n, dt in outs],
        compiler_params=_cparams("parallel", "parallel"),
        name="even_in_proj",
    )(h, *ws)


def _dsa_body(nk, q_ref, qi_ref, wi_ref, ki_ref, kv_ref, o_ref, key_ref, bias_ref, vt_ref, n_sel, t_real):
    i = pl.program_id(1)
    tq = q_ref.shape[1]
    qi = qi_ref[0]
    ki = ki_ref[0, :nk, :]
    wit = wi_ref[0].T
    nt = (((1,), (1,)), ((), ()))
    score = jnp.zeros((nk, tq), F32)
    for h in range(IDX_HEADS):
        s = lax.dot_general(ki, qi[:, h * LANES:(h + 1) * LANES], nt, preferred_element_type=F32)
        score = score + wit[h:h + 1, :] * jnp.maximum(s, 0.0)

    kpos = lax.broadcasted_iota(I32, (nk, tq), 0)
    qpos = i * tq + lax.broadcasted_iota(I32, (nk, tq), 1)
    bits = lax.bitcast_convert_type(score, I32)
    key = jnp.where(bits < 0, bits ^ jnp.int32(0x7FFFFFFF), bits)
    key_ref[:nk, :] = jnp.where(kpos <= qpos, key, jnp.int32(INT_MIN))
    key_ref[0:FRONT, :] = jnp.full((FRONT, tq), INT_MIN, I32)
    if nk > FRONT + t_real:
        key_ref[FRONT + t_real:nk, :] = jnp.full((nk - FRONT - t_real, tq), INT_MIN, I32)

    def over_keys(x, op):
        pair = jnp.add if op is jnp.sum else jnp.maximum
        group = 8
        x = x.reshape(nk // (group * SUBLANES), group, SUBLANES, x.shape[1])
        while x.shape[1] > 1:
            half = x.shape[1] // 2
            x = pair(x[:, :half], x[:, half:])
        return op(op(x[:, 0], axis=0), axis=0, keepdims=True)

    def count(mask):
        return over_keys(jnp.where(mask, 1, 0).astype(I32), jnp.sum)

    def any_query(mask):
        return jnp.max(jnp.where(mask, 1, 0))

    n_vis = count(key_ref[:nk, :] > jnp.int32(INT_MIN))

    first_bits, later_bits = 20, 4

    def bit_cond(c):
        return (c[0] < 32) & (c[3] > 0)

    def bit_group(c):
        it0, cand, cnt_c, _ = c
        n_bits = jnp.where(it0 == 0, first_bits, later_bits)

        def bit_step(b, cc):
            cand, cnt_c = cc
            trial = cand + jnp.left_shift(jnp.int32(1), 31 - (it0 + b))
            cnt = count(key_ref[:nk, :] >= trial)
            take = cnt >= n_sel
            return jnp.where(take, trial, cand), jnp.where(take, cnt, cnt_c)

        cand, cnt_c = lax.fori_loop(0, n_bits, bit_step, (cand, cnt_c))
        return it0 + n_bits, cand, cnt_c, any_query((cnt_c != n_sel) & (n_vis > n_sel))

    start = (jnp.int32(0), jnp.full((1, tq), INT_MIN, I32), jnp.full((1, tq), nk, I32),
             any_query(n_vis > n_sel))
    _, thr, cnt_ge, _ = lax.while_loop(bit_cond, bit_group, start)
    keys = key_ref[:nk, :]
    bias_ref[:nk, :] = jnp.where(keys >= jnp.maximum(thr, jnp.int32(INT_MIN + 1)), 0.0, NEG_BIG)
    tie_queries = (cnt_ge > n_sel) & (thr > jnp.int32(INT_MIN))

    @pl.when(any_query(tie_queries) > 0)
    def _():
        keys = key_ref[:nk, :]
        eq = keys == thr
        need = n_sel - count(keys > thr)
        nbits = max(1, (nk - 1).bit_length())

        def idx_step(it, cand):
            trial = cand + jnp.left_shift(jnp.int32(1), nbits - 1 - it)
            return jnp.where(count(eq & (kpos < trial)) < need, trial, cand)

        last = lax.fori_loop(0, nbits, idx_step, jnp.zeros((1, tq), I32))
        sel = (keys > thr) | (eq & (kpos <= last))
        bias_ref[:nk, :] = jnp.where(sel & (keys > jnp.int32(INT_MIN)), 0.0, NEG_BIG)

    bias = bias_ref[:nk, :]
    q = q_ref[0]
    k = kv_ref[0, :nk, :A_HEAD_DIM]
    qs = jnp.concatenate([q[:, h * A_HEAD_DIM:(h + 1) * A_HEAD_DIM] for h in range(A_HEADS)], axis=0)
    lg = lax.dot_general(k, qs, nt, preferred_element_type=F32) * (A_HEAD_DIM ** -0.5)
    lg = lg + jnp.concatenate([bias] * A_HEADS, axis=1)
    m = over_keys(lg, jnp.max)
    p = jnp.exp(lg - m)
    l = over_keys(p, jnp.sum)
    ot = jnp.dot(vt_ref[:, :nk], p.astype(BF16), preferred_element_type=F32) / l
    for h in range(A_HEADS):
        o_ref[0, :, h * A_HEAD_DIM:(h + 1) * A_HEAD_DIM] = ot[:, h * tq:(h + 1) * tq].T.astype(BF16)


DSA_WIDTH_STEP = 2


def _dsa_kernel(q_ref, qi_ref, wi_ref, ki_ref, kv_ref, o_ref, key_ref, bias_ref, vt_ref, *, n_sel, t_real):
    i = pl.program_id(1)
    tq = q_ref.shape[1]
    tp = ki_ref.shape[1]
    n_tiles = tp // tq

    @pl.when(i == 0)
    def _():
        vt_ref[...] = kv_ref[0, :, A_HEAD_DIM:].astype(F32).T.astype(BF16)

    lo = 0
    while lo < n_tiles:
        hi = min(lo + DSA_WIDTH_STEP, n_tiles)

        @pl.when((i >= lo) & (i < hi))
        def _(hi=hi):
            _dsa_body(hi * tq, q_ref, qi_ref, wi_ref, ki_ref, kv_ref, o_ref, key_ref, bias_ref, vt_ref,
                      n_sel, t_real)

        lo = hi


def _dsa(q, qi, wi, ki, kv, t_real):
    B, TP, _ = q.shape
    tq = TIME_TILE
    n_sel = min(IDX_TOPK_CAP, t_real // 4)
    kern = functools.partial(_dsa_kernel, n_sel=n_sel, t_real=t_real)
    qspec = lambda n: pl.BlockSpec((1, tq, n), lambda b, i: (b, i, 0))
    kspec = lambda n: pl.BlockSpec((1, TP, n), lambda b, i: (b, 0, 0))
    return pl.pallas_call(
        kern,
        grid=(B, TP // tq),
        in_specs=[qspec(A_OUT), qspec(IDX_HEADS * LANES), qspec(LANES), kspec(LANES), kspec(2 * A_HEAD_DIM)],
        out_specs=qspec(A_OUT),
        out_shape=jax.ShapeDtypeStruct((B, TP, A_OUT), BF16),
        scratch_shapes=[pltpu.VMEM((TP, tq), I32), pltpu.VMEM((TP, tq), F32), pltpu.VMEM((A_HEAD_DIM, TP), BF16)],
        compiler_params=_cparams("parallel", "arbitrary"),
        name="dsa_attention",
    )(q, qi, wi, ki, kv)


def _shift_rows(x, s, fill):
    rows = lax.broadcasted_iota(I32, x.shape, 0)
    return jnp.where(rows >= s, pltpu.roll(x, s, 0), fill)


def _causal_conv(x, win_ref, cw):
    tt = x.shape[0]
    win_ref[0:SUBLANES, :] = win_ref[tt:tt + SUBLANES, :]
    win_ref[SUBLANES:, :] = x
    y = cw[CONV_WIDTH - 1:CONV_WIDTH, :] * x
    for t in range(CONV_WIDTH - 1):
        off = SUBLANES - (CONV_WIDTH - 1) + t
        y = y + cw[t:t + 1, :] * win_ref[off:off + tt, :]
    return y


def _griffin_kernel(xb_ref, gate_ref, cw_ref, cb_ref, wg_ref, bg_ref, lam_ref, o_ref, tail_ref, car_ref, *, t_real):
    j = pl.program_id(1)
    tt = xb_ref.shape[1]

    @pl.when(j == 0)
    def _():
        tail_ref[...] = jnp.zeros_like(tail_ref)
        car_ref[...] = jnp.zeros_like(car_ref)

    xr = _causal_conv(xb_ref[0], tail_ref, cw_ref[...]) + cb_ref[...]
    g = jnp.dot(xr.astype(BF16), wg_ref[...], preferred_element_type=F32) + bg_ref[...]
    r = jax.nn.sigmoid(g[:, :RG_WIDTH])
    ig = jax.nn.sigmoid(g[:, RG_WIDTH:])
    log_a = -RG_C * r * _softplus(-lam_ref[...])
    a = jnp.exp(log_a)
    th = jnp.tanh(log_a)
    u = jnp.sqrt(-2.0 * th / (1.0 - th)) * (ig * xr)
    p = j * tt + lax.broadcasted_iota(I32, xr.shape, 0)
    u = jnp.where(_valid_rows(p, t_real), u, 0.0)
    s = 1
    while s < tt:
        u = u + a * _shift_rows(u, s, 0.0)
        a = a * _shift_rows(a, s, 1.0)
        s *= 2
    hcur = u + a * car_ref[0:1, :]
    car_ref[...] = jnp.broadcast_to(hcur[tt - 1:tt, :], car_ref.shape)
    o_ref[0] = hcur * jax.nn.gelu(gate_ref[0])


def _block_diag(w):
    nb, bs, _ = w.shape
    eye = jnp.eye(nb, dtype=w.dtype)
    return (w[:, :, None, :] * eye[:, None, :, None]).reshape(nb * bs, nb * bs)


def _griffin(xb, gate, conv_w, conv_b, rg_wa, rg_ba, rg_wx, rg_bx, rg_lambda, t_real):
    B, TP, R = xb.shape
    tt = TIME_TILE
    cw = jnp.pad(conv_w.astype(F32), ((0, SUBLANES - CONV_WIDTH), (0, 0)))
    cb = conv_b.reshape(1, R).astype(F32)
    wg = jnp.concatenate([_block_diag(rg_wa), _block_diag(rg_wx)], axis=1).astype(BF16)
    bg = jnp.concatenate([rg_ba, rg_bx]).reshape(1, 2 * R).astype(F32)
    lam = rg_lambda.reshape(1, R).astype(F32)
    tspec = pl.BlockSpec((1, tt, R), lambda b, j: (b, j, 0))
    ws = (cw, cb, wg, bg, lam)
    return pl.pallas_call(
        functools.partial(_griffin_kernel, t_real=t_real),
        grid=(B, TP // tt),
        in_specs=[tspec, tspec] + [_const_spec(w.shape) for w in ws],
        out_specs=tspec,
        out_shape=jax.ShapeDtypeStruct((B, TP, R), F32),
        scratch_shapes=[pltpu.VMEM((SUBLANES + tt, R), F32), pltpu.VMEM((SUBLANES, R), F32)],
        compiler_params=_cparams("parallel", "arbitrary"),
        name="griffin_rglru",
    )(xb, gate, *ws)


def _out_ln_kernel(*refs, n_in, t_real):
    h_ref = refs[0]
    a_refs = refs[1:1 + n_in]
    w_refs = refs[1 + n_in:1 + 2 * n_in]
    g_ref, b_ref, o_ref = refs[1 + 2 * n_in:]
    j = pl.program_id(1)
    tm = h_ref.shape[1]
    n_chunks = 4 if tm % (4 * SUBLANES) == 0 else 1
    tc = tm // n_chunks
    mixes = []
    for c in range(n_chunks):
        rs = slice(c * tc, (c + 1) * tc)
        mix = None
        for a_ref, w_ref in zip(a_refs, w_refs):
            d = jnp.dot(a_ref[0, rs, :].astype(BF16), w_ref[...], preferred_element_type=F32)
            mix = d if mix is None else mix + d
        mixes.append(mix)
    for c in range(n_chunks):
        rs = slice(c * tc, (c + 1) * tc)
        out = _layer_norm_rows(DN_ALPHA * h_ref[0, rs, :] + mixes[c], g_ref[...], b_ref[...])
        p = j * tm + c * tc + lax.broadcasted_iota(I32, out.shape, 0)
        o_ref[0, rs, :] = jnp.where(_valid_rows(p, t_real), out, 0.0)


def _out_ln(h, acts, ws, ln_g, ln_b, t_real):
    B, TP, D = h.shape
    tm = _row_tile(TP)
    ws = [w.astype(BF16) for w in ws]
    g = ln_g.reshape(1, D).astype(F32)
    b = ln_b.reshape(1, D).astype(F32)
    tspec = lambda n: pl.BlockSpec((1, tm, n), lambda bb, j: (bb, j, 0))
    return pl.pallas_call(
        functools.partial(_out_ln_kernel, n_in=len(acts), t_real=t_real),
        grid=(B, TP // tm),
        in_specs=[tspec(D)] + [tspec(a.shape[-1]) for a in acts] + [_const_spec(w.shape) for w in ws]
        + [_const_spec(g.shape), _const_spec(b.shape)],
        out_specs=tspec(D),
        out_shape=jax.ShapeDtypeStruct((B, TP, D), F32),
        compiler_params=_cparams("parallel", "parallel"),
        name="out_proj_ln",
    )(h, *acts, *ws, g, b)


E_LANE0, G_LANE0 = 0, N_EXPERTS
R_E1, R_E2, R_RANK1, R_RANK2, R_G1, R_G2 = range(6)


def _router_kernel(h_ref, w_ref, b_ref, rec_ref, rect_ref, cnt_ref, car_ref):
    t = pl.program_id(0)

    @pl.when(t == 0)
    def _():
        car_ref[...] = jnp.zeros_like(car_ref)

    tm = h_ref.shape[0]
    h = h_ref[...]
    h_hi = h.astype(BF16)
    h_lo = (h - h_hi.astype(F32)).astype(BF16)
    both = jnp.dot(h_hi, w_ref[...], preferred_element_type=F32)
    lg = (both[:, :LANES] + both[:, LANES:]
          + jnp.dot(h_lo, w_ref[:, :LANES], preferred_element_type=F32) + b_ref[...])
    lane = lax.broadcasted_iota(I32, lg.shape, 1)
    neg_inf = -jnp.inf

    def first_argmax(x):
        m = jnp.max(x, axis=1, keepdims=True)
        return m, jnp.min(jnp.where(x == m, lane, jnp.int32(2 ** 30)), axis=1, keepdims=True)

    glog = jnp.where((lane >= G_LANE0) & (lane < G_LANE0 + MOE_GROUPS), lg, neg_inf)
    gmax, glane = first_argmax(glog)
    grp = glane - G_LANE0
    grp_gate = 1.0 / jnp.sum(jnp.exp(glog - gmax), axis=1, keepdims=True)
    in_grp = (lane >= grp * MOE_PER_GROUP) & (lane < (grp + 1) * MOE_PER_GROUP)
    elog = jnp.where(in_grp, lg, neg_inf)
    v1, e1 = first_argmax(elog)
    v2, e2 = first_argmax(jnp.where(lane == e1, neg_inf, elog))
    ex = jnp.exp(v2 - v1)
    g1 = grp_gate / (1.0 + ex)
    g2 = grp_gate * ex / (1.0 + ex)

    oh1 = jnp.where(lane == e1, 1.0, 0.0)
    oh2 = jnp.where(lane == e2, 1.0, 0.0)
    rows = lax.broadcasted_iota(I32, (tm, tm), 0)
    cols = lax.broadcasted_iota(I32, (tm, tm), 1)
    ltri = jnp.where(cols < rows, 1.0, 0.0).astype(BF16)
    p1 = jnp.dot(ltri, oh1.astype(BF16), preferred_element_type=F32)
    p2 = jnp.dot(ltri, oh2.astype(BF16), preferred_element_type=F32)
    car = car_ref[0:1, :]
    c1 = jnp.sum(oh1, axis=0, keepdims=True)
    c2 = jnp.sum(oh2, axis=0, keepdims=True)
    rank1 = jnp.sum(oh1 * (car + p1), axis=1, keepdims=True)
    rank2 = jnp.sum(oh2 * (car + c1 + p2), axis=1, keepdims=True)
    car = car + c1 + c2
    car_ref[...] = jnp.broadcast_to(car, car_ref.shape)
    cnt_ref[...] = jnp.broadcast_to(car, cnt_ref.shape)

    rec = jnp.zeros(lg.shape, F32)
    for ln, val in ((R_E1, e1.astype(F32)), (R_E2, e2.astype(F32)), (R_RANK1, rank1), (R_RANK2, rank2),
                    (R_G1, g1), (R_G2, g2)):
        rec = jnp.where(lane == ln, val, rec)
    rec_ref[...] = rec
    rect_ref[0] = rec.T[:SUBLANES, :]


def _moe_route(hf, group_w, group_b, expert_w, expert_b):
    N, D = hf.shape
    tm = 2 * MOE_BLOCK if N % (2 * MOE_BLOCK) == 0 else MOE_BLOCK
    nt = N // tm
    w = jnp.zeros((D, LANES), F32)
    w = w.at[:, E_LANE0:E_LANE0 + N_EXPERTS].set(expert_w).at[:, G_LANE0:G_LANE0 + MOE_GROUPS].set(group_w)
    b = jnp.zeros((1, LANES), F32)
    b = b.at[0, E_LANE0:E_LANE0 + N_EXPERTS].set(expert_b).at[0, G_LANE0:G_LANE0 + MOE_GROUPS].set(group_b)
    w_hi = w.astype(BF16)
    w = jnp.concatenate([w_hi, (w - w_hi.astype(F32)).astype(BF16)], axis=1)
    return pl.pallas_call(
        _router_kernel,
        grid=(nt,),
        in_specs=[pl.BlockSpec((tm, D), lambda t: (t, 0)), _const_spec(w.shape), _const_spec(b.shape)],
        out_specs=[pl.BlockSpec((tm, LANES), lambda t: (t, 0)),
                   pl.BlockSpec((1, SUBLANES, tm), lambda t: (t, 0, 0)),
                   _const_spec((SUBLANES, LANES))],
        out_shape=[jax.ShapeDtypeStruct((N, LANES), F32),
                   jax.ShapeDtypeStruct((nt, SUBLANES, tm), F32),
                   jax.ShapeDtypeStruct((SUBLANES, LANES), F32)],
        scratch_shapes=[pltpu.VMEM((SUBLANES, LANES), F32)],
        compiler_params=_cparams("arbitrary"),
        name="moe_router",
    )(hf, w, b)


def _row_copy(src, si, dst, di, sem):
    return pltpu.make_async_copy(src.at[pl.ds(si, 1)], dst.at[pl.ds(di, 1)], sem)


def _route_copy(idx_hbm, idx_smem, sems, t):
    return pltpu.make_async_copy(idx_hbm.at[t], idx_smem.at[t % 2], sems.at[t % 2])


def _slot(idx_smem, t, k, r):
    return idx_smem[t % 2, k * MOE_BLOCK + r]


def _slots_kernel(rect_ref, ps_ref, o_ref):
    rec = rect_ref[0]
    erow = lax.broadcasted_iota(I32, (N_EXPERTS, rec.shape[1]), 0).astype(F32)
    rows = []
    for k in range(MOE_TOPK):
        e = rec[R_E1 + k:R_E1 + k + 1, :]
        base = jnp.sum(jnp.where(erow == e, ps_ref[...], 0.0), axis=0, keepdims=True)
        rows.append(rec[R_RANK1 + k:R_RANK1 + k + 1, :] + base)
    rows.append(jnp.zeros((SUBLANES - MOE_TOPK, rec.shape[1]), F32))
    o_ref[0] = jnp.concatenate(rows, axis=0).astype(I32)


def _moe_slots(rect, pstart):
    nt, _, tm = rect.shape
    ps = pstart.astype(F32).reshape(N_EXPERTS, 1)
    return pl.pallas_call(
        _slots_kernel,
        grid=(nt,),
        in_specs=[pl.BlockSpec((1, SUBLANES, tm), lambda t: (t, 0, 0)), _const_spec(ps.shape)],
        out_specs=pl.BlockSpec((1, SUBLANES, tm), lambda t: (t, 0, 0)),
        out_shape=jax.ShapeDtypeStruct((nt, SUBLANES, tm), I32),
        compiler_params=_cparams("parallel"),
        name="moe_slots",
    )(rect, ps)


N_HBUF = 3


def _dispatch_kernel(idx_hbm, h_hbm, xs_in, xs_hbm, idx_smem, hbuf, isems, hsems, ssems):
    del xs_in
    t = pl.program_id(0)
    nt = pl.num_programs(0)
    tm = MOE_BLOCK

    def tile_copy(u):
        return pltpu.make_async_copy(h_hbm.at[pl.ds(u * tm, tm)], hbuf.at[u % N_HBUF], hsems.at[u % N_HBUF])

    def wait_scatters(u):
        for _ in range(tm * MOE_TOPK):
            _row_copy(hbuf.at[0], 0, xs_hbm, 0, ssems.at[u % 2]).wait()

    @pl.when(t == 0)
    def _():
        _route_copy(idx_hbm, idx_smem, isems, t).start()
        tile_copy(t).start()

    _route_copy(idx_hbm, idx_smem, isems, t).wait()
    tile_copy(t).wait()

    @pl.when(t + 1 < nt)
    def _():
        _route_copy(idx_hbm, idx_smem, isems, t + 1).start()
        tile_copy(t + 1).start()

    src = hbuf.at[t % N_HBUF]
    for r in range(tm):
        for k in range(MOE_TOPK):
            _row_copy(src, r, xs_hbm, _slot(idx_smem, t, k, r), ssems.at[t % 2]).start()

    @pl.when(t > 0)
    def _():
        wait_scatters(t - 1)

    @pl.when(t == nt - 1)
    def _():
        wait_scatters(t)


def _moe_dispatch(hf, idx, n_slots):
    N, D = hf.shape
    nt = N // MOE_BLOCK
    xs0 = jnp.zeros((n_slots, D), F32)
    return pl.pallas_call(
        _dispatch_kernel,
        grid=(nt,),
        in_specs=[pl.BlockSpec(memory_space=pl.ANY)] * 3,
        out_specs=pl.BlockSpec(memory_space=pl.ANY),
        scratch_shapes=[pltpu.SMEM((2, MOE_TOPK * MOE_BLOCK), I32),
                        pltpu.VMEM((N_HBUF, MOE_BLOCK, D), F32),
                        pltpu.SemaphoreType.DMA((2,)), pltpu.SemaphoreType.DMA((N_HBUF,)),
                        pltpu.SemaphoreType.DMA((2,))],
        out_shape=jax.ShapeDtypeStruct((n_slots, D), F32),
        input_output_aliases={2: 0},
        compiler_params=_cparams("arbitrary"),
        name="moe_dispatch",
    )(idx, hf, xs0)


def _ffn_kernel(be_ref, nb_ref, x_ref, wg_ref, wu_ref, wd_ref, o_ref, wg16, wu16, wd16):
    b = pl.program_id(0)

    @pl.when((b == 0) | (be_ref[b] != be_ref[jnp.maximum(b - 1, 0)]))
    def _():
        wg16[...] = wg_ref[...].astype(BF16)
        wu16[...] = wu_ref[...].astype(BF16)
        wd16[...] = wd_ref[...].astype(BF16)

    @pl.when(b < nb_ref[0])
    def _():
        n_chunks = x_ref.shape[0] // TIME_TILE
        rows = [slice(c * TIME_TILE, (c + 1) * TIME_TILE) for c in range(n_chunks)]
        xs = [x_ref[r, :].astype(BF16) for r in rows]
        hgs = [jnp.dot(x, wg16[...], preferred_element_type=F32) for x in xs]
        hus = [jnp.dot(x, wu16[...], preferred_element_type=F32) for x in xs]
        hids = [((hg * jax.nn.sigmoid(hg)) * hu).astype(BF16) for hg, hu in zip(hgs, hus)]
        for r, hid in zip(rows, hids):
            o_ref[r, :] = jnp.dot(hid, wd16[...], preferred_element_type=F32)

    @pl.when(pl.program_id(0) >= nb_ref[0])
    def _():
        o_ref[...] = jnp.zeros_like(o_ref)


def _moe_ffn(xs, block_expert, nb_used, w_gate, w_up, w_down):
    P, D = xs.shape
    nb = P // FFN_BLOCK
    wspec = lambda s: pl.BlockSpec((None,) + s, lambda b, be, nbu: (be[b], 0, 0))
    return pl.pallas_call(
        _ffn_kernel,
        grid_spec=pltpu.PrefetchScalarGridSpec(
            num_scalar_prefetch=2,
            grid=(nb,),
            in_specs=[pl.BlockSpec((FFN_BLOCK, D), lambda b, be, nbu: (b, 0)),
                      wspec((D, EXPERT_FF)), wspec((D, EXPERT_FF)), wspec((EXPERT_FF, D))],
            out_specs=pl.BlockSpec((FFN_BLOCK, D), lambda b, be, nbu: (b, 0)),
            scratch_shapes=[pltpu.VMEM((D, EXPERT_FF), BF16), pltpu.VMEM((D, EXPERT_FF), BF16),
                            pltpu.VMEM((EXPERT_FF, D), BF16)],
        ),
        out_shape=jax.ShapeDtypeStruct((P, D), F32),
        compiler_params=_cparams("arbitrary"),
        name="moe_expert_ffn",
    )(block_expert, nb_used, xs, w_gate, w_up, w_down)


def _combine_kernel(idx_hbm, ys_hbm, h_ref, rec_ref, g_ref, b_ref, o_ref, idx_smem, ybuf, isems, gsems,
                    *, tp, t_real):
    t = pl.program_id(0)
    nt = pl.num_programs(0)
    tm = MOE_BLOCK

    def issue_gathers(u):
        dst = ybuf.at[u % 2]
        for r in range(tm):
            for k in range(MOE_TOPK):
                _row_copy(ys_hbm, _slot(idx_smem, u, k, r), dst.at[k], r, gsems.at[u % 2]).start()

    @pl.when(t == 0)
    def _():
        _route_copy(idx_hbm, idx_smem, isems, t).start()
        _route_copy(idx_hbm, idx_smem, isems, t).wait()
        issue_gathers(t)

        @pl.when(nt > 1)
        def _():
            _route_copy(idx_hbm, idx_smem, isems, t + 1).start()

    @pl.when(t + 1 < nt)
    def _():
        _route_copy(idx_hbm, idx_smem, isems, t + 1).wait()
        issue_gathers(t + 1)

    @pl.when(t + 2 < nt)
    def _():
        _route_copy(idx_hbm, idx_smem, isems, t + 2).start()

    for _ in range(tm * MOE_TOPK):
        _row_copy(ys_hbm, 0, ybuf.at[0, 0], 0, gsems.at[t % 2]).wait()
    b = t % 2
    rec = rec_ref[...]
    y = DN_ALPHA * h_ref[...] + rec[:, R_G1:R_G1 + 1] * ybuf[b, 0] + rec[:, R_G2:R_G2 + 1] * ybuf[b, 1]
    out = _layer_norm_rows(y, g_ref[...], b_ref[...])
    n = (t * tm + lax.broadcasted_iota(I32, out.shape, 0)).astype(F32)
    p = n - jnp.floor((n + 0.5) / tp) * tp
    o_ref[...] = jnp.where((p >= FRONT) & (p < FRONT + t_real), out, 0.0)


def _moe_combine(hf, ys, idx, rec, ln_g, ln_b, tp, t_real):
    N, D = hf.shape
    nt = N // MOE_BLOCK
    g = ln_g.reshape(1, D).astype(F32)
    b = ln_b.reshape(1, D).astype(F32)
    anyspec = pl.BlockSpec(memory_space=pl.ANY)
    return pl.pallas_call(
        functools.partial(_combine_kernel, tp=tp, t_real=t_real),
        grid=(nt,),
        in_specs=[anyspec, anyspec,
                  pl.BlockSpec((MOE_BLOCK, D), lambda t: (t, 0)),
                  pl.BlockSpec((MOE_BLOCK, LANES), lambda t: (t, 0)),
                  _const_spec((1, D)), _const_spec((1, D))],
        out_specs=pl.BlockSpec((MOE_BLOCK, D), lambda t: (t, 0)),
        scratch_shapes=[pltpu.SMEM((2, MOE_TOPK * MOE_BLOCK), I32),
                        pltpu.VMEM((2, MOE_TOPK, MOE_BLOCK, D), F32),
                        pltpu.SemaphoreType.DMA((2,)), pltpu.SemaphoreType.DMA((2,))],
        out_shape=jax.ShapeDtypeStruct((N, D), F32),
        compiler_params=_cparams("arbitrary"),
        name="moe_combine_ln",
    )(idx, ys, hf, rec, g, b)


def _moe_layer(h, group_w, group_b, expert_w, expert_b, w_gate, w_up, w_down, ln_g, ln_b, t_real):
    B, TP, D = h.shape
    hf = h.reshape(B * TP, D)
    N = B * TP
    rec, rect, cnt = _moe_route(hf, group_w, group_b, expert_w, expert_b)
    counts = cnt[0, :N_EXPERTS].astype(I32)
    padded = (counts + FFN_BLOCK - 1) // FFN_BLOCK * FFN_BLOCK
    pend = jnp.cumsum(padded)
    pstart = (pend - padded).astype(I32)
    n_blocks = -(-(N * MOE_TOPK) // FFN_BLOCK) + N_EXPERTS
    block_start = jnp.arange(n_blocks, dtype=I32) * FFN_BLOCK
    block_expert = jnp.minimum(jnp.sum((pend[None, :] <= block_start[:, None]).astype(I32), axis=1), N_EXPERTS - 1)
    nb_used = (pend[-1:] // FFN_BLOCK).astype(I32)
    n_slots = n_blocks * FFN_BLOCK
    slots = _moe_slots(rect, pstart)
    idx = slots[:, :MOE_TOPK, :].reshape(slots.shape[0], MOE_TOPK, slots.shape[2] // MOE_BLOCK, MOE_BLOCK)
    idx = idx.transpose(0, 2, 1, 3).reshape(N // MOE_BLOCK, MOE_TOPK * MOE_BLOCK)
    xs = _moe_dispatch(hf, idx, n_slots)
    ys = _moe_ffn(xs, block_expert, nb_used, w_gate, w_up, w_down)
    out = _moe_combine(hf, ys, idx, rec, ln_g, ln_b, TP, t_real)
    return out.reshape(B, TP, D)


def _even_layer(h, w_in, kv_norm, w_uk, w_uv, conv_w, conv_b, rg_wa, rg_ba, rg_wx, rg_bx, rg_lambda, w_out,
                ln_g, ln_b, t_real):
    q, qi, ki, wi, gate, xb, kv = _even_in_proj(h, w_in, kv_norm, w_uk, w_uv)
    attn = _dsa(q, qi, wi, ki, kv, t_real)
    rec = _griffin(xb, gate, conv_w, conv_b, rg_wa, rg_ba, rg_wx, rg_bx, rg_lambda, t_real)
    return _out_ln(h, [attn, rec], [w_out[:A_OUT], w_out[A_OUT:]], ln_g, ln_b, t_real)


def _odd_in_kernel(h_ref, wq_ref, wk_ref, wv_ref, wz_ref, wa_ref, wb_ref, q_ref, k_ref, v_ref, z_ref, a_ref, b_ref):
    a = h_ref[0].astype(BF16)
    for w_ref, o_ref in ((wq_ref, q_ref), (wk_ref, k_ref), (wv_ref, v_ref), (wz_ref, z_ref), (wa_ref, a_ref),
                         (wb_ref, b_ref)):
        o_ref[0] = jnp.dot(a, w_ref[...], preferred_element_type=F32).astype(o_ref.dtype)


def _odd_in_proj(h, w_in):
    B, TP, D = h.shape
    tm = _row_tile(TP)
    wb = w_in.astype(BF16)
    o = [0, GDN_QK_WIDTH, 2 * GDN_QK_WIDTH, 2 * GDN_QK_WIDTH + GDN_V_WIDTH, 2 * GDN_QK_WIDTH + 2 * GDN_V_WIDTH]
    o += [o[-1] + GDN_V_HEADS, o[-1] + 2 * GDN_V_HEADS]
    ws = [wb[:, o[i]:o[i + 1]] for i in range(4)]
    ws += [jnp.pad(wb[:, o[i]:o[i + 1]], ((0, 0), (0, LANES - GDN_V_HEADS))) for i in (4, 5)]
    widths = [GDN_QK_WIDTH, GDN_QK_WIDTH, GDN_V_WIDTH, GDN_V_WIDTH, LANES, LANES]
    return pl.pallas_call(
        _odd_in_kernel,
        grid=(B, TP // tm),
        in_specs=[pl.BlockSpec((1, tm, D), lambda b, j: (b, j, 0))] + [_const_spec(w.shape) for w in ws],
        out_specs=[pl.BlockSpec((1, tm, n), lambda b, j: (b, j, 0)) for n in widths],
        out_shape=[jax.ShapeDtypeStruct((B, TP, n), BF16 if n > LANES else F32) for n in widths],
        compiler_params=_cparams("parallel", "parallel"),
        name="odd_in_proj",
    )(h, *ws)


def _conv_silu(x, win_ref, cw):
    y = _causal_conv(x, win_ref, cw)
    return y * jax.nn.sigmoid(y)


def _gdn_prep_kernel(q_ref, k_ref, v_ref, a_ref, b_ref, cwq_ref, cwk_ref, cwv_ref, alog_ref, dtb_ref,
                     u_ref, w_ref, qg_ref, kg_ref, at_ref, e_ref,
                     tq_ref, tk_ref, tv_ref, qs_ref, ks_ref, vs_ref, *, t_real):
    j = pl.program_id(1)
    tt = q_ref.shape[1]
    hd = GDN_HEAD_DIM
    half = CHUNK

    @pl.when(j == 0)
    def _():
        tq_ref[...] = jnp.zeros_like(tq_ref)
        tk_ref[...] = jnp.zeros_like(tk_ref)
        tv_ref[...] = jnp.zeros_like(tv_ref)

    rows1 = j * tt + lax.broadcasted_iota(I32, (tt, 1), 0)
    valid = _valid_rows(rows1, t_real)
    q = jnp.where(valid, _conv_silu(q_ref[0].astype(F32), tq_ref, cwq_ref[...]), 0.0)
    k = jnp.where(valid, _conv_silu(k_ref[0].astype(F32), tk_ref, cwk_ref[...]), 0.0)
    vs_ref[...] = jnp.where(valid, _conv_silu(v_ref[0].astype(F32), tv_ref, cwv_ref[...]), 0.0)
    for h in range(GDN_K_HEADS):
        sl = slice(h * hd, (h + 1) * hd)
        qh, kh = q[:, sl], k[:, sl]
        qs_ref[:, sl] = qh * lax.rsqrt(jnp.sum(qh * qh, axis=1, keepdims=True) + 1e-6) * (hd ** -0.5)
        ks_ref[:, sl] = kh * lax.rsqrt(jnp.sum(kh * kh, axis=1, keepdims=True) + 1e-6)

    lane = lax.broadcasted_iota(I32, (tt, LANES), 1)
    rowi = lax.broadcasted_iota(I32, (tt, LANES), 0)
    live = valid & (lane < GDN_V_HEADS)
    g = jnp.where(live, -jnp.exp(alog_ref[...]) * _softplus(a_ref[0] + dtb_ref[...]), 0.0)
    beta = jnp.where(live, jax.nn.sigmoid(b_ref[0]), 0.0)
    gc = g
    s = 1
    while s < CHUNK:
        gc = gc + jnp.where((rowi % CHUNK) >= s, pltpu.roll(gc, s, 0), 0.0)
        s *= 2
    gc_next = pltpu.roll(gc, LANES - 1, 1)
    n_ch = tt // CHUNK
    e_rows = []
    for c in range(n_ch):
        e_rows.append(jnp.broadcast_to(jnp.exp(gc[(c + 1) * CHUNK - 1:(c + 1) * CHUNK, :]), (CHUNK, LANES)))
    e_ref[0] = jnp.concatenate(e_rows, axis=0)

    nt = (((1,), (1,)), ((), ()))
    l2 = lax.broadcasted_iota(I32, (CHUNK, 2 * half), 1)
    ii = lax.broadcasted_iota(I32, (CHUNK, 2 * half), 0)
    jj = l2 % half
    left = l2 < half
    r2 = lax.broadcasted_iota(I32, (2 * half, 2 * half), 0)
    c2 = lax.broadcasted_iota(I32, (2 * half, 2 * half), 1)
    eye2 = jnp.where(r2 == c2, 1.0, 0.0)
    bdot = lambda x, y: jnp.dot(x.astype(BF16), y.astype(BF16), preferred_element_type=F32)
    pair_group = GDN_K_HEADS
    for c, g0 in [(c, g0) for c in range(n_ch) for g0 in range(0, GDN_K_HEADS, pair_group)]:
        rs = slice(c * CHUNK, (c + 1) * CHUNK)
        gc_c, beta_c = gc[rs], beta[rs]
        gt = jnp.concatenate([gc_c, gc_next[rs]], axis=0).T
        glast = gc_c[CHUNK - 1:CHUNK, :]
        pairs = range(g0, g0 + pair_group)
        ms, tinvs, rhss = [], [], []
        for pr in pairs:
            ha, hb = 2 * pr, 2 * pr + 1
            ksl = slice(pr * hd, (pr + 1) * hd)
            osl = slice(ha * hd, (hb + 1) * hd)
            kc = ks_ref[rs, ksl]
            qc = qs_ref[rs, ksl]
            kc16 = kc.astype(BF16)
            k2 = jnp.concatenate([kc16, kc16], axis=0)
            kk2 = lax.dot_general(kc16, k2, nt, preferred_element_type=F32)
            qk2 = lax.dot_general(qc.astype(BF16), k2, nt, preferred_element_type=F32)
            gca, gcb = gc_c[:, ha:ha + 1], gc_c[:, hb:hb + 1]
            ba, bb = beta_c[:, ha:ha + 1], beta_c[:, hb:hb + 1]
            gcol2 = jnp.where(left, gca, gcb)
            bcol2 = jnp.where(left, ba, bb)
            decay2 = jnp.exp(jnp.where(ii >= jj, gcol2 - gt[ha:ha + 1, :], -jnp.inf))
            m2 = jnp.where(ii > jj, bcol2 * kk2 * decay2, 0.0)
            at_ref[0, rs, pr * 2 * half:(pr + 1) * 2 * half] = (qk2 * decay2).astype(BF16)
            m = jnp.concatenate([jnp.where(left, m2, 0.0), jnp.where(left, 0.0, m2)], axis=0)
            ms.append(m)
            tinvs.append(eye2 - m)
            ega, egb = jnp.exp(gca), jnp.exp(gcb)
            va = vs_ref[rs, ha * hd:(ha + 1) * hd]
            vb = vs_ref[rs, hb * hd:(hb + 1) * hd]
            rhss.append(jnp.concatenate([jnp.concatenate([va * ba, kc * (ba * ega)], axis=1),
                                         jnp.concatenate([vb * bb, kc * (bb * egb)], axis=1)],
                                        axis=0).astype(BF16))
            qg_ref[0, rs, osl] = jnp.concatenate([qc * ega, qc * egb], axis=1).astype(BF16)
            kg_ref[0, rs, osl] = jnp.concatenate([kc * jnp.exp(glast[:, ha:ha + 1] - gca),
                                                  kc * jnp.exp(glast[:, hb:hb + 1] - gcb)], axis=1).astype(BF16)
        pws = [bdot(m, m) for m in ms]
        n_fac = CHUNK.bit_length() - 2
        for f in range(n_fac):
            if f + 1 < n_fac:
                both = [bdot(jnp.concatenate([t, p], axis=0), p) for t, p in zip(tinvs, pws)]
                tinvs = [t + r[:2 * half] for t, r in zip(tinvs, both)]
                pws = [r[2 * half:] for r in both]
            else:
                tinvs = [t + bdot(t, p) for t, p in zip(tinvs, pws)]
        sols = [jnp.dot(t.astype(BF16), r, preferred_element_type=F32) for t, r in zip(tinvs, rhss)]
        for pr, sol in zip(pairs, sols):
            osl = slice(2 * pr * hd, (2 * pr + 2) * hd)
            u_ref[0, rs, osl] = jnp.concatenate([sol[:CHUNK, :hd], sol[CHUNK:, :hd]], axis=1)
            w_ref[0, rs, osl] = jnp.concatenate([sol[:CHUNK, hd:], sol[CHUNK:, hd:]], axis=1).astype(BF16)


def _gdn_prep(q, k, v, a, b, conv_w, a_log, dt_bias, t_real):
    B, TP, _ = q.shape
    tt = TIME_TILE
    cw = jnp.pad(conv_w.astype(F32), ((0, SUBLANES - CONV_WIDTH), (0, 0)))
    cwq, cwk, cwv = cw[:, :GDN_QK_WIDTH], cw[:, GDN_QK_WIDTH:2 * GDN_QK_WIDTH], cw[:, 2 * GDN_QK_WIDTH:]
    alog = jnp.pad(a_log.astype(F32), (0, LANES - GDN_V_HEADS)).reshape(1, LANES)
    dtb = jnp.pad(dt_bias.astype(F32), (0, LANES - GDN_V_HEADS)).reshape(1, LANES)
    tspec = lambda n: pl.BlockSpec((1, tt, n), lambda bb, j: (bb, j, 0))
    ws = (cwq, cwk, cwv, alog, dtb)
    VW, QW = GDN_V_WIDTH, GDN_QK_WIDTH
    outs = [(VW, F32), (VW, BF16), (VW, BF16), (VW, BF16), (GDN_K_HEADS * 2 * CHUNK, BF16), (LANES, F32)]
    return pl.pallas_call(
        functools.partial(_gdn_prep_kernel, t_real=t_real),
        grid=(B, TP // tt),
        in_specs=[tspec(QW), tspec(QW), tspec(VW), tspec(LANES), tspec(LANES)] + [_const_spec(w.shape) for w in ws],
        out_specs=[tspec(n) for n, _ in outs],
        out_shape=[jax.ShapeDtypeStruct((B, TP, n), dt) for n, dt in outs],
        scratch_shapes=[pltpu.VMEM((SUBLANES + tt, QW), F32), pltpu.VMEM((SUBLANES + tt, QW), F32),
                        pltpu.VMEM((SUBLANES + tt, VW), F32), pltpu.VMEM((tt, QW), F32), pltpu.VMEM((tt, QW), F32),
                        pltpu.VMEM((tt, VW), F32)],
        compiler_params=_cparams("parallel", "arbitrary"),
        name="gdn_prep",
    )(q, k, v, a, b, *ws)


def _gdn_scan_kernel(u_ref, w_ref, qg_ref, kg_ref, at_ref, e_ref, z_ref, on_ref, o_ref, s_ref):
    j = pl.program_id(1)
    tt = u_ref.shape[1]
    hd = GDN_HEAD_DIM

    @pl.when(j == 0)
    def _():
        s_ref[...] = jnp.zeros_like(s_ref)

    lane = lax.broadcasted_iota(I32, (CHUNK, 2 * CHUNK), 1)
    tn = (((0,), (0,)), ((), ()))
    heads = range(GDN_V_HEADS)
    hsl = [slice(h * hd, (h + 1) * hd) for h in heads]
    for c in range(tt // CHUNK):
        rs = slice(c * CHUNK, (c + 1) * CHUNK)
        e_row = e_ref[0, c * CHUNK:c * CHUNK + 1, :]
        ss = [s_ref[h] for h in heads]
        ws = [jnp.dot(jnp.concatenate([w_ref[0, rs, hsl[h]], qg_ref[0, rs, hsl[h]]], axis=0), ss[h].astype(BF16),
                      preferred_element_type=F32) for h in heads]
        v16 = [(u_ref[0, rs, hsl[h]] - ws[h][:CHUNK]).astype(BF16) for h in heads]
        intra = []
        for pr in range(GDN_K_HEADS):
            at2 = at_ref[0, rs, pr * 2 * CHUNK:(pr + 1) * 2 * CHUNK]
            zero = jnp.zeros_like(at2)
            lhs = jnp.concatenate([jnp.where(lane < CHUNK, at2, zero), jnp.where(lane >= CHUNK, at2, zero)], axis=0)
            rhs = jnp.concatenate([v16[2 * pr], v16[2 * pr + 1]], axis=0)
            intra.append(jnp.dot(lhs, rhs, preferred_element_type=F32))
        for h in heads:
            s_ref[h] = ss[h] * e_row[:, h:h + 1] + lax.dot_general(kg_ref[0, rs, hsl[h]], v16[h], tn,
                                                                   preferred_element_type=F32)
        for h in heads:
            o = ws[h][CHUNK:] + intra[h // 2][(h % 2) * CHUNK:(h % 2 + 1) * CHUNK]
            on = o * lax.rsqrt(jnp.mean(o * o, axis=1, keepdims=True) + 1e-6) * on_ref[...]
            z = z_ref[0, rs, hsl[h]].astype(F32)
            o_ref[0, rs, hsl[h]] = (on * (z * jax.nn.sigmoid(z))).astype(BF16)


def _gdn_scan(u, w, qg, kg, at, e, z, o_norm):
    B, TP, VW = u.shape
    tt = TIME_TILE
    tspec = lambda n: pl.BlockSpec((1, tt, n), lambda bb, j: (bb, j, 0))
    on = o_norm.reshape(1, GDN_HEAD_DIM).astype(F32)
    return pl.pallas_call(
        _gdn_scan_kernel,
        grid=(B, TP // tt),
        in_specs=[tspec(VW), tspec(VW), tspec(VW), tspec(VW), tspec(at.shape[-1]), tspec(LANES), tspec(VW),
                  _const_spec(on.shape)],
        out_specs=tspec(VW),
        out_shape=jax.ShapeDtypeStruct((B, TP, VW), BF16),
        scratch_shapes=[pltpu.VMEM((GDN_V_HEADS, GDN_HEAD_DIM, GDN_HEAD_DIM), F32)],
        compiler_params=_cparams("parallel", "arbitrary"),
        name="gdn_scan",
    )(u, w, qg, kg, at, e, z, on)


def _odd_layer(h, w_in, conv_w, a_log, dt_bias, o_norm, w_out, ln_g, ln_b, t_real):
    q, k, v, z, a, b = _odd_in_proj(h, w_in)
    u, w, qg, kg, at, e = _gdn_prep(q, k, v, a, b, conv_w, a_log, dt_bias, t_real)
    gated = _gdn_scan(u, w, qg, kg, at, e, z, o_norm)
    return _out_ln(h, [gated], [w_out], ln_g, ln_b, t_real)


def kernel(x, meta_tokens, even_w_in, even_kv_norm, even_w_uk, even_w_uv, even_conv_w, even_conv_b, even_rg_wa,
           even_rg_ba, even_rg_wx, even_rg_bx, even_rg_lambda, even_w_out, odd_w_in, odd_conv_w, odd_a_log,
           odd_dt_bias, odd_o_norm, odd_w_out, ln_g, ln_b, moe_group_w, moe_group_b, moe_expert_w, moe_expert_b,
           moe_w_gate, moe_w_up, moe_w_down):
    B, S, D = x.shape
    t_real = N_META + S
    tp = -(-(FRONT + t_real) // TIME_TILE) * TIME_TILE
    meta = jnp.broadcast_to(meta_tokens.astype(x.dtype)[None], (B, N_META, D))
    h = jnp.concatenate([jnp.zeros((B, FRONT, D), x.dtype), meta, x,
                         jnp.zeros((B, tp - FRONT - t_real, D), x.dtype)], axis=1)
    for layer in range(DEPTH):
        i = layer // 2
        if layer % 2 == 0:
            h = _even_layer(h, even_w_in[i], even_kv_norm[i], even_w_uk[i], even_w_uv[i], even_conv_w[i],
                            even_conv_b[i], even_rg_wa[i], even_rg_ba[i], even_rg_wx[i], even_rg_bx[i],
                            even_rg_lambda[i], even_w_out[i], ln_g[layer, 0], ln_b[layer, 0], t_real)
        else:
            h = _odd_layer(h, odd_w_in[i], odd_conv_w[i], odd_a_log[i], odd_dt_bias[i], odd_o_norm[i],
                           odd_w_out[i], ln_g[layer, 0], ln_b[layer, 0], t_real)
        h = _moe_layer(h, moe_group_w[layer], moe_group_b[layer], moe_expert_w[layer], moe_expert_b[layer],
                       moe_w_gate[layer], moe_w_up[layer], moe_w_down[layer], ln_g[layer, 1], ln_b[layer, 1], t_real)
    return h[:, FRONT:FRONT + t_real][:, N_META:]
```

```python
import functools

import jax
import jax.numpy as jnp
from jax import lax
from jax.experimental import pallas as pl
from jax.experimental.pallas import tpu as pltpu

F32 = jnp.float32
BF16 = jnp.bfloat16
I32 = jnp.int32

D_MODEL = 1024
DEPTH = 2
N_META = 16
DN_ALPHA = (2 * DEPTH) ** 0.25

A_HEADS = 4
A_HEAD_DIM = 128
A_OUT = A_HEADS * A_HEAD_DIM
A_KV_RANK = 256
IDX_HEADS = 8
IDX_DIM = 64
IDX_TOPK_CAP = 256

RG_WIDTH = 512
RG_BLOCKS = 8
RG_C = 8.0
CONV_WIDTH = 4

GDN_K_HEADS = 8
GDN_V_HEADS = 16
GDN_HEAD_DIM = 128
GDN_QK_WIDTH = GDN_K_HEADS * GDN_HEAD_DIM
GDN_V_WIDTH = GDN_V_HEADS * GDN_HEAD_DIM
CHUNK = 64

MOE_GROUPS = 4
MOE_PER_GROUP = 8
N_EXPERTS = MOE_GROUPS * MOE_PER_GROUP
MOE_TOPK = 2
EXPERT_FF = 512
MOE_BLOCK = 256
FFN_BLOCK = 512

LANES = 128
SUBLANES = 8
TIME_TILE = 128
FRONT = (-N_META) % CHUNK
NEG_BIG = -1e30
INT_MIN = -2 ** 31
VMEM_LIMIT = 56 * 1024 * 1024


def _cparams(*sem, flags=None):
    return pltpu.CompilerParams(dimension_semantics=sem, vmem_limit_bytes=VMEM_LIMIT, flags=flags)


def _const_spec(shape):
    nd = len(shape)
    return pl.BlockSpec(shape, lambda *_: (0,) * nd)


def _row_tile(tp):
    best = SUBLANES
    for t in range(SUBLANES, 641, SUBLANES):
        if tp % t == 0:
            best = t
    return best


def _valid_rows(p, t_real):
    return (p >= FRONT) & (p < FRONT + t_real)


def _layer_norm_rows(y, g, b):
    mu = jnp.mean(y, axis=-1, keepdims=True)
    yc = y - mu
    var = jnp.mean(yc * yc, axis=-1, keepdims=True)
    return yc * lax.rsqrt(var + 1e-5) * g + b


def _softplus(z):
    return jnp.maximum(z, 0.0) + jnp.log(1.0 + jnp.exp(-jnp.abs(z)))


def _even_in_kernel(h_ref, wq_ref, wc_ref, wqi_ref, wki_ref, wwi_ref, wg_ref, wx_ref, kvn_ref, wukv_ref,
                    q_ref, qi_ref, ki_ref, wi_ref, gate_ref, xb_ref, kv_ref):
    a = h_ref[0].astype(BF16)
    dot = functools.partial(jnp.dot, preferred_element_type=F32)
    q_ref[0] = dot(a, wq_ref[...]).astype(BF16)
    qi_ref[0] = dot(a, wqi_ref[...]).astype(BF16)
    ki_ref[0] = dot(a, wki_ref[...]).astype(BF16)
    wi_ref[0] = dot(a, wwi_ref[...]) * (IDX_HEADS ** -0.5) * (IDX_DIM ** -0.5)
    gate_ref[0] = dot(a, wg_ref[...])
    xb_ref[0] = dot(a, wx_ref[...])
    c = dot(a, wc_ref[...])
    latent = c * lax.rsqrt(jnp.mean(c * c, axis=-1, keepdims=True) + 1e-6) * kvn_ref[...]
    kv_ref[0] = dot(latent.astype(BF16), wukv_ref[...]).astype(BF16)


def _even_in_proj(h, w_in, kv_norm, w_uk, w_uv):
    B, TP, D = h.shape
    tm = _row_tile(TP)
    o = [0]
    for p in (A_OUT, A_KV_RANK, IDX_HEADS * IDX_DIM, IDX_DIM, IDX_HEADS, RG_WIDTH, RG_WIDTH):
        o.append(o[-1] + p)
    wb = w_in.astype(BF16)
    wq, wc = wb[:, o[0]:o[1]], wb[:, o[1]:o[2]]
    wqi = wb[:, o[2]:o[3]].reshape(D, IDX_HEADS, IDX_DIM)
    wqi = jnp.pad(wqi, ((0, 0), (0, 0), (0, LANES - IDX_DIM))).reshape(D, IDX_HEADS * LANES)
    wki = jnp.pad(wb[:, o[3]:o[4]], ((0, 0), (0, LANES - IDX_DIM)))
    wwi = jnp.pad(wb[:, o[4]:o[5]], ((0, 0), (0, LANES - IDX_HEADS)))
    wg, wx = wb[:, o[5]:o[6]], wb[:, o[6]:o[7]]
    wukv = jnp.concatenate([w_uk, w_uv], axis=1).astype(BF16)
    kvn = kv_norm.reshape(1, A_KV_RANK).astype(F32)
    ws = (wq, wc, wqi, wki, wwi, wg, wx, kvn, wukv)
    outs = [(A_OUT, BF16), (IDX_HEADS * LANES, BF16), (LANES, BF16), (LANES, F32), (RG_WIDTH, F32),
            (RG_WIDTH, F32), (2 * A_HEAD_DIM, BF16)]
    return pl.pallas_call(
        _even_in_kernel,
        grid=(B, TP // tm),
        in_specs=[pl.BlockSpec((1, tm, D), lambda b, j: (b, j, 0))] + [_const_spec(w.shape) for w in ws],
        out_specs=[pl.BlockSpec((1, tm, n), lambda b, j: (b, j, 0)) for n, _ in outs],
        out_shape=[jax.ShapeDtypeStruct((B, TP, n), dt) for n, dt in outs],
        compiler_params=_cparams("parallel", "parallel"),
        name="even_in_proj",
    )(h, *ws)


def _dsa_body(nk, q_ref, qi_ref, wi_ref, ki_ref, kv_ref, o_ref, key_ref, bias_ref, vt_ref, n_sel, t_real):
    i = pl.program_id(1)
    tq = q_ref.shape[1]
    qi = qi_ref[0]
    ki = ki_ref[0, :nk, :]
    wit = wi_ref[0].T
    nt = (((1,), (1,)), ((), ()))
    score = jnp.zeros((nk, tq), F32)
    for h in range(IDX_HEADS):
        s = lax.dot_general(ki, qi[:, h * LANES:(h + 1) * LANES], nt, preferred_element_type=F32)
        score = score + wit[h:h + 1, :] * jnp.maximum(s, 0.0)

    kpos = lax.broadcasted_iota(I32, (nk, tq), 0)
    qpos = i * tq + lax.broadcasted_iota(I32, (nk, tq), 1)
    bits = lax.bitcast_convert_type(score, I32)
    key = jnp.where(bits < 0, bits ^ jnp.int32(0x7FFFFFFF), bits)
    key_ref[:nk, :] = jnp.where(kpos <= qpos, key, jnp.int32(INT_MIN))
    key_ref[0:FRONT, :] = jnp.full((FRONT, tq), INT_MIN, I32)
    if nk > FRONT + t_real:
        key_ref[FRONT + t_real:nk, :] = jnp.full((nk - FRONT - t_real, tq), INT_MIN, I32)

    def over_keys(x, op):
        pair = jnp.add if op is jnp.sum else jnp.maximum
        group = 8
        x = x.reshape(nk // (group * SUBLANES), group, SUBLANES, x.shape[1])
        while x.shape[1] > 1:
            half = x.shape[1] // 2
            x = pair(x[:, :half], x[:, half:])
        return op(op(x[:, 0], axis=0), axis=0, keepdims=True)

    def count(mask):
        return over_keys(jnp.where(mask, 1, 0).astype(I32), jnp.sum)

    def any_query(mask):
        return jnp.max(jnp.where(mask, 1, 0))

    n_vis = count(key_ref[:nk, :] > jnp.int32(INT_MIN))

    first_bits, later_bits = 20, 4

    def bit_cond(c):
        return (c[0] < 32) & (c[3] > 0)

    def bit_group(c):
        it0, cand, cnt_c, _ = c
        n_bits = jnp.where(it0 == 0, first_bits, later_bits)

        def bit_step(b, cc):
            cand, cnt_c = cc
            trial = cand + jnp.left_shift(jnp.int32(1), 31 - (it0 + b))
            cnt = count(key_ref[:nk, :] >= trial)
            take = cnt >= n_sel
            return jnp.where(take, trial, cand), jnp.where(take, cnt, cnt_c)

        cand, cnt_c = lax.fori_loop(0, n_bits, bit_step, (cand, cnt_c))
        return it0 + n_bits, cand, cnt_c, any_query((cnt_c != n_sel) & (n_vis > n_sel))

    start = (jnp.int32(0), jnp.full((1, tq), INT_MIN, I32), jnp.full((1, tq), nk, I32),
             any_query(n_vis > n_sel))
    _, thr, cnt_ge, _ = lax.while_loop(bit_cond, bit_group, start)
    keys = key_ref[:nk, :]
    bias_ref[:nk, :] = jnp.where(keys >= jnp.maximum(thr, jnp.int32(INT_MIN + 1)), 0.0, NEG_BIG)
    tie_queries = (cnt_ge > n_sel) & (thr > jnp.int32(INT_MIN))

    @pl.when(any_query(tie_queries) > 0)
    def _():
        keys = key_ref[:nk, :]
        eq = keys == thr
        need = n_sel - count(keys > thr)
        nbits = max(1, (nk - 1).bit_length())

        def idx_step(it, cand):
            trial = cand + jnp.left_shift(jnp.int32(1), nbits - 1 - it)
            return jnp.where(count(eq & (kpos < trial)) < need, trial, cand)

        last = lax.fori_loop(0, nbits, idx_step, jnp.zeros((1, tq), I32))
        sel = (keys > thr) | (eq & (kpos <= last))
        bias_ref[:nk, :] = jnp.where(sel & (keys > jnp.int32(INT_MIN)), 0.0, NEG_BIG)

    bias = bias_ref[:nk, :]
    q = q_ref[0]
    k = kv_ref[0, :nk, :A_HEAD_DIM]
    qs = jnp.concatenate([q[:, h * A_HEAD_DIM:(h + 1) * A_HEAD_DIM] for h in range(A_HEADS)], axis=0)
    lg = lax.dot_general(k, qs, nt, preferred_element_type=F32) * (A_HEAD_DIM ** -0.5)
    lg = lg + jnp.concatenate([bias] * A_HEADS, axis=1)
    m = over_keys(lg, jnp.max)
    p = jnp.exp(lg - m)
    l = over_keys(p, jnp.sum)
    ot = jnp.dot(vt_ref[:, :nk], p.astype(BF16), preferred_element_type=F32) / l
    for h in range(A_HEADS):
        o_ref[0, :, h * A_HEAD_DIM:(h + 1) * A_HEAD_DIM] = ot[:, h * tq:(h + 1) * tq].T.astype(BF16)


DSA_WIDTH_STEP = 1


def _dsa_kernel(q_ref, qi_ref, wi_ref, ki_ref, kv_ref, o_ref, key_ref, bias_ref, vt_ref, *, n_sel, t_real):
    i = pl.program_id(1)
    tq = q_ref.shape[1]
    tp = ki_ref.shape[1]
    n_tiles = tp // tq

    @pl.when(i == 0)
    def _():
        vt_ref[...] = kv_ref[0, :, A_HEAD_DIM:].astype(F32).T.astype(BF16)

    lo = 0
    while lo < n_tiles:
        hi = min(lo + DSA_WIDTH_STEP, n_tiles)

        @pl.when((i >= lo) & (i < hi))
        def _(hi=hi):
            _dsa_body(hi * tq, q_ref, qi_ref, wi_ref, ki_ref, kv_ref, o_ref, key_ref, bias_ref, vt_ref,
                      n_sel, t_real)

        lo = hi


def _dsa(q, qi, wi, ki, kv, t_real):
    B, TP, _ = q.shape
    tq = TIME_TILE
    n_sel = min(IDX_TOPK_CAP, t_real // 4)
    kern = functools.partial(_dsa_kernel, n_sel=n_sel, t_real=t_real)
    qspec = lambda n: pl.BlockSpec((1, tq, n), lambda b, i: (b, i, 0))
    kspec = lambda n: pl.BlockSpec((1, TP, n), lambda b, i: (b, 0, 0))
    return pl.pallas_call(
        kern,
        grid=(B, TP // tq),
        in_specs=[qspec(A_OUT), qspec(IDX_HEADS * LANES), qspec(LANES), kspec(LANES), kspec(2 * A_HEAD_DIM)],
        out_specs=qspec(A_OUT),
        out_shape=jax.ShapeDtypeStruct((B, TP, A_OUT), BF16),
        scratch_shapes=[pltpu.VMEM((TP, tq), I32), pltpu.VMEM((TP, tq), F32), pltpu.VMEM((A_HEAD_DIM, TP), BF16)],
        compiler_params=_cparams("parallel", "arbitrary"),
        name="dsa_attention",
    )(q, qi, wi, ki, kv)


def _shift_rows(x, s, fill):
    rows = lax.broadcasted_iota(I32, x.shape, 0)
    return jnp.where(rows >= s, pltpu.roll(x, s, 0), fill)


def _causal_conv(x, win_ref, cw):
    tt = x.shape[0]
    win_ref[0:SUBLANES, :] = win_ref[tt:tt + SUBLANES, :]
    win_ref[SUBLANES:, :] = x
    y = cw[CONV_WIDTH - 1:CONV_WIDTH, :] * x
    for t in range(CONV_WIDTH - 1):
        off = SUBLANES - (CONV_WIDTH - 1) + t
        y = y + cw[t:t + 1, :] * win_ref[off:off + tt, :]
    return y


def _griffin_kernel(xb_ref, gate_ref, cw_ref, cb_ref, wg_ref, bg_ref, lam_ref, o_ref, tail_ref, car_ref, *, t_real):
    j = pl.program_id(1)
    tt = xb_ref.shape[1]

    @pl.when(j == 0)
    def _():
        tail_ref[...] = jnp.zeros_like(tail_ref)
        car_ref[...] = jnp.zeros_like(car_ref)

    xr = _causal_conv(xb_ref[0], tail_ref, cw_ref[...]) + cb_ref[...]
    g = jnp.dot(xr.astype(BF16), wg_ref[...], preferred_element_type=F32) + bg_ref[...]
    r = jax.nn.sigmoid(g[:, :RG_WIDTH])
    ig = jax.nn.sigmoid(g[:, RG_WIDTH:])
    log_a = -RG_C * r * _softplus(-lam_ref[...])
    a = jnp.exp(log_a)
    th = jnp.tanh(log_a)
    u = jnp.sqrt(-2.0 * th / (1.0 - th)) * (ig * xr)
    p = j * tt + lax.broadcasted_iota(I32, xr.shape, 0)
    u = jnp.where(_valid_rows(p, t_real), u, 0.0)
    s = 1
    while s < tt:
        u = u + a * _shift_rows(u, s, 0.0)
        a = a * _shift_rows(a, s, 1.0)
        s *= 2
    hcur = u + a * car_ref[0:1, :]
    car_ref[...] = jnp.broadcast_to(hcur[tt - 1:tt, :], car_ref.shape)
    o_ref[0] = hcur * jax.nn.gelu(gate_ref[0])


def _block_diag(w):
    nb, bs, _ = w.shape
    eye = jnp.eye(nb, dtype=w.dtype)
    return (w[:, :, None, :] * eye[:, None, :, None]).reshape(nb * bs, nb * bs)


def _griffin(xb, gate, conv_w, conv_b, rg_wa, rg_ba, rg_wx, rg_bx, rg_lambda, t_real):
    B, TP, R = xb.shape
    tt = TIME_TILE
    cw = jnp.pad(conv_w.astype(F32), ((0, SUBLANES - CONV_WIDTH), (0, 0)))
    cb = conv_b.reshape(1, R).astype(F32)
    wg = jnp.concatenate([_block_diag(rg_wa), _block_diag(rg_wx)], axis=1).astype(BF16)
    bg = jnp.concatenate([rg_ba, rg_bx]).reshape(1, 2 * R).astype(F32)
    lam = rg_lambda.reshape(1, R).astype(F32)
    tspec = pl.BlockSpec((1, tt, R), lambda b, j: (b, j, 0))
    ws = (cw, cb, wg, bg, lam)
    return pl.pallas_call(
        functools.partial(_griffin_kernel, t_real=t_real),
        grid=(B, TP // tt),
        in_specs=[tspec, tspec] + [_const_spec(w.shape) for w in ws],
        out_specs=tspec,
        out_shape=jax.ShapeDtypeStruct((B, TP, R), F32),
        scratch_shapes=[pltpu.VMEM((SUBLANES + tt, R), F32), pltpu.VMEM((SUBLANES, R), F32)],
        compiler_params=_cparams("parallel", "arbitrary"),
        name="griffin_rglru",
    )(xb, gate, *ws)


def _out_ln_kernel(*refs, n_in, t_real):
    h_ref = refs[0]
    a_refs = refs[1:1 + n_in]
    w_refs = refs[1 + n_in:1 + 2 * n_in]
    g_ref, b_ref, o_ref = refs[1 + 2 * n_in:]
    j = pl.program_id(1)
    tm = h_ref.shape[1]
    n_chunks = 4 if tm % (4 * SUBLANES) == 0 else 1
    tc = tm // n_chunks
    mixes = []
    for c in range(n_chunks):
        rs = slice(c * tc, (c + 1) * tc)
        mix = None
        for a_ref, w_ref in zip(a_refs, w_refs):
            d = jnp.dot(a_ref[0, rs, :].astype(BF16), w_ref[...], preferred_element_type=F32)
            mix = d if mix is None else mix + d
        mixes.append(mix)
    for c in range(n_chunks):
        rs = slice(c * tc, (c + 1) * tc)
        out = _layer_norm_rows(DN_ALPHA * h_ref[0, rs, :] + mixes[c], g_ref[...], b_ref[...])
        p = j * tm + c * tc + lax.broadcasted_iota(I32, out.shape, 0)
        o_ref[0, rs, :] = jnp.where(_valid_rows(p, t_real), out, 0.0)


def _out_ln(h, acts, ws, ln_g, ln_b, t_real):
    B, TP, D = h.shape
    tm = _row_tile(TP)
    ws = [w.astype(BF16) for w in ws]
    g = ln_g.reshape(1, D).astype(F32)
    b = ln_b.reshape(1, D).astype(F32)
    tspec = lambda n: pl.BlockSpec((1, tm, n), lambda bb, j: (bb, j, 0))
    return pl.pallas_call(
        functools.partial(_out_ln_kernel, n_in=len(acts), t_real=t_real),
        grid=(B, TP // tm),
        in_specs=[tspec(D)] + [tspec(a.shape[-1]) for a in acts] + [_const_spec(w.shape) for w in ws]
        + [_const_spec(g.shape), _const_spec(b.shape)],
        out_specs=tspec(D),
        out_shape=jax.ShapeDtypeStruct((B, TP, D), F32),
        compiler_params=_cparams("parallel", "parallel"),
        name="out_proj_ln",
    )(h, *acts, *ws, g, b)


E_LANE0, G_LANE0 = 0, N_EXPERTS
R_E1, R_E2, R_RANK1, R_RANK2, R_G1, R_G2 = range(6)


def _router_kernel(h_ref, w_ref, b_ref, rec_ref, rect_ref, cnt_ref, car_ref):
    t = pl.program_id(0)

    @pl.when(t == 0)
    def _():
        car_ref[...] = jnp.zeros_like(car_ref)

    tm = h_ref.shape[0]
    h = h_ref[...]
    h_hi = h.astype(BF16)
    h_lo = (h - h_hi.astype(F32)).astype(BF16)
    both = jnp.dot(h_hi, w_ref[...], preferred_element_type=F32)
    lg = (both[:, :LANES] + both[:, LANES:]
          + jnp.dot(h_lo, w_ref[:, :LANES], preferred_element_type=F32) + b_ref[...])
    lane = lax.broadcasted_iota(I32, lg.shape, 1)
    neg_inf = -jnp.inf

    def first_argmax(x):
        m = jnp.max(x, axis=1, keepdims=True)
        return m, jnp.min(jnp.where(x == m, lane, jnp.int32(2 ** 30)), axis=1, keepdims=True)

    glog = jnp.where((lane >= G_LANE0) & (lane < G_LANE0 + MOE_GROUPS), lg, neg_inf)
    gmax, glane = first_argmax(glog)
    grp = glane - G_LANE0
    grp_gate = 1.0 / jnp.sum(jnp.exp(glog - gmax), axis=1, keepdims=True)
    in_grp = (lane >= grp * MOE_PER_GROUP) & (lane < (grp + 1) * MOE_PER_GROUP)
    elog = jnp.where(in_grp, lg, neg_inf)
    v1, e1 = first_argmax(elog)
    v2, e2 = first_argmax(jnp.where(lane == e1, neg_inf, elog))
    ex = jnp.exp(v2 - v1)
    g1 = grp_gate / (1.0 + ex)
    g2 = grp_gate * ex / (1.0 + ex)

    oh1 = jnp.where(lane == e1, 1.0, 0.0)
    oh2 = jnp.where(lane == e2, 1.0, 0.0)
    rows = lax.broadcasted_iota(I32, (tm, tm), 0)
    cols = lax.broadcasted_iota(I32, (tm, tm), 1)
    ltri = jnp.where(cols < rows, 1.0, 0.0).astype(BF16)
    p1 = jnp.dot(ltri, oh1.astype(BF16), preferred_element_type=F32)
    p2 = jnp.dot(ltri, oh2.astype(BF16), preferred_element_type=F32)
    car = car_ref[0:1, :]
    c1 = jnp.sum(oh1, axis=0, keepdims=True)
    c2 = jnp.sum(oh2, axis=0, keepdims=True)
    rank1 = jnp.sum(oh1 * (car + p1), axis=1, keepdims=True)
    rank2 = jnp.sum(oh2 * (car + c1 + p2), axis=1, keepdims=True)
    car = car + c1 + c2
    car_ref[...] = jnp.broadcast_to(car, car_ref.shape)
    cnt_ref[...] = jnp.broadcast_to(car, cnt_ref.shape)

    rec = jnp.zeros(lg.shape, F32)
    for ln, val in ((R_E1, e1.astype(F32)), (R_E2, e2.astype(F32)), (R_RANK1, rank1), (R_RANK2, rank2),
                    (R_G1, g1), (R_G2, g2)):
        rec = jnp.where(lane == ln, val, rec)
    rec_ref[...] = rec
    rect_ref[0] = rec.T[:SUBLANES, :]


def _moe_route(hf, group_w, group_b, expert_w, expert_b):
    N, D = hf.shape
    tm = 2 * MOE_BLOCK if N % (2 * MOE_BLOCK) == 0 else MOE_BLOCK
    nt = N // tm
    w = jnp.zeros((D, LANES), F32)
    w = w.at[:, E_LANE0:E_LANE0 + N_EXPERTS].set(expert_w).at[:, G_LANE0:G_LANE0 + MOE_GROUPS].set(group_w)
    b = jnp.zeros((1, LANES), F32)
    b = b.at[0, E_LANE0:E_LANE0 + N_EXPERTS].set(expert_b).at[0, G_LANE0:G_LANE0 + MOE_GROUPS].set(group_b)
    w_hi = w.astype(BF16)
    w = jnp.concatenate([w_hi, (w - w_hi.astype(F32)).astype(BF16)], axis=1)
    return pl.pallas_call(
        _router_kernel,
        grid=(nt,),
        in_specs=[pl.BlockSpec((tm, D), lambda t: (t, 0)), _const_spec(w.shape), _const_spec(b.shape)],
        out_specs=[pl.BlockSpec((tm, LANES), lambda t: (t, 0)),
                   pl.BlockSpec((1, SUBLANES, tm), lambda t: (t, 0, 0)),
                   _const_spec((SUBLANES, LANES))],
        out_shape=[jax.ShapeDtypeStruct((N, LANES), F32),
                   jax.ShapeDtypeStruct((nt, SUBLANES, tm), F32),
                   jax.ShapeDtypeStruct((SUBLANES, LANES), F32)],
        scratch_shapes=[pltpu.VMEM((SUBLANES, LANES), F32)],
        compiler_params=_cparams("arbitrary"),
        name="moe_router",
    )(hf, w, b)


def _row_copy(src, si, dst, di, sem):
    return pltpu.make_async_copy(src.at[pl.ds(si, 1)], dst.at[pl.ds(di, 1)], sem)


def _route_copy(idx_hbm, idx_smem, sems, t):
    return pltpu.make_async_copy(idx_hbm.at[t], idx_smem.at[t % 2], sems.at[t % 2])


def _slot(idx_smem, t, k, r):
    return idx_smem[t % 2, k * MOE_BLOCK + r]


def _slots_kernel(rect_ref, ps_ref, o_ref):
    rec = rect_ref[0]
    erow = lax.broadcasted_iota(I32, (N_EXPERTS, rec.shape[1]), 0).astype(F32)
    rows = []
    for k in range(MOE_TOPK):
        e = rec[R_E1 + k:R_E1 + k + 1, :]
        base = jnp.sum(jnp.where(erow == e, ps_ref[...], 0.0), axis=0, keepdims=True)
        rows.append(rec[R_RANK1 + k:R_RANK1 + k + 1, :] + base)
    rows.append(jnp.zeros((SUBLANES - MOE_TOPK, rec.shape[1]), F32))
    o_ref[0] = jnp.concatenate(rows, axis=0).astype(I32)


def _moe_slots(rect, pstart):
    nt, _, tm = rect.shape
    ps = pstart.astype(F32).reshape(N_EXPERTS, 1)
    return pl.pallas_call(
        _slots_kernel,
        grid=(nt,),
        in_specs=[pl.BlockSpec((1, SUBLANES, tm), lambda t: (t, 0, 0)), _const_spec(ps.shape)],
        out_specs=pl.BlockSpec((1, SUBLANES, tm), lambda t: (t, 0, 0)),
        out_shape=jax.ShapeDtypeStruct((nt, SUBLANES, tm), I32),
        compiler_params=_cparams("parallel"),
        name="moe_slots",
    )(rect, ps)


N_HBUF = 3


def _dispatch_kernel(idx_hbm, h_hbm, xs_in, xs_hbm, idx_smem, hbuf, isems, hsems, ssems):
    del xs_in
    t = pl.program_id(0)
    nt = pl.num_programs(0)
    tm = MOE_BLOCK

    def tile_copy(u):
        return pltpu.make_async_copy(h_hbm.at[pl.ds(u * tm, tm)], hbuf.at[u % N_HBUF], hsems.at[u % N_HBUF])

    def wait_scatters(u):
        for _ in range(tm * MOE_TOPK):
            _row_copy(hbuf.at[0], 0, xs_hbm, 0, ssems.at[u % 2]).wait()

    @pl.when(t == 0)
    def _():
        _route_copy(idx_hbm, idx_smem, isems, t).start()
        tile_copy(t).start()

    _route_copy(idx_hbm, idx_smem, isems, t).wait()
    tile_copy(t).wait()

    @pl.when(t + 1 < nt)
    def _():
        _route_copy(idx_hbm, idx_smem, isems, t + 1).start()
        tile_copy(t + 1).start()

    src = hbuf.at[t % N_HBUF]
    for r in range(tm):
        for k in range(MOE_TOPK):
            _row_copy(src, r, xs_hbm, _slot(idx_smem, t, k, r), ssems.at[t % 2]).start()

    @pl.when(t > 0)
    def _():
        wait_scatters(t - 1)

    @pl.when(t == nt - 1)
    def _():
        wait_scatters(t)


def _moe_dispatch(hf, idx, n_slots):
    N, D = hf.shape
    nt = N // MOE_BLOCK
    xs0 = jnp.zeros((n_slots, D), F32)
    return pl.pallas_call(
        _dispatch_kernel,
        grid=(nt,),
        in_specs=[pl.BlockSpec(memory_space=pl.ANY)] * 3,
        out_specs=pl.BlockSpec(memory_space=pl.ANY),
        scratch_shapes=[pltpu.SMEM((2, MOE_TOPK * MOE_BLOCK), I32),
                        pltpu.VMEM((N_HBUF, MOE_BLOCK, D), F32),
                        pltpu.SemaphoreType.DMA((2,)), pltpu.SemaphoreType.DMA((N_HBUF,)),
                        pltpu.SemaphoreType.DMA((2,))],
        out_shape=jax.ShapeDtypeStruct((n_slots, D), F32),
        input_output_aliases={2: 0},
        compiler_params=_cparams("arbitrary"),
        name="moe_dispatch",
    )(idx, hf, xs0)


def _ffn_kernel(be_ref, nb_ref, x_ref, wg_ref, wu_ref, wd_ref, o_ref, wg16, wu16, wd16):
    b = pl.program_id(0)

    @pl.when((b == 0) | (be_ref[b] != be_ref[jnp.maximum(b - 1, 0)]))
    def _():
        wg16[...] = wg_ref[...].astype(BF16)
        wu16[...] = wu_ref[...].astype(BF16)
        wd16[...] = wd_ref[...].astype(BF16)

    @pl.when(b < nb_ref[0])
    def _():
        n_chunks = x_ref.shape[0] // TIME_TILE
        rows = [slice(c * TIME_TILE, (c + 1) * TIME_TILE) for c in range(n_chunks)]
        xs = [x_ref[r, :].astype(BF16) for r in rows]
        hgs = [jnp.dot(x, wg16[...], preferred_element_type=F32) for x in xs]
        hus = [jnp.dot(x, wu16[...], preferred_element_type=F32) for x in xs]
        hids = [((hg * jax.nn.sigmoid(hg)) * hu).astype(BF16) for hg, hu in zip(hgs, hus)]
        for r, hid in zip(rows, hids):
            o_ref[r, :] = jnp.dot(hid, wd16[...], preferred_element_type=F32)

    @pl.when(pl.program_id(0) >= nb_ref[0])
    def _():
        o_ref[...] = jnp.zeros_like(o_ref)


def _moe_ffn(xs, block_expert, nb_used, w_gate, w_up, w_down):
    P, D = xs.shape
    nb = P // FFN_BLOCK
    wspec = lambda s: pl.BlockSpec((None,) + s, lambda b, be, nbu: (be[b], 0, 0))
    return pl.pallas_call(
        _ffn_kernel,
        grid_spec=pltpu.PrefetchScalarGridSpec(
            num_scalar_prefetch=2,
            grid=(nb,),
            in_specs=[pl.BlockSpec((FFN_BLOCK, D), lambda b, be, nbu: (b, 0)),
                      wspec((D, EXPERT_FF)), wspec((D, EXPERT_FF)), wspec((EXPERT_FF, D))],
            out_specs=pl.BlockSpec((FFN_BLOCK, D), lambda b, be, nbu: (b, 0)),
            scratch_shapes=[pltpu.VMEM((D, EXPERT_FF), BF16), pltpu.VMEM((D, EXPERT_FF), BF16),
                            pltpu.VMEM((EXPERT_FF, D), BF16)],
        ),
        out_shape=jax.ShapeDtypeStruct((P, D), F32),
        compiler_params=_cparams("arbitrary"),
        name="moe_expert_ffn",
    )(block_expert, nb_used, xs, w_gate, w_up, w_down)


def _combine_kernel(idx_hbm, ys_hbm, h_ref, rec_ref, g_ref, b_ref, o_ref, idx_smem, ybuf, isems, gsems,
                    *, tp, t_real):
    t = pl.program_id(0)
    nt = pl.num_programs(0)
    tm = MOE_BLOCK

    def issue_gathers(u):
        dst = ybuf.at[u % 2]
        for r in range(tm):
            for k in range(MOE_TOPK):
                _row_copy(ys_hbm, _slot(idx_smem, u, k, r), dst.at[k], r, gsems.at[u % 2]).start()

    @pl.when(t == 0)
    def _():
        _route_copy(idx_hbm, idx_smem, isems, t).start()
        _route_copy(idx_hbm, idx_smem, isems, t).wait()
        issue_gathers(t)

        @pl.when(nt > 1)
        def _():
            _route_copy(idx_hbm, idx_smem, isems, t + 1).start()

    @pl.when(t + 1 < nt)
    def _():
        _route_copy(idx_hbm, idx_smem, isems, t + 1).wait()
        issue_gathers(t + 1)

    @pl.when(t + 2 < nt)
    def _():
        _route_copy(idx_hbm, idx_smem, isems, t + 2).start()

    for _ in range(tm * MOE_TOPK):
        _row_copy(ys_hbm, 0, ybuf.at[0, 0], 0, gsems.at[t % 2]).wait()
    b = t % 2
    rec = rec_ref[...]
    y = DN_ALPHA * h_ref[...] + rec[:, R_G1:R_G1 + 1] * ybuf[b, 0] + rec[:, R_G2:R_G2 + 1] * ybuf[b, 1]
    out = _layer_norm_rows(y, g_ref[...], b_ref[...])
    n = (t * tm + lax.broadcasted_iota(I32, out.shape, 0)).astype(F32)
    p = n - jnp.floor((n + 0.5) / tp) * tp
    o_ref[...] = jnp.where((p >= FRONT) & (p < FRONT + t_real), out, 0.0)


def _moe_combine(hf, ys, idx, rec, ln_g, ln_b, tp, t_real):
    N, D = hf.shape
    nt = N // MOE_BLOCK
    g = ln_g.reshape(1, D).astype(F32)
    b = ln_b.reshape(1, D).astype(F32)
    anyspec = pl.BlockSpec(memory_space=pl.ANY)
    return pl.pallas_call(
        functools.partial(_combine_kernel, tp=tp, t_real=t_real),
        grid=(nt,),
        in_specs=[anyspec, anyspec,
                  pl.BlockSpec((MOE_BLOCK, D), lambda t: (t, 0)),
                  pl.BlockSpec((MOE_BLOCK, LANES), lambda t: (t, 0)),
                  _const_spec((1, D)), _const_spec((1, D))],
        out_specs=pl.BlockSpec((MOE_BLOCK, D), lambda t: (t, 0)),
        scratch_shapes=[pltpu.SMEM((2, MOE_TOPK * MOE_BLOCK), I32),
                        pltpu.VMEM((2, MOE_TOPK, MOE_BLOCK, D), F32),
                        pltpu.SemaphoreType.DMA((2,)), pltpu.SemaphoreType.DMA((2,))],
        out_shape=jax.ShapeDtypeStruct((N, D), F32),
        compiler_params=_cparams("arbitrary"),
        name="moe_combine_ln",
    )(idx, ys, hf, rec, g, b)


def _moe_layer(h, group_w, group_b, expert_w, expert_b, w_gate, w_up, w_down, ln_g, ln_b, t_real):
    B, TP, D = h.shape
    hf = h.reshape(B * TP, D)
    N = B * TP
    rec, rect, cnt = _moe_route(hf, group_w, group_b, expert_w, expert_b)
    counts = cnt[0, :N_EXPERTS].astype(I32)
    padded = (counts + FFN_BLOCK - 1) // FFN_BLOCK * FFN_BLOCK
    pend = jnp.cumsum(padded)
    pstart = (pend - padded).astype(I32)
    n_blocks = -(-(N * MOE_TOPK) // FFN_BLOCK) + N_EXPERTS
    block_start = jnp.arange(n_blocks, dtype=I32) * FFN_BLOCK
    block_expert = jnp.minimum(jnp.sum((pend[None, :] <= block_start[:, None]).astype(I32), axis=1), N_EXPERTS - 1)
    nb_used = (pend[-1:] // FFN_BLOCK).astype(I32)
    n_slots = n_blocks * FFN_BLOCK
    slots = _moe_slots(rect, pstart)
    idx = slots[:, :MOE_TOPK, :].reshape(slots.shape[0], MOE_TOPK, slots.shape[2] // MOE_BLOCK, MOE_BLOCK)
    idx = idx.transpose(0, 2, 1, 3).reshape(N // MOE_BLOCK, MOE_TOPK * MOE_BLOCK)
    xs = _moe_dispatch(hf, idx, n_slots)
    ys = _moe_ffn(xs, block_expert, nb_used, w_gate, w_up, w_down)
    out = _moe_combine(hf, ys, idx, rec, ln_g, ln_b, TP, t_real)
    return out.reshape(B, TP, D)


def _even_layer(h, w_in, kv_norm, w_uk, w_uv, conv_w, conv_b, rg_wa, rg_ba, rg_wx, rg_bx, rg_lambda, w_out,
                ln_g, ln_b, t_real):
    q, qi, ki, wi, gate, xb, kv = _even_in_proj(h, w_in, kv_norm, w_uk, w_uv)
    attn = _dsa(q, qi, wi, ki, kv, t_real)
    rec = _griffin(xb, gate, conv_w, conv_b, rg_wa, rg_ba, rg_wx, rg_bx, rg_lambda, t_real)
    return _out_ln(h, [attn, rec], [w_out[:A_OUT], w_out[A_OUT:]], ln_g, ln_b, t_real)


def _odd_in_kernel(h_ref, wq_ref, wk_ref, wv_ref, wz_ref, wa_ref, wb_ref, q_ref, k_ref, v_ref, z_ref, a_ref, b_ref):
    a = h_ref[0].astype(BF16)
    for w_ref, o_ref in ((wq_ref, q_ref), (wk_ref, k_ref), (wv_ref, v_ref), (wz_ref, z_ref), (wa_ref, a_ref),
                         (wb_ref, b_ref)):
        o_ref[0] = jnp.dot(a, w_ref[...], preferred_element_type=F32).astype(o_ref.dtype)


def _odd_in_proj(h, w_in):
    B, TP, D = h.shape
    tm = _row_tile(TP)
    wb = w_in.astype(BF16)
    o = [0, GDN_QK_WIDTH, 2 * GDN_QK_WIDTH, 2 * GDN_QK_WIDTH + GDN_V_WIDTH, 2 * GDN_QK_WIDTH + 2 * GDN_V_WIDTH]
    o += [o[-1] + GDN_V_HEADS, o[-1] + 2 * GDN_V_HEADS]
    ws = [wb[:, o[i]:o[i + 1]] for i in range(4)]
    ws += [jnp.pad(wb[:, o[i]:o[i + 1]], ((0, 0), (0, LANES - GDN_V_HEADS))) for i in (4, 5)]
    widths = [GDN_QK_WIDTH, GDN_QK_WIDTH, GDN_V_WIDTH, GDN_V_WIDTH, LANES, LANES]
    return pl.pallas_call(
        _odd_in_kernel,
        grid=(B, TP // tm),
        in_specs=[pl.BlockSpec((1, tm, D), lambda b, j: (b, j, 0))] + [_const_spec(w.shape) for w in ws],
        out_specs=[pl.BlockSpec((1, tm, n), lambda b, j: (b, j, 0)) for n in widths],
        out_shape=[jax.ShapeDtypeStruct((B, TP, n), BF16 if n > LANES else F32) for n in widths],
        compiler_params=_cparams("parallel", "parallel"),
        name="odd_in_proj",
    )(h, *ws)


def _conv_silu(x, win_ref, cw):
    y = _causal_conv(x, win_ref, cw)
    return y * jax.nn.sigmoid(y)


def _gdn_prep_kernel(q_ref, k_ref, v_ref, a_ref, b_ref, cwq_ref, cwk_ref, cwv_ref, alog_ref, dtb_ref,
                     u_ref, w_ref, qg_ref, kg_ref, at_ref, e_ref,
                     tq_ref, tk_ref, tv_ref, qs_ref, ks_ref, vs_ref, *, t_real):
    j = pl.program_id(1)
    tt = q_ref.shape[1]
    hd = GDN_HEAD_DIM
    half = CHUNK

    @pl.when(j == 0)
    def _():
        tq_ref[...] = jnp.zeros_like(tq_ref)
        tk_ref[...] = jnp.zeros_like(tk_ref)
        tv_ref[...] = jnp.zeros_like(tv_ref)

    rows1 = j * tt + lax.broadcasted_iota(I32, (tt, 1), 0)
    valid = _valid_rows(rows1, t_real)
    q = jnp.where(valid, _conv_silu(q_ref[0].astype(F32), tq_ref, cwq_ref[...]), 0.0)
    k = jnp.where(valid, _conv_silu(k_ref[0].astype(F32), tk_ref, cwk_ref[...]), 0.0)
    vs_ref[...] = jnp.where(valid, _conv_silu(v_ref[0].astype(F32), tv_ref, cwv_ref[...]), 0.0)
    for h in range(GDN_K_HEADS):
        sl = slice(h * hd, (h + 1) * hd)
        qh, kh = q[:, sl], k[:, sl]
        qs_ref[:, sl] = qh * lax.rsqrt(jnp.sum(qh * qh, axis=1, keepdims=True) + 1e-6) * (hd ** -0.5)
        ks_ref[:, sl] = kh * lax.rsqrt(jnp.sum(kh * kh, axis=1, keepdims=True) + 1e-6)

    lane = lax.broadcasted_iota(I32, (tt, LANES), 1)
    rowi = lax.broadcasted_iota(I32, (tt, LANES), 0)
    live = valid & (lane < GDN_V_HEADS)
    g = jnp.where(live, -jnp.exp(alog_ref[...]) * _softplus(a_ref[0] + dtb_ref[...]), 0.0)
    beta = jnp.where(live, jax.nn.sigmoid(b_ref[0]), 0.0)
    gc = g
    s = 1
    while s < CHUNK:
        gc = gc + jnp.where((rowi % CHUNK) >= s, pltpu.roll(gc, s, 0), 0.0)
        s *= 2
    gc_next = pltpu.roll(gc, LANES - 1, 1)
    n_ch = tt // CHUNK
    e_rows = []
    for c in range(n_ch):
        e_rows.append(jnp.broadcast_to(jnp.exp(gc[(c + 1) * CHUNK - 1:(c + 1) * CHUNK, :]), (CHUNK, LANES)))
    e_ref[0] = jnp.concatenate(e_rows, axis=0)

    nt = (((1,), (1,)), ((), ()))
    l2 = lax.broadcasted_iota(I32, (CHUNK, 2 * half), 1)
    ii = lax.broadcasted_iota(I32, (CHUNK, 2 * half), 0)
    jj = l2 % half
    left = l2 < half
    r2 = lax.broadcasted_iota(I32, (2 * half, 2 * half), 0)
    c2 = lax.broadcasted_iota(I32, (2 * half, 2 * half), 1)
    eye2 = jnp.where(r2 == c2, 1.0, 0.0)
    bdot = lambda x, y: jnp.dot(x.astype(BF16), y.astype(BF16), preferred_element_type=F32)
    pair_group = GDN_K_HEADS
    for c, g0 in [(c, g0) for c in range(n_ch) for g0 in range(0, GDN_K_HEADS, pair_group)]:
        rs = slice(c * CHUNK, (c + 1) * CHUNK)
        gc_c, beta_c = gc[rs], beta[rs]
        gt = jnp.concatenate([gc_c, gc_next[rs]], axis=0).T
        glast = gc_c[CHUNK - 1:CHUNK, :]
        pairs = range(g0, g0 + pair_group)
        ms, tinvs, rhss = [], [], []
        for pr in pairs:
            ha, hb = 2 * pr, 2 * pr + 1
            ksl = slice(pr * hd, (pr + 1) * hd)
            osl = slice(ha * hd, (hb + 1) * hd)
            kc = ks_ref[rs, ksl]
            qc = qs_ref[rs, ksl]
            kc16 = kc.astype(BF16)
            k2 = jnp.concatenate([kc16, kc16], axis=0)
            kk2 = lax.dot_general(kc16, k2, nt, preferred_element_type=F32)
            qk2 = lax.dot_general(qc.astype(BF16), k2, nt, preferred_element_type=F32)
            gca, gcb = gc_c[:, ha:ha + 1], gc_c[:, hb:hb + 1]
            ba, bb = beta_c[:, ha:ha + 1], beta_c[:, hb:hb + 1]
            gcol2 = jnp.where(left, gca, gcb)
            bcol2 = jnp.where(left, ba, bb)
            decay2 = jnp.exp(jnp.where(ii >= jj, gcol2 - gt[ha:ha + 1, :], -jnp.inf))
            m2 = jnp.where(ii > jj, bcol2 * kk2 * decay2, 0.0)
            at_ref[0, rs, pr * 2 * half:(pr + 1) * 2 * half] = (qk2 * decay2).astype(BF16)
            m = jnp.concatenate([jnp.where(left, m2, 0.0), jnp.where(left, 0.0, m2)], axis=0)
            ms.append(m)
            tinvs.append(eye2 - m)
            ega, egb = jnp.exp(gca), jnp.exp(gcb)
            va = vs_ref[rs, ha * hd:(ha + 1) * hd]
            vb = vs_ref[rs, hb * hd:(hb + 1) * hd]
            rhss.append(jnp.concatenate([jnp.concatenate([va * ba, kc * (ba * ega)], axis=1),
                                         jnp.concatenate([vb * bb, kc * (bb * egb)], axis=1)],
                                        axis=0).astype(BF16))
            qg_ref[0, rs, osl] = jnp.concatenate([qc * ega, qc * egb], axis=1).astype(BF16)
            kg_ref[0, rs, osl] = jnp.concatenate([kc * jnp.exp(glast[:, ha:ha + 1] - gca),
                                                  kc * jnp.exp(glast[:, hb:hb + 1] - gcb)], axis=1).astype(BF16)
        pws = [bdot(m, m) for m in ms]
        n_fac = CHUNK.bit_length() - 2
        for f in range(n_fac):
            if f + 1 < n_fac:
                both = [bdot(jnp.concatenate([t, p], axis=0), p) for t, p in zip(tinvs, pws)]
                tinvs = [t + r[:2 * half] for t, r in zip(tinvs, both)]
                pws = [r[2 * half:] for r in both]
            else:
                tinvs = [t + bdot(t, p) for t, p in zip(tinvs, pws)]
        sols = [jnp.dot(t.astype(BF16), r, preferred_element_type=F32) for t, r in zip(tinvs, rhss)]
        for pr, sol in zip(pairs, sols):
            osl = slice(2 * pr * hd, (2 * pr + 2) * hd)
            u_ref[0, rs, osl] = jnp.concatenate([sol[:CHUNK, :hd], sol[CHUNK:, :hd]], axis=1)
            w_ref[0, rs, osl] = jnp.concatenate([sol[:CHUNK, hd:], sol[CHUNK:, hd:]], axis=1).astype(BF16)


def _gdn_prep(q, k, v, a, b, conv_w, a_log, dt_bias, t_real):
    B, TP, _ = q.shape
    tt = TIME_TILE
    cw = jnp.pad(conv_w.astype(F32), ((0, SUBLANES - CONV_WIDTH), (0, 0)))
    cwq, cwk, cwv = cw[:, :GDN_QK_WIDTH], cw[:, GDN_QK_WIDTH:2 * GDN_QK_WIDTH], cw[:, 2 * GDN_QK_WIDTH:]
    alog = jnp.pad(a_log.astype(F32), (0, LANES - GDN_V_HEADS)).reshape(1, LANES)
    dtb = jnp.pad(dt_bias.astype(F32), (0, LANES - GDN_V_HEADS)).reshape(1, LANES)
    tspec = lambda n: pl.BlockSpec((1, tt, n), lambda bb, j: (bb, j, 0))
    ws = (cwq, cwk, cwv, alog, dtb)
    VW, QW = GDN_V_WIDTH, GDN_QK_WIDTH
    outs = [(VW, F32), (VW, BF16), (VW, BF16), (VW, BF16), (GDN_K_HEADS * 2 * CHUNK, BF16), (LANES, F32)]
    return pl.pallas_call(
        functools.partial(_gdn_prep_kernel, t_real=t_real),
        grid=(B, TP // tt),
        in_specs=[tspec(QW), tspec(QW), tspec(VW), tspec(LANES), tspec(LANES)] + [_const_spec(w.shape) for w in ws],
        out_specs=[tspec(n) for n, _ in outs],
        out_shape=[jax.ShapeDtypeStruct((B, TP, n), dt) for n, dt in outs],
        scratch_shapes=[pltpu.VMEM((SUBLANES + tt, QW), F32), pltpu.VMEM((SUBLANES + tt, QW), F32),
                        pltpu.VMEM((SUBLANES + tt, VW), F32), pltpu.VMEM((tt, QW), F32), pltpu.VMEM((tt, QW), F32),
                        pltpu.VMEM((tt, VW), F32)],
        compiler_params=_cparams("parallel", "arbitrary"),
        name="gdn_prep",
    )(q, k, v, a, b, *ws)


def _gdn_scan_kernel(u_ref, w_ref, qg_ref, kg_ref, at_ref, e_ref, z_ref, on_ref, o_ref, s_ref):
    j = pl.program_id(1)
    tt = u_ref.shape[1]
    hd = GDN_HEAD_DIM

    @pl.when(j == 0)
    def _():
        s_ref[...] = jnp.zeros_like(s_ref)

    lane = lax.broadcasted_iota(I32, (CHUNK, 2 * CHUNK), 1)
    tn = (((0,), (0,)), ((), ()))
    heads = range(GDN_V_HEADS)
    hsl = [slice(h * hd, (h + 1) * hd) for h in heads]
    for c in range(tt // CHUNK):
        rs = slice(c * CHUNK, (c + 1) * CHUNK)
        e_row = e_ref[0, c * CHUNK:c * CHUNK + 1, :]
        ss = [s_ref[h] for h in heads]
        ws = [jnp.dot(jnp.concatenate([w_ref[0, rs, hsl[h]], qg_ref[0, rs, hsl[h]]], axis=0), ss[h].astype(BF16),
                      preferred_element_type=F32) for h in heads]
        v16 = [(u_ref[0, rs, hsl[h]] - ws[h][:CHUNK]).astype(BF16) for h in heads]
        intra = []
        for pr in range(GDN_K_HEADS):
            at2 = at_ref[0, rs, pr * 2 * CHUNK:(pr + 1) * 2 * CHUNK]
            zero = jnp.zeros_like(at2)
            lhs = jnp.concatenate([jnp.where(lane < CHUNK, at2, zero), jnp.where(lane >= CHUNK, at2, zero)], axis=0)
            rhs = jnp.concatenate([v16[2 * pr], v16[2 * pr + 1]], axis=0)
            intra.append(jnp.dot(lhs, rhs, preferred_element_type=F32))
        for h in heads:
            s_ref[h] = ss[h] * e_row[:, h:h + 1] + lax.dot_general(kg_ref[0, rs, hsl[h]], v16[h], tn,
                                                                   preferred_element_type=F32)
        for h in heads:
            o = ws[h][CHUNK:] + intra[h // 2][(h % 2) * CHUNK:(h % 2 + 1) * CHUNK]
            on = o * lax.rsqrt(jnp.mean(o * o, axis=1, keepdims=True) + 1e-6) * on_ref[...]
            z = z_ref[0, rs, hsl[h]].astype(F32)
            o_ref[0, rs, hsl[h]] = (on * (z * jax.nn.sigmoid(z))).astype(BF16)


def _gdn_scan(u, w, qg, kg, at, e, z, o_norm):
    B, TP, VW = u.shape
    tt = TIME_TILE
    tspec = lambda n: pl.BlockSpec((1, tt, n), lambda bb, j: (bb, j, 0))
    on = o_norm.reshape(1, GDN_HEAD_DIM).astype(F32)
    return pl.pallas_call(
        _gdn_scan_kernel,
        grid=(B, TP // tt),
        in_specs=[tspec(VW), tspec(VW), tspec(VW), tspec(VW), tspec(at.shape[-1]), tspec(LANES), tspec(VW),
                  _const_spec(on.shape)],
        out_specs=tspec(VW),
        out_shape=jax.ShapeDtypeStruct((B, TP, VW), BF16),
        scratch_shapes=[pltpu.VMEM((GDN_V_HEADS, GDN_HEAD_DIM, GDN_HEAD_DIM), F32)],
        compiler_params=_cparams("parallel", "arbitrary"),
        name="gdn_scan",
    )(u, w, qg, kg, at, e, z, on)


def _odd_layer(h, w_in, conv_w, a_log, dt_bias, o_norm, w_out, ln_g, ln_b, t_real):
    q, k, v, z, a, b = _odd_in_proj(h, w_in)
    u, w, qg, kg, at, e = _gdn_prep(q, k, v, a, b, conv_w, a_log, dt_bias, t_real)
    gated = _gdn_scan(u, w, qg, kg, at, e, z, o_norm)
    return _out_ln(h, [gated], [w_out], ln_g, ln_b, t_real)


def kernel(x, meta_tokens, even_w_in, even_kv_norm, even_w_uk, even_w_uv, even_conv_w, even_conv_b, even_rg_wa,
           even_rg_ba, even_rg_wx, even_rg_bx, even_rg_lambda, even_w_out, odd_w_in, odd_conv_w, odd_a_log,
           odd_dt_bias, odd_o_norm, odd_w_out, ln_g, ln_b, moe_group_w, moe_group_b, moe_expert_w, moe_expert_b,
           moe_w_gate, moe_w_up, moe_w_down):
    B, S, D = x.shape
    t_real = N_META + S
    tp = -(-(FRONT + t_real) // TIME_TILE) * TIME_TILE
    meta = jnp.broadcast_to(meta_tokens.astype(x.dtype)[None], (B, N_META, D))
    h = jnp.concatenate([jnp.zeros((B, FRONT, D), x.dtype), meta, x,
                         jnp.zeros((B, tp - FRONT - t_real, D), x.dtype)], axis=1)
    for layer in range(DEPTH):
        i = layer // 2
        if layer % 2 == 0:
            h = _even_layer(h, even_w_in[i], even_kv_norm[i], even_w_uk[i], even_w_uv[i], even_conv_w[i],
                            even_conv_b[i], even_rg_wa[i], even_rg_ba[i], even_rg_wx[i], even_rg_bx[i],
                            even_rg_lambda[i], even_w_out[i], ln_g[layer, 0], ln_b[layer, 0], t_real)
        else:
            h = _odd_layer(h, odd_w_in[i], odd_conv_w[i], odd_a_log[i], odd_dt_bias[i], odd_o_norm[i],
                           odd_w_out[i], ln_g[layer, 0], ln_b[layer, 0], t_real)
        h = _moe_layer(h, moe_group_w[layer], moe_group_b[layer], moe_expert_w[layer], moe_expert_b[layer],
                       moe_w_gate[layer], moe_w_up[layer], moe_w_down[layer], ln_g[layer, 1], ln_b[layer, 1], t_real)
    return h[:, FRONT:FRONT + t_real][:, N_META:]
```

```python
import functools
import math

import jax
import jax.numpy as jnp
from jax import lax
from jax.experimental import pallas as pl
from jax.experimental.pallas import tpu as pltpu

F32 = jnp.float32
BF16 = jnp.bfloat16
I32 = jnp.int32

D_MODEL = 1024
DEPTH = 2
N_META = 16
DN_ALPHA = (2 * DEPTH) ** 0.25

A_HEADS = 4
A_HEAD_DIM = 128
A_OUT = A_HEADS * A_HEAD_DIM
A_KV_RANK = 256
IDX_HEADS = 8
IDX_DIM = 64
IDX_TOPK_CAP = 256

RG_WIDTH = 512
RG_BLOCKS = 8
RG_C = 8.0
CONV_WIDTH = 4

GDN_K_HEADS = 8
GDN_V_HEADS = 16
GDN_HEAD_DIM = 128
GDN_QK_WIDTH = GDN_K_HEADS * GDN_HEAD_DIM
GDN_V_WIDTH = GDN_V_HEADS * GDN_HEAD_DIM
CHUNK = 64

MOE_GROUPS = 4
MOE_PER_GROUP = 8
N_EXPERTS = MOE_GROUPS * MOE_PER_GROUP
MOE_TOPK = 2
EXPERT_FF = 512
MOE_BLOCK = 256
FFN_BLOCK = 512

LANES = 128
SUBLANES = 8
TIME_TILE = 128
FRONT = (-N_META) % CHUNK
NEG_BIG = -1e30
INT_MIN = -2 ** 31
VMEM_LIMIT = 56 * 1024 * 1024


def _cparams(*sem, flags=None):
    return pltpu.CompilerParams(dimension_semantics=sem, vmem_limit_bytes=VMEM_LIMIT, flags=flags)


def _const_spec(shape):
    nd = len(shape)
    return pl.BlockSpec(shape, lambda *_: (0,) * nd)


def _row_tile(tp):
    best = SUBLANES
    for t in range(SUBLANES, 641, SUBLANES):
        if tp % t == 0:
            best = t
    return best


def _valid_rows(p, t_real):
    return (p >= FRONT) & (p < FRONT + t_real)


def _layer_norm_rows(y, g, b):
    mu = jnp.mean(y, axis=-1, keepdims=True)
    yc = y - mu
    var = jnp.mean(yc * yc, axis=-1, keepdims=True)
    return yc * lax.rsqrt(var + 1e-5) * g + b


def _softplus(z):
    return jnp.maximum(z, 0.0) + jnp.log(1.0 + jnp.exp(-jnp.abs(z)))


def _even_in_kernel(h_ref, wq_ref, wc_ref, wqi_ref, wki_ref, wwi_ref, wg_ref, wx_ref, kvn_ref, wukv_ref,
                    q_ref, qi_ref, ki_ref, wi_ref, gate_ref, xb_ref, kv_ref):
    a = h_ref[0].astype(BF16)
    dot = functools.partial(jnp.dot, preferred_element_type=F32)
    q_ref[0] = dot(a, wq_ref[...]).astype(BF16)
    qi_ref[0] = dot(a, wqi_ref[...]).astype(BF16)
    ki_ref[0] = dot(a, wki_ref[...]).astype(BF16)
    wi_ref[0] = dot(a, wwi_ref[...]) * (IDX_HEADS ** -0.5) * (IDX_DIM ** -0.5)
    gate_ref[0] = dot(a, wg_ref[...])
    xb_ref[0] = dot(a, wx_ref[...])
    c = dot(a, wc_ref[...])
    latent = c * lax.rsqrt(jnp.mean(c * c, axis=-1, keepdims=True) + 1e-6) * kvn_ref[...]
    kv_ref[0] = dot(latent.astype(BF16), wukv_ref[...]).astype(BF16)


def _even_in_proj(h, w_in, kv_norm, w_uk, w_uv):
    B, TP, D = h.shape
    tm = _row_tile(TP)
    o = [0]
    for p in (A_OUT, A_KV_RANK, IDX_HEADS * IDX_DIM, IDX_DIM, IDX_HEADS, RG_WIDTH, RG_WIDTH):
        o.append(o[-1] + p)
    wb = w_in.astype(BF16)
    wq, wc = wb[:, o[0]:o[1]], wb[:, o[1]:o[2]]
    wqi = wb[:, o[2]:o[3]].reshape(D, IDX_HEADS, IDX_DIM)
    wqi = jnp.pad(wqi, ((0, 0), (0, 0), (0, LANES - IDX_DIM))).reshape(D, IDX_HEADS * LANES)
    wki = jnp.pad(wb[:, o[3]:o[4]], ((0, 0), (0, LANES - IDX_DIM)))
    wwi = jnp.pad(wb[:, o[4]:o[5]], ((0, 0), (0, LANES - IDX_HEADS)))
    wg, wx = wb[:, o[5]:o[6]], wb[:, o[6]:o[7]]
    wukv = jnp.concatenate([w_uk, w_uv], axis=1).astype(BF16)
    kvn = kv_norm.reshape(1, A_KV_RANK).astype(F32)
    ws = (wq, wc, wqi, wki, wwi, wg, wx, kvn, wukv)
    outs = [(A_OUT, BF16), (IDX_HEADS * LANES, BF16), (LANES, BF16), (LANES, F32), (RG_WIDTH, F32),
            (RG_WIDTH, F32), (2 * A_HEAD_DIM, BF16)]
    return pl.pallas_call(
        _even_in_kernel,
        grid=(B, TP // tm),
        in_specs=[pl.BlockSpec((1, tm, D), lambda b, j: (b, j, 0))] + [_const_spec(w.shape) for w in ws],
        out_specs=[pl.BlockSpec((1, tm, n), lambda b, j: (b, j, 0)) for n, _ in outs],
        out_shape=[jax.ShapeDtypeStruct((B, TP, n), dt) for n, dt in outs],
        compiler_params=_cparams("parallel", "parallel"),
        name="even_in_proj",
    )(h, *ws)


def _dsa_body(nk, q_ref, qi_ref, wi_ref, ki_ref, kv_ref, o_ref, key_ref, bias_ref, vt_ref, n_sel, t_real):
    i = pl.program_id(1)
    tq = q_ref.shape[1]
    qi = qi_ref[0]
    ki = ki_ref[0, :nk, :]
    wit = wi_ref[0].T
    nt = (((1,), (1,)), ((), ()))
    score = jnp.zeros((nk, tq), F32)
    for h in range(IDX_HEADS):
        s = lax.dot_general(ki, qi[:, h * LANES:(h + 1) * LANES], nt, preferred_element_type=F32)
        score = score + wit[h:h + 1, :] * jnp.maximum(s, 0.0)

    kpos = lax.broadcasted_iota(I32, (nk, tq), 0)
    qpos = i * tq + lax.broadcasted_iota(I32, (nk, tq), 1)
    bits = lax.bitcast_convert_type(score, I32)
    key = jnp.where(bits < 0, bits ^ jnp.int32(0x7FFFFFFF), bits)
    key_ref[:nk, :] = jnp.where(kpos <= qpos, key, jnp.int32(INT_MIN))
    key_ref[0:FRONT, :] = jnp.full((FRONT, tq), INT_MIN, I32)
    if nk > FRONT + t_real:
        key_ref[FRONT + t_real:nk, :] = jnp.full((nk - FRONT - t_real, tq), INT_MIN, I32)

    def over_keys(x, op):
        pair = jnp.add if op is jnp.sum else jnp.maximum
        group = 8
        x = x.reshape(nk // (group * SUBLANES), group, SUBLANES, x.shape[1])
        while x.shape[1] > 1:
            half = x.shape[1] // 2
            x = pair(x[:, :half], x[:, half:])
        return op(op(x[:, 0], axis=0), axis=0, keepdims=True)

    def count(mask):
        return over_keys(jnp.where(mask, 1, 0).astype(I32), jnp.sum)

    def any_query(mask):
        return jnp.max(jnp.where(mask, 1, 0))

    n_vis = count(key_ref[:nk, :] > jnp.int32(INT_MIN))

    first_bits, later_bits = 20, 4

    def bit_cond(c):
        return (c[0] < 32) & (c[3] > 0)

    def bit_group(c):
        it0, cand, cnt_c, _ = c
        n_bits = jnp.where(it0 == 0, first_bits, later_bits)

        def bit_step(b, cc):
            cand, cnt_c = cc
            trial = cand + jnp.left_shift(jnp.int32(1), 31 - (it0 + b))
            cnt = count(key_ref[:nk, :] >= trial)
            take = cnt >= n_sel
            return jnp.where(take, trial, cand), jnp.where(take, cnt, cnt_c)

        cand, cnt_c = lax.fori_loop(0, n_bits, bit_step, (cand, cnt_c))
        return it0 + n_bits, cand, cnt_c, any_query((cnt_c != n_sel) & (n_vis > n_sel))

    start = (jnp.int32(0), jnp.full((1, tq), INT_MIN, I32), jnp.full((1, tq), nk, I32),
             any_query(n_vis > n_sel))
    _, thr, cnt_ge, _ = lax.while_loop(bit_cond, bit_group, start)
    keys = key_ref[:nk, :]
    bias_ref[:nk, :] = jnp.where(keys >= jnp.maximum(thr, jnp.int32(INT_MIN + 1)), 0.0, NEG_BIG)
    tie_queries = (cnt_ge > n_sel) & (thr > jnp.int32(INT_MIN))

    @pl.when(any_query(tie_queries) > 0)
    def _():
        keys = key_ref[:nk, :]
        eq = keys == thr
        need = n_sel - count(keys > thr)
        nbits = max(1, (nk - 1).bit_length())

        def idx_step(it, cand):
            trial = cand + jnp.left_shift(jnp.int32(1), nbits - 1 - it)
            return jnp.where(count(eq & (kpos < trial)) < need, trial, cand)

        last = lax.fori_loop(0, nbits, idx_step, jnp.zeros((1, tq), I32))
        sel = (keys > thr) | (eq & (kpos <= last))
        bias_ref[:nk, :] = jnp.where(sel & (keys > jnp.int32(INT_MIN)), 0.0, NEG_BIG)

    bias = bias_ref[:nk, :]
    q = q_ref[0]
    k = kv_ref[0, :nk, :A_HEAD_DIM]
    qs = jnp.concatenate([q[:, h * A_HEAD_DIM:(h + 1) * A_HEAD_DIM] for h in range(A_HEADS)], axis=0)
    lg = lax.dot_general(k, qs, nt, preferred_element_type=F32) * (A_HEAD_DIM ** -0.5)
    lg = lg + jnp.concatenate([bias] * A_HEADS, axis=1)
    m = over_keys(lg, jnp.max)
    p = jnp.exp(lg - m)
    l = over_keys(p, jnp.sum)
    ot = jnp.dot(vt_ref[:, :nk], p.astype(BF16), preferred_element_type=F32) / l
    for h in range(A_HEADS):
        o_ref[0, :, h * A_HEAD_DIM:(h + 1) * A_HEAD_DIM] = ot[:, h * tq:(h + 1) * tq].T.astype(BF16)


DSA_WIDTH_STEP = 2


def _dsa_kernel(q_ref, qi_ref, wi_ref, ki_ref, kv_ref, o_ref, key_ref, bias_ref, vt_ref, *, n_sel, t_real):
    i = pl.program_id(1)
    tq = q_ref.shape[1]
    tp = ki_ref.shape[1]
    n_tiles = tp // tq

    @pl.when(i == 0)
    def _():
        vt_ref[...] = kv_ref[0, :, A_HEAD_DIM:].astype(F32).T.astype(BF16)

    lo = 0
    while lo < n_tiles:
        hi = min(lo + DSA_WIDTH_STEP, n_tiles)

        @pl.when((i >= lo) & (i < hi))
        def _(hi=hi):
            _dsa_body(hi * tq, q_ref, qi_ref, wi_ref, ki_ref, kv_ref, o_ref, key_ref, bias_ref, vt_ref,
                      n_sel, t_real)

        lo = hi


def _dsa(q, qi, wi, ki, kv, t_real):
    B, TP, _ = q.shape
    tq = TIME_TILE
    n_sel = min(IDX_TOPK_CAP, t_real // 4)
    kern = functools.partial(_dsa_kernel, n_sel=n_sel, t_real=t_real)
    qspec = lambda n: pl.BlockSpec((1, tq, n), lambda b, i: (b, i, 0))
    kspec = lambda n: pl.BlockSpec((1, TP, n), lambda b, i: (b, 0, 0))
    return pl.pallas_call(
        kern,
        grid=(B, TP // tq),
        in_specs=[qspec(A_OUT), qspec(IDX_HEADS * LANES), qspec(LANES), kspec(LANES), kspec(2 * A_HEAD_DIM)],
        out_specs=qspec(A_OUT),
        out_shape=jax.ShapeDtypeStruct((B, TP, A_OUT), BF16),
        scratch_shapes=[pltpu.VMEM((TP, tq), I32), pltpu.VMEM((TP, tq), F32), pltpu.VMEM((A_HEAD_DIM, TP), BF16)],
        compiler_params=_cparams("parallel", "arbitrary"),
        name="dsa_attention",
    )(q, qi, wi, ki, kv)


def _shift_rows(x, s, fill):
    rows = lax.broadcasted_iota(I32, x.shape, 0)
    return jnp.where(rows >= s, pltpu.roll(x, s, 0), fill)


def _causal_conv(x, win_ref, cw):
    tt = x.shape[0]
    win_ref[0:SUBLANES, :] = win_ref[tt:tt + SUBLANES, :]
    win_ref[SUBLANES:, :] = x
    y = cw[CONV_WIDTH - 1:CONV_WIDTH, :] * x
    for t in range(CONV_WIDTH - 1):
        off = SUBLANES - (CONV_WIDTH - 1) + t
        y = y + cw[t:t + 1, :] * win_ref[off:off + tt, :]
    return y


def _griffin_kernel(xb_ref, gate_ref, cw_ref, cb_ref, wg_ref, bg_ref, lam_ref, o_ref, tail_ref, car_ref, *, t_real):
    j = pl.program_id(1)
    tt = xb_ref.shape[1]

    @pl.when(j == 0)
    def _():
        tail_ref[...] = jnp.zeros_like(tail_ref)
        car_ref[...] = jnp.zeros_like(car_ref)

    xr = _causal_conv(xb_ref[0], tail_ref, cw_ref[...]) + cb_ref[...]
    g = jnp.dot(xr.astype(BF16), wg_ref[...], preferred_element_type=F32) + bg_ref[...]
    r = jax.nn.sigmoid(g[:, :RG_WIDTH])
    ig = jax.nn.sigmoid(g[:, RG_WIDTH:])
    log_a = -RG_C * r * _softplus(-lam_ref[...])
    a = jnp.exp(log_a)
    th = jnp.tanh(log_a)
    u = jnp.sqrt(-2.0 * th / (1.0 - th)) * (ig * xr)
    p = j * tt + lax.broadcasted_iota(I32, xr.shape, 0)
    u = jnp.where(_valid_rows(p, t_real), u, 0.0)
    s = 1
    while s < tt:
        u = u + a * _shift_rows(u, s, 0.0)
        a = a * _shift_rows(a, s, 1.0)
        s *= 2
    hcur = u + a * car_ref[0:1, :]
    car_ref[...] = jnp.broadcast_to(hcur[tt - 1:tt, :], car_ref.shape)
    o_ref[0] = hcur * jax.nn.gelu(gate_ref[0])


def _block_diag(w):
    nb, bs, _ = w.shape
    eye = jnp.eye(nb, dtype=w.dtype)
    return (w[:, :, None, :] * eye[:, None, :, None]).reshape(nb * bs, nb * bs)


def _griffin(xb, gate, conv_w, conv_b, rg_wa, rg_ba, rg_wx, rg_bx, rg_lambda, t_real):
    B, TP, R = xb.shape
    tt = TIME_TILE
    cw = jnp.pad(conv_w.astype(F32), ((0, SUBLANES - CONV_WIDTH), (0, 0)))
    cb = conv_b.reshape(1, R).astype(F32)
    wg = jnp.concatenate([_block_diag(rg_wa), _block_diag(rg_wx)], axis=1).astype(BF16)
    bg = jnp.concatenate([rg_ba, rg_bx]).reshape(1, 2 * R).astype(F32)
    lam = rg_lambda.reshape(1, R).astype(F32)
    tspec = pl.BlockSpec((1, tt, R), lambda b, j: (b, j, 0))
    ws = (cw, cb, wg, bg, lam)
    return pl.pallas_call(
        functools.partial(_griffin_kernel, t_real=t_real),
        grid=(B, TP // tt),
        in_specs=[tspec, tspec] + [_const_spec(w.shape) for w in ws],
        out_specs=tspec,
        out_shape=jax.ShapeDtypeStruct((B, TP, R), F32),
        scratch_shapes=[pltpu.VMEM((SUBLANES + tt, R), F32), pltpu.VMEM((SUBLANES, R), F32)],
        compiler_params=_cparams("parallel", "arbitrary"),
        name="griffin_rglru",
    )(xb, gate, *ws)


def _out_ln_kernel(*refs, n_in, t_real):
    h_ref = refs[0]
    a_refs = refs[1:1 + n_in]
    w_refs = refs[1 + n_in:1 + 2 * n_in]
    g_ref, b_ref, o_ref = refs[1 + 2 * n_in:]
    j = pl.program_id(1)
    tm = h_ref.shape[1]
    n_chunks = 4 if tm % (4 * SUBLANES) == 0 else 1
    tc = tm // n_chunks
    mixes = []
    for c in range(n_chunks):
        rs = slice(c * tc, (c + 1) * tc)
        mix = None
        for a_ref, w_ref in zip(a_refs, w_refs):
            d = jnp.dot(a_ref[0, rs, :].astype(BF16), w_ref[...], preferred_element_type=F32)
            mix = d if mix is None else mix + d
        mixes.append(mix)
    for c in range(n_chunks):
        rs = slice(c * tc, (c + 1) * tc)
        out = _layer_norm_rows(DN_ALPHA * h_ref[0, rs, :] + mixes[c], g_ref[...], b_ref[...])
        p = j * tm + c * tc + lax.broadcasted_iota(I32, out.shape, 0)
        o_ref[0, rs, :] = jnp.where(_valid_rows(p, t_real), out, 0.0)


def _out_ln(h, acts, ws, ln_g, ln_b, t_real):
    B, TP, D = h.shape
    tm = _row_tile(TP)
    ws = [w.astype(BF16) for w in ws]
    g = ln_g.reshape(1, D).astype(F32)
    b = ln_b.reshape(1, D).astype(F32)
    tspec = lambda n: pl.BlockSpec((1, tm, n), lambda bb, j: (bb, j, 0))
    return pl.pallas_call(
        functools.partial(_out_ln_kernel, n_in=len(acts), t_real=t_real),
        grid=(B, TP // tm),
        in_specs=[tspec(D)] + [tspec(a.shape[-1]) for a in acts] + [_const_spec(w.shape) for w in ws]
        + [_const_spec(g.shape), _const_spec(b.shape)],
        out_specs=tspec(D),
        out_shape=jax.ShapeDtypeStruct((B, TP, D), F32),
        compiler_params=_cparams("parallel", "parallel"),
        name="out_proj_ln",
    )(h, *acts, *ws, g, b)


E_LANE0, G_LANE0 = 0, N_EXPERTS
R_E1, R_E2, R_RANK1, R_RANK2, R_G1, R_G2 = range(6)


def _router_kernel(h_ref, w_ref, b_ref, rec_ref, rect_ref, cnt_ref, car_ref):
    t = pl.program_id(0)

    @pl.when(t == 0)
    def _():
        car_ref[...] = jnp.zeros_like(car_ref)

    tm = h_ref.shape[0]
    h = h_ref[...]
    h_hi = h.astype(BF16)
    h_lo = (h - h_hi.astype(F32)).astype(BF16)
    both = jnp.dot(h_hi, w_ref[...], preferred_element_type=F32)
    lg = (both[:, :LANES] + both[:, LANES:]
          + jnp.dot(h_lo, w_ref[:, :LANES], preferred_element_type=F32) + b_ref[...])
    lane = lax.broadcasted_iota(I32, lg.shape, 1)
    neg_inf = -jnp.inf

    def first_argmax(x):
        m = jnp.max(x, axis=1, keepdims=True)
        return m, jnp.min(jnp.where(x == m, lane, jnp.int32(2 ** 30)), axis=1, keepdims=True)

    glog = jnp.where((lane >= G_LANE0) & (lane < G_LANE0 + MOE_GROUPS), lg, neg_inf)
    gmax, glane = first_argmax(glog)
    grp = glane - G_LANE0
    grp_gate = 1.0 / jnp.sum(jnp.exp(glog - gmax), axis=1, keepdims=True)
    in_grp = (lane >= grp * MOE_PER_GROUP) & (lane < (grp + 1) * MOE_PER_GROUP)
    elog = jnp.where(in_grp, lg, neg_inf)
    v1, e1 = first_argmax(elog)
    v2, e2 = first_argmax(jnp.where(lane == e1, neg_inf, elog))
    ex = jnp.exp(v2 - v1)
    g1 = grp_gate / (1.0 + ex)
    g2 = grp_gate * ex / (1.0 + ex)

    oh1 = jnp.where(lane == e1, 1.0, 0.0)
    oh2 = jnp.where(lane == e2, 1.0, 0.0)
    rows = lax.broadcasted_iota(I32, (tm, tm), 0)
    cols = lax.broadcasted_iota(I32, (tm, tm), 1)
    ltri = jnp.where(cols < rows, 1.0, 0.0).astype(BF16)
    p1 = jnp.dot(ltri, oh1.astype(BF16), preferred_element_type=F32)
    p2 = jnp.dot(ltri, oh2.astype(BF16), preferred_element_type=F32)
    car = car_ref[0:1, :]
    c1 = jnp.sum(oh1, axis=0, keepdims=True)
    c2 = jnp.sum(oh2, axis=0, keepdims=True)
    rank1 = jnp.sum(oh1 * (car + p1), axis=1, keepdims=True)
    rank2 = jnp.sum(oh2 * (car + c1 + p2), axis=1, keepdims=True)
    car = car + c1 + c2
    car_ref[...] = jnp.broadcast_to(car, car_ref.shape)
    cnt_ref[...] = jnp.broadcast_to(car, cnt_ref.shape)

    rec = jnp.zeros(lg.shape, F32)
    for ln, val in ((R_E1, e1.astype(F32)), (R_E2, e2.astype(F32)), (R_RANK1, rank1), (R_RANK2, rank2),
                    (R_G1, g1), (R_G2, g2)):
        rec = jnp.where(lane == ln, val, rec)
    rec_ref[...] = rec
    rect_ref[0] = rec.T[:SUBLANES, :]


def _moe_route(hf, group_w, group_b, expert_w, expert_b):
    N, D = hf.shape
    tm = 2 * MOE_BLOCK if N % (2 * MOE_BLOCK) == 0 else MOE_BLOCK
    nt = N // tm
    w = jnp.zeros((D, LANES), F32)
    w = w.at[:, E_LANE0:E_LANE0 + N_EXPERTS].set(expert_w).at[:, G_LANE0:G_LANE0 + MOE_GROUPS].set(group_w)
    b = jnp.zeros((1, LANES), F32)
    b = b.at[0, E_LANE0:E_LANE0 + N_EXPERTS].set(expert_b).at[0, G_LANE0:G_LANE0 + MOE_GROUPS].set(group_b)
    w_hi = w.astype(BF16)
    w = jnp.concatenate([w_hi, (w - w_hi.astype(F32)).astype(BF16)], axis=1)
    return pl.pallas_call(
        _router_kernel,
        grid=(nt,),
        in_specs=[pl.BlockSpec((tm, D), lambda t: (t, 0)), _const_spec(w.shape), _const_spec(b.shape)],
        out_specs=[pl.BlockSpec((tm, LANES), lambda t: (t, 0)),
                   pl.BlockSpec((1, SUBLANES, tm), lambda t: (t, 0, 0)),
                   _const_spec((SUBLANES, LANES))],
        out_shape=[jax.ShapeDtypeStruct((N, LANES), F32),
                   jax.ShapeDtypeStruct((nt, SUBLANES, tm), F32),
                   jax.ShapeDtypeStruct((SUBLANES, LANES), F32)],
        scratch_shapes=[pltpu.VMEM((SUBLANES, LANES), F32)],
        compiler_params=_cparams("arbitrary"),
        name="moe_router",
    )(hf, w, b)


def _row_copy(src, si, dst, di, sem):
    return pltpu.make_async_copy(src.at[pl.ds(si, 1)], dst.at[pl.ds(di, 1)], sem)


def _route_copy(idx_hbm, idx_smem, sems, t):
    return pltpu.make_async_copy(idx_hbm.at[t], idx_smem.at[t % 2], sems.at[t % 2])


def _slot(idx_smem, t, k, r):
    return idx_smem[t % 2, k * MOE_BLOCK + r]


def _slots_kernel(rect_ref, ps_ref, o_ref):
    rec = rect_ref[0]
    erow = lax.broadcasted_iota(I32, (N_EXPERTS, rec.shape[1]), 0).astype(F32)
    rows = []
    for k in range(MOE_TOPK):
        e = rec[R_E1 + k:R_E1 + k + 1, :]
        base = jnp.sum(jnp.where(erow == e, ps_ref[...], 0.0), axis=0, keepdims=True)
        rows.append(rec[R_RANK1 + k:R_RANK1 + k + 1, :] + base)
    rows.append(jnp.zeros((SUBLANES - MOE_TOPK, rec.shape[1]), F32))
    o_ref[0] = jnp.concatenate(rows, axis=0).astype(I32)


def _moe_slots(rect, pstart):
    nt, _, tm = rect.shape
    ps = pstart.astype(F32).reshape(N_EXPERTS, 1)
    return pl.pallas_call(
        _slots_kernel,
        grid=(nt,),
        in_specs=[pl.BlockSpec((1, SUBLANES, tm), lambda t: (t, 0, 0)), _const_spec(ps.shape)],
        out_specs=pl.BlockSpec((1, SUBLANES, tm), lambda t: (t, 0, 0)),
        out_shape=jax.ShapeDtypeStruct((nt, SUBLANES, tm), I32),
        compiler_params=_cparams("parallel"),
        name="moe_slots",
    )(rect, ps)


N_HBUF = 3


def _dispatch_kernel(idx_hbm, h_hbm, xs_in, xs_hbm, idx_smem, hbuf, isems, hsems, ssems):
    del xs_in
    t = pl.program_id(0)
    nt = pl.num_programs(0)
    tm = MOE_BLOCK

    def tile_copy(u):
        return pltpu.make_async_copy(h_hbm.at[pl.ds(u * tm, tm)], hbuf.at[u % N_HBUF], hsems.at[u % N_HBUF])

    def wait_scatters(u):
        for _ in range(tm * MOE_TOPK):
            _row_copy(hbuf.at[0], 0, xs_hbm, 0, ssems.at[u % 2]).wait()

    @pl.when(t == 0)
    def _():
        _route_copy(idx_hbm, idx_smem, isems, t).start()
        tile_copy(t).start()

    _route_copy(idx_hbm, idx_smem, isems, t).wait()
    tile_copy(t).wait()

    @pl.when(t + 1 < nt)
    def _():
        _route_copy(idx_hbm, idx_smem, isems, t + 1).start()
        tile_copy(t + 1).start()

    src = hbuf.at[t % N_HBUF]
    for r in range(tm):
        for k in range(MOE_TOPK):
            _row_copy(src, r, xs_hbm, _slot(idx_smem, t, k, r), ssems.at[t % 2]).start(priority=k % 2)

    @pl.when(t > 0)
    def _():
        wait_scatters(t - 1)

    @pl.when(t == nt - 1)
    def _():
        wait_scatters(t)


def _moe_dispatch(hf, idx, n_slots):
    N, D = hf.shape
    nt = N // MOE_BLOCK
    xs0 = jnp.zeros((n_slots, D), F32)
    return pl.pallas_call(
        _dispatch_kernel,
        grid=(nt,),
        in_specs=[pl.BlockSpec(memory_space=pl.ANY)] * 3,
        out_specs=pl.BlockSpec(memory_space=pl.ANY),
        scratch_shapes=[pltpu.SMEM((2, MOE_TOPK * MOE_BLOCK), I32),
                        pltpu.VMEM((N_HBUF, MOE_BLOCK, D), F32),
                        pltpu.SemaphoreType.DMA((2,)), pltpu.SemaphoreType.DMA((N_HBUF,)),
                        pltpu.SemaphoreType.DMA((2,))],
        out_shape=jax.ShapeDtypeStruct((n_slots, D), F32),
        input_output_aliases={2: 0},
        compiler_params=_cparams("arbitrary"),
        name="moe_dispatch",
    )(idx, hf, xs0)


def _ffn_kernel(be_ref, nb_ref, x_ref, wg_ref, wu_ref, wd_ref, o_ref, wg16, wu16, wd16):
    b = pl.program_id(0)

    @pl.when((b == 0) | (be_ref[b] != be_ref[jnp.maximum(b - 1, 0)]))
    def _():
        wg16[...] = wg_ref[...].astype(BF16)
        wu16[...] = wu_ref[...].astype(BF16)
        wd16[...] = wd_ref[...].astype(BF16)

    @pl.when(b < nb_ref[0])
    def _():
        n_chunks = x_ref.shape[0] // TIME_TILE
        rows = [slice(c * TIME_TILE, (c + 1) * TIME_TILE) for c in range(n_chunks)]
        xs = [x_ref[r, :].astype(BF16) for r in rows]
        hgs = [jnp.dot(x, wg16[...], preferred_element_type=F32) for x in xs]
        hus = [jnp.dot(x, wu16[...], preferred_element_type=F32) for x in xs]
        hids = [((hg * jax.nn.sigmoid(hg)) * hu).astype(BF16) for hg, hu in zip(hgs, hus)]
        for r, hid in zip(rows, hids):
            o_ref[r, :] = jnp.dot(hid, wd16[...], preferred_element_type=F32)

    @pl.when(pl.program_id(0) >= nb_ref[0])
    def _():
        o_ref[...] = jnp.zeros_like(o_ref)


def _moe_ffn(xs, block_expert, nb_used, w_gate, w_up, w_down):
    P, D = xs.shape
    nb = P // FFN_BLOCK
    wspec = lambda s: pl.BlockSpec((None,) + s, lambda b, be, nbu: (be[b], 0, 0))
    return pl.pallas_call(
        _ffn_kernel,
        grid_spec=pltpu.PrefetchScalarGridSpec(
            num_scalar_prefetch=2,
            grid=(nb,),
            in_specs=[pl.BlockSpec((FFN_BLOCK, D), lambda b, be, nbu: (b, 0)),
                      wspec((D, EXPERT_FF)), wspec((D, EXPERT_FF)), wspec((EXPERT_FF, D))],
            out_specs=pl.BlockSpec((FFN_BLOCK, D), lambda b, be, nbu: (b, 0)),
            scratch_shapes=[pltpu.VMEM((D, EXPERT_FF), BF16), pltpu.VMEM((D, EXPERT_FF), BF16),
                            pltpu.VMEM((EXPERT_FF, D), BF16)],
        ),
        out_shape=jax.ShapeDtypeStruct((P, D), F32),
        compiler_params=_cparams("arbitrary"),
        name="moe_expert_ffn",
    )(block_expert, nb_used, xs, w_gate, w_up, w_down)


def _combine_kernel(idx_hbm, ys_hbm, h_ref, rec_ref, g_ref, b_ref, o_ref, idx_smem, ybuf, isems, gsems,
                    *, tp, t_real):
    t = pl.program_id(0)
    nt = pl.num_programs(0)
    tm = MOE_BLOCK

    def issue_gathers(u):
        dst = ybuf.at[u % 2]
        for r in range(tm):
            for k in range(MOE_TOPK):
                _row_copy(ys_hbm, _slot(idx_smem, u, k, r), dst.at[k], r, gsems.at[u % 2]).start(priority=k % 2)

    @pl.when(t == 0)
    def _():
        _route_copy(idx_hbm, idx_smem, isems, t).start()
        _route_copy(idx_hbm, idx_smem, isems, t).wait()
        issue_gathers(t)

        @pl.when(nt > 1)
        def _():
            _route_copy(idx_hbm, idx_smem, isems, t + 1).start()

    @pl.when(t + 1 < nt)
    def _():
        _route_copy(idx_hbm, idx_smem, isems, t + 1).wait()
        issue_gathers(t + 1)

    @pl.when(t + 2 < nt)
    def _():
        _route_copy(idx_hbm, idx_smem, isems, t + 2).start()

    for _ in range(tm * MOE_TOPK):
        _row_copy(ys_hbm, 0, ybuf.at[0, 0], 0, gsems.at[t % 2]).wait()
    b = t % 2
    rec = rec_ref[...]
    y = DN_ALPHA * h_ref[...] + rec[:, R_G1:R_G1 + 1] * ybuf[b, 0] + rec[:, R_G2:R_G2 + 1] * ybuf[b, 1]
    out = _layer_norm_rows(y, g_ref[...], b_ref[...])
    n = (t * tm + lax.broadcasted_iota(I32, out.shape, 0)).astype(F32)
    p = n - jnp.floor((n + 0.5) / tp) * tp
    o_ref[...] = jnp.where((p >= FRONT) & (p < FRONT + t_real), out, 0.0)


def _moe_combine(hf, ys, idx, rec, ln_g, ln_b, tp, t_real):
    N, D = hf.shape
    nt = N // MOE_BLOCK
    g = ln_g.reshape(1, D).astype(F32)
    b = ln_b.reshape(1, D).astype(F32)
    anyspec = pl.BlockSpec(memory_space=pl.ANY)
    return pl.pallas_call(
        functools.partial(_combine_kernel, tp=tp, t_real=t_real),
        grid=(nt,),
        in_specs=[anyspec, anyspec,
                  pl.BlockSpec((MOE_BLOCK, D), lambda t: (t, 0)),
                  pl.BlockSpec((MOE_BLOCK, LANES), lambda t: (t, 0)),
                  _const_spec((1, D)), _const_spec((1, D))],
        out_specs=pl.BlockSpec((MOE_BLOCK, D), lambda t: (t, 0)),
        scratch_shapes=[pltpu.SMEM((2, MOE_TOPK * MOE_BLOCK), I32),
                        pltpu.VMEM((2, MOE_TOPK, MOE_BLOCK, D), F32),
                        pltpu.SemaphoreType.DMA((2,)), pltpu.SemaphoreType.DMA((2,))],
        out_shape=jax.ShapeDtypeStruct((N, D), F32),
        compiler_params=_cparams("arbitrary"),
        name="moe_combine_ln",
    )(idx, ys, hf, rec, g, b)


def _moe_layer(h, group_w, group_b, expert_w, expert_b, w_gate, w_up, w_down, ln_g, ln_b, t_real):
    B, TP, D = h.shape
    hf = h.reshape(B * TP, D)
    N = B * TP
    rec, rect, cnt = _moe_route(hf, group_w, group_b, expert_w, expert_b)
    counts = cnt[0, :N_EXPERTS].astype(I32)
    padded = (counts + FFN_BLOCK - 1) // FFN_BLOCK * FFN_BLOCK
    pend = jnp.cumsum(padded)
    pstart = (pend - padded).astype(I32)
    n_blocks = -(-(N * MOE_TOPK) // FFN_BLOCK) + N_EXPERTS
    block_start = jnp.arange(n_blocks, dtype=I32) * FFN_BLOCK
    block_expert = jnp.minimum(jnp.sum((pend[None, :] <= block_start[:, None]).astype(I32), axis=1), N_EXPERTS - 1)
    nb_used = (pend[-1:] // FFN_BLOCK).astype(I32)
    n_slots = n_blocks * FFN_BLOCK
    slots = _moe_slots(rect, pstart)
    idx = slots[:, :MOE_TOPK, :].reshape(slots.shape[0], MOE_TOPK, slots.shape[2] // MOE_BLOCK, MOE_BLOCK)
    idx = idx.transpose(0, 2, 1, 3).reshape(N // MOE_BLOCK, MOE_TOPK * MOE_BLOCK)
    xs = _moe_dispatch(hf, idx, n_slots)
    ys = _moe_ffn(xs, block_expert, nb_used, w_gate, w_up, w_down)
    out = _moe_combine(hf, ys, idx, rec, ln_g, ln_b, TP, t_real)
    return out.reshape(B, TP, D)


def _even_layer(h, w_in, kv_norm, w_uk, w_uv, conv_w, conv_b, rg_wa, rg_ba, rg_wx, rg_bx, rg_lambda, w_out,
                ln_g, ln_b, t_real):
    q, qi, ki, wi, gate, xb, kv = _even_in_proj(h, w_in, kv_norm, w_uk, w_uv)
    attn = _dsa(q, qi, wi, ki, kv, t_real)
    rec = _griffin(xb, gate, conv_w, conv_b, rg_wa, rg_ba, rg_wx, rg_bx, rg_lambda, t_real)
    return _out_ln(h, [attn, rec], [w_out[:A_OUT], w_out[A_OUT:]], ln_g, ln_b, t_real)


def _odd_in_kernel(h_ref, wq_ref, wk_ref, wv_ref, wz_ref, wa_ref, wb_ref, q_ref, k_ref, v_ref, z_ref, a_ref, b_ref):
    a = h_ref[0].astype(BF16)
    for w_ref, o_ref in ((wq_ref, q_ref), (wk_ref, k_ref), (wv_ref, v_ref), (wz_ref, z_ref), (wa_ref, a_ref),
                         (wb_ref, b_ref)):
        o_ref[0] = jnp.dot(a, w_ref[...], preferred_element_type=F32).astype(o_ref.dtype)


def _odd_in_proj(h, w_in):
    B, TP, D = h.shape
    tm = _row_tile(TP)
    wb = w_in.astype(BF16)
    o = [0, GDN_QK_WIDTH, 2 * GDN_QK_WIDTH, 2 * GDN_QK_WIDTH + GDN_V_WIDTH, 2 * GDN_QK_WIDTH + 2 * GDN_V_WIDTH]
    o += [o[-1] + GDN_V_HEADS, o[-1] + 2 * GDN_V_HEADS]
    ws = [wb[:, o[i]:o[i + 1]] for i in range(4)]
    ws += [jnp.pad(wb[:, o[i]:o[i + 1]], ((0, 0), (0, LANES - GDN_V_HEADS))) for i in (4, 5)]
    widths = [GDN_QK_WIDTH, GDN_QK_WIDTH, GDN_V_WIDTH, GDN_V_WIDTH, LANES, LANES]
    return pl.pallas_call(
        _odd_in_kernel,
        grid=(B, TP // tm),
        in_specs=[pl.BlockSpec((1, tm, D), lambda b, j: (b, j, 0))] + [_const_spec(w.shape) for w in ws],
        out_specs=[pl.BlockSpec((1, tm, n), lambda b, j: (b, j, 0)) for n in widths],
        out_shape=[jax.ShapeDtypeStruct((B, TP, n), BF16 if n > LANES else F32) for n in widths],
        compiler_params=_cparams("parallel", "parallel"),
        name="odd_in_proj",
    )(h, *ws)


def _conv_silu(x, win_ref, cw):
    y = _causal_conv(x, win_ref, cw)
    return y * jax.nn.sigmoid(y)


def _gdn_prep_kernel(q_ref, k_ref, v_ref, a_ref, b_ref, cwq_ref, cwk_ref, cwv_ref, alog_ref, dtb_ref,
                     u_ref, w_ref, qg_ref, kg_ref, at_ref, e_ref,
                     tq_ref, tk_ref, tv_ref, qs_ref, ks_ref, vs_ref, *, t_real):
    j = pl.program_id(1)
    tt = q_ref.shape[1]
    hd = GDN_HEAD_DIM
    half = CHUNK

    @pl.when(j == 0)
    def _():
        tq_ref[...] = jnp.zeros_like(tq_ref)
        tk_ref[...] = jnp.zeros_like(tk_ref)
        tv_ref[...] = jnp.zeros_like(tv_ref)

    rows1 = j * tt + lax.broadcasted_iota(I32, (tt, 1), 0)
    valid = _valid_rows(rows1, t_real)
    q = jnp.where(valid, _conv_silu(q_ref[0].astype(F32), tq_ref, cwq_ref[...]), 0.0)
    k = jnp.where(valid, _conv_silu(k_ref[0].astype(F32), tk_ref, cwk_ref[...]), 0.0)
    vs_ref[...] = jnp.where(valid, _conv_silu(v_ref[0].astype(F32), tv_ref, cwv_ref[...]), 0.0)
    for h in range(GDN_K_HEADS):
        sl = slice(h * hd, (h + 1) * hd)
        qh, kh = q[:, sl], k[:, sl]
        qs_ref[:, sl] = qh * lax.rsqrt(jnp.sum(qh * qh, axis=1, keepdims=True) + 1e-6) * (hd ** -0.5)
        ks_ref[:, sl] = kh * lax.rsqrt(jnp.sum(kh * kh, axis=1, keepdims=True) + 1e-6)

    lane = lax.broadcasted_iota(I32, (tt, LANES), 1)
    rowi = lax.broadcasted_iota(I32, (tt, LANES), 0)
    live = valid & (lane < GDN_V_HEADS)
    g = jnp.where(live, -jnp.exp(alog_ref[...]) * _softplus(a_ref[0] + dtb_ref[...]), 0.0)
    beta = jnp.where(live, jax.nn.sigmoid(b_ref[0]), 0.0)
    gc = g
    s = 1
    while s < CHUNK:
        gc = gc + jnp.where((rowi % CHUNK) >= s, pltpu.roll(gc, s, 0), 0.0)
        s *= 2
    gc_next = pltpu.roll(gc, LANES - 1, 1)
    n_ch = tt // CHUNK
    e_rows = []
    for c in range(n_ch):
        e_rows.append(jnp.broadcast_to(jnp.exp(gc[(c + 1) * CHUNK - 1:(c + 1) * CHUNK, :]), (CHUNK, LANES)))
    e_ref[0] = jnp.concatenate(e_rows, axis=0)

    nt = (((1,), (1,)), ((), ()))
    l2 = lax.broadcasted_iota(I32, (CHUNK, 2 * half), 1)
    ii = lax.broadcasted_iota(I32, (CHUNK, 2 * half), 0)
    jj = l2 % half
    left = l2 < half
    r2 = lax.broadcasted_iota(I32, (2 * half, 2 * half), 0)
    c2 = lax.broadcasted_iota(I32, (2 * half, 2 * half), 1)
    eye2 = jnp.where(r2 == c2, 1.0, 0.0)
    bdot = lambda x, y: jnp.dot(x.astype(BF16), y.astype(BF16), preferred_element_type=F32)
    pair_group = GDN_K_HEADS
    for c, g0 in [(c, g0) for c in range(n_ch) for g0 in range(0, GDN_K_HEADS, pair_group)]:
        rs = slice(c * CHUNK, (c + 1) * CHUNK)
        gc_c, beta_c = gc[rs], beta[rs]
        gt = jnp.concatenate([gc_c, gc_next[rs]], axis=0).T
        glast = gc_c[CHUNK - 1:CHUNK, :]
        pairs = range(g0, g0 + pair_group)
        ms, tinvs, rhss = [], [], []
        for pr in pairs:
            ha, hb = 2 * pr, 2 * pr + 1
            ksl = slice(pr * hd, (pr + 1) * hd)
            osl = slice(ha * hd, (hb + 1) * hd)
            kc = ks_ref[rs, ksl]
            qc = qs_ref[rs, ksl]
            kc16 = kc.astype(BF16)
            k2 = jnp.concatenate([kc16, kc16], axis=0)
            kk2 = lax.dot_general(kc16, k2, nt, preferred_element_type=F32)
            qk2 = lax.dot_general(qc.astype(BF16), k2, nt, preferred_element_type=F32)
            gca, gcb = gc_c[:, ha:ha + 1], gc_c[:, hb:hb + 1]
            ba, bb = beta_c[:, ha:ha + 1], beta_c[:, hb:hb + 1]
            gcol2 = jnp.where(left, gca, gcb)
            bcol2 = jnp.where(left, ba, bb)
            decay2 = jnp.exp(jnp.where(ii >= jj, gcol2 - gt[ha:ha + 1, :], -jnp.inf))
            m2 = jnp.where(ii > jj, bcol2 * kk2 * decay2, 0.0)
            at_ref[0, rs, pr * 2 * half:(pr + 1) * 2 * half] = (qk2 * decay2).astype(BF16)
            m = jnp.concatenate([jnp.where(left, m2, 0.0), jnp.where(left, 0.0, m2)], axis=0)
            ms.append(m)
            tinvs.append(eye2 - m)
            ega, egb = jnp.exp(gca), jnp.exp(gcb)
            va = vs_ref[rs, ha * hd:(ha + 1) * hd]
            vb = vs_ref[rs, hb * hd:(hb + 1) * hd]
            rhss.append(jnp.concatenate([jnp.concatenate([va * ba, kc * (ba * ega)], axis=1),
                                         jnp.concatenate([vb * bb, kc * (bb * egb)], axis=1)],
                                        axis=0).astype(BF16))
            qg_ref[0, rs, osl] = jnp.concatenate([qc * ega, qc * egb], axis=1).astype(BF16)
            kg_ref[0, rs, osl] = jnp.concatenate([kc * jnp.exp(glast[:, ha:ha + 1] - gca),
                                                  kc * jnp.exp(glast[:, hb:hb + 1] - gcb)], axis=1).astype(BF16)
        pws = [bdot(m, m) for m in ms]
        n_fac = CHUNK.bit_length() - 2
        for f in range(n_fac):
            if f + 1 < n_fac:
                both = [bdot(jnp.concatenate([t, p], axis=0), p) for t, p in zip(tinvs, pws)]
                tinvs = [t + r[:2 * half] for t, r in zip(tinvs, both)]
                pws = [r[2 * half:] for r in both]
            else:
                tinvs = [t + bdot(t, p) for t, p in zip(tinvs, pws)]
        sols = [jnp.dot(t.astype(BF16), r, preferred_element_type=F32) for t, r in zip(tinvs, rhss)]
        for pr, sol in zip(pairs, sols):
            osl = slice(2 * pr * hd, (2 * pr + 2) * hd)
            u_ref[0, rs, osl] = jnp.concatenate([sol[:CHUNK, :hd], sol[CHUNK:, :hd]], axis=1)
            w_ref[0, rs, osl] = jnp.concatenate([sol[:CHUNK, hd:], sol[CHUNK:, hd:]], axis=1).astype(BF16)


def _gdn_prep(q, k, v, a, b, conv_w, a_log, dt_bias, t_real):
    B, TP, _ = q.shape
    tt = TIME_TILE
    cw = jnp.pad(conv_w.astype(F32), ((0, SUBLANES - CONV_WIDTH), (0, 0)))
    cwq, cwk, cwv = cw[:, :GDN_QK_WIDTH], cw[:, GDN_QK_WIDTH:2 * GDN_QK_WIDTH], cw[:, 2 * GDN_QK_WIDTH:]
    alog = jnp.pad(a_log.astype(F32), (0, LANES - GDN_V_HEADS)).reshape(1, LANES)
    dtb = jnp.pad(dt_bias.astype(F32), (0, LANES - GDN_V_HEADS)).reshape(1, LANES)
    tspec = lambda n: pl.BlockSpec((1, tt, n), lambda bb, j: (bb, j, 0))
    ws = (cwq, cwk, cwv, alog, dtb)
    VW, QW = GDN_V_WIDTH, GDN_QK_WIDTH
    outs = [(VW, F32), (VW, BF16), (VW, BF16), (VW, BF16), (GDN_K_HEADS * 2 * CHUNK, BF16), (LANES, F32)]
    return pl.pallas_call(
        functools.partial(_gdn_prep_kernel, t_real=t_real),
        grid=(B, TP // tt),
        in_specs=[tspec(QW), tspec(QW), tspec(VW), tspec(LANES), tspec(LANES)] + [_const_spec(w.shape) for w in ws],
        out_specs=[tspec(n) for n, _ in outs],
        out_shape=[jax.ShapeDtypeStruct((B, TP, n), dt) for n, dt in outs],
        scratch_shapes=[pltpu.VMEM((SUBLANES + tt, QW), F32), pltpu.VMEM((SUBLANES + tt, QW), F32),
                        pltpu.VMEM((SUBLANES + tt, VW), F32), pltpu.VMEM((tt, QW), F32), pltpu.VMEM((tt, QW), F32),
                        pltpu.VMEM((tt, VW), F32)],
        compiler_params=_cparams("parallel", "arbitrary"),
        name="gdn_prep",
    )(q, k, v, a, b, *ws)


def _gdn_scan_kernel(u_ref, w_ref, qg_ref, kg_ref, at_ref, e_ref, z_ref, on_ref, o_ref, s_ref):
    j = pl.program_id(1)
    tt = u_ref.shape[1]
    hd = GDN_HEAD_DIM

    @pl.when(j == 0)
    def _():
        s_ref[...] = jnp.zeros_like(s_ref)

    lane = lax.broadcasted_iota(I32, (CHUNK, 2 * CHUNK), 1)
    tn = (((0,), (0,)), ((), ()))
    heads = range(GDN_V_HEADS)
    hsl = [slice(h * hd, (h + 1) * hd) for h in heads]
    for c in range(tt // CHUNK):
        rs = slice(c * CHUNK, (c + 1) * CHUNK)
        e_row = e_ref[0, c * CHUNK:c * CHUNK + 1, :]
        ss = [s_ref[h] for h in heads]
        ws = [jnp.dot(jnp.concatenate([w_ref[0, rs, hsl[h]], qg_ref[0, rs, hsl[h]]], axis=0), ss[h].astype(BF16),
                      preferred_element_type=F32) for h in heads]
        v16 = [(u_ref[0, rs, hsl[h]] - ws[h][:CHUNK]).astype(BF16) for h in heads]
        intra = []
        for pr in range(GDN_K_HEADS):
            at2 = at_ref[0, rs, pr * 2 * CHUNK:(pr + 1) * 2 * CHUNK]
            zero = jnp.zeros_like(at2)
            lhs = jnp.concatenate([jnp.where(lane < CHUNK, at2, zero), jnp.where(lane >= CHUNK, at2, zero)], axis=0)
            rhs = jnp.concatenate([v16[2 * pr], v16[2 * pr + 1]], axis=0)
            intra.append(jnp.dot(lhs, rhs, preferred_element_type=F32))
        for h in heads:
            s_ref[h] = ss[h] * e_row[:, h:h + 1] + lax.dot_general(kg_ref[0, rs, hsl[h]], v16[h], tn,
                                                                   preferred_element_type=F32)
        for h in heads:
            o = ws[h][CHUNK:] + intra[h // 2][(h % 2) * CHUNK:(h % 2 + 1) * CHUNK]
            on = o * lax.rsqrt(jnp.mean(o * o, axis=1, keepdims=True) + 1e-6) * on_ref[...]
            z = z_ref[0, rs, hsl[h]].astype(F32)
            o_ref[0, rs, hsl[h]] = (on * (z * jax.nn.sigmoid(z))).astype(BF16)


def _gdn_scan(u, w, qg, kg, at, e, z, o_norm):
    B, TP, VW = u.shape
    tt = TIME_TILE
    tspec = lambda n: pl.BlockSpec((1, tt, n), lambda bb, j: (bb, j, 0))
    on = o_norm.reshape(1, GDN_HEAD_DIM).astype(F32)
    return pl.pallas_call(
        _gdn_scan_kernel,
        grid=(B, TP // tt),
        in_specs=[tspec(VW), tspec(VW), tspec(VW), tspec(VW), tspec(at.shape[-1]), tspec(LANES), tspec(VW),
                  _const_spec(on.shape)],
        out_specs=tspec(VW),
        out_shape=jax.ShapeDtypeStruct((B, TP, VW), BF16),
        scratch_shapes=[pltpu.VMEM((GDN_V_HEADS, GDN_HEAD_DIM, GDN_HEAD_DIM), F32)],
        compiler_params=_cparams("parallel", "arbitrary"),
        name="gdn_scan",
    )(u, w, qg, kg, at, e, z, on)


def _odd_layer(h, w_in, conv_w, a_log, dt_bias, o_norm, w_out, ln_g, ln_b, t_real):
    q, k, v, z, a, b = _odd_in_proj(h, w_in)
    u, w, qg, kg, at, e = _gdn_prep(q, k, v, a, b, conv_w, a_log, dt_bias, t_real)
    gated = _gdn_scan(u, w, qg, kg, at, e, z, o_norm)
    return _out_ln(h, [gated], [w_out], ln_g, ln_b, t_real)


def kernel(x, meta_tokens, even_w_in, even_kv_norm, even_w_uk, even_w_uv, even_conv_w, even_conv_b, even_rg_wa,
           even_rg_ba, even_rg_wx, even_rg_bx, even_rg_lambda, even_w_out, odd_w_in, odd_conv_w, odd_a_log,
           odd_dt_bias, odd_o_norm, odd_w_out, ln_g, ln_b, moe_group_w, moe_group_b, moe_expert_w, moe_expert_b,
           moe_w_gate, moe_w_up, moe_w_down):
    B, S, D = x.shape
    t_real = N_META + S
    tp = -(-(FRONT + t_real) // TIME_TILE) * TIME_TILE
    meta = jnp.broadcast_to(meta_tokens.astype(x.dtype)[None], (B, N_META, D))
    h = jnp.concatenate([jnp.zeros((B, FRONT, D), x.dtype), meta, x,
                         jnp.zeros((B, tp - FRONT - t_real, D), x.dtype)], axis=1)
    for layer in range(DEPTH):
        i = layer // 2
        if layer % 2 == 0:
            h = _even_layer(h, even_w_in[i], even_kv_norm[i], even_w_uk[i], even_w_uv[i], even_conv_w[i],
                            even_conv_b[i], even_rg_wa[i], even_rg_ba[i], even_rg_wx[i], even_rg_bx[i],
                            even_rg_lambda[i], even_w_out[i], ln_g[layer, 0], ln_b[layer, 0], t_real)
        else:
            h = _odd_layer(h, odd_w_in[i], odd_conv_w[i], odd_a_log[i], odd_dt_bias[i], odd_o_norm[i],
                           odd_w_out[i], ln_g[layer, 0], ln_b[layer, 0], t_real)
        h = _moe_layer(h, moe_group_w[layer], moe_group_b[layer], moe_expert_w[layer], moe_expert_b[layer],
                       moe_w_gate[layer], moe_w_up[layer], moe_w_down[layer], ln_g[layer, 1], ln_b[layer, 1], t_real)
    return h[:, FRONT:FRONT + t_real][:, N_META:]
```
